```python
import jax, jax.numpy as jnp
from jax import lax
import numpy as np

D_MODEL = 1024
BATCH = 8
SEQ = 2048
DEPTH = 4
DEC_BATCH = 128
DEC_SEQ = 4
PAST_LEN = 2048
PAGE_SIZE = 128

HEAD_DIM = 64
H_FOX = 4
FOX_BIAS_MEAN = 3.0
H_DSA = 4
H_IDX = 4
D_IDX = 32
DSA_TOPK = 256
H_RWKV = 4
RWKV_N = HEAD_DIM
LORA_W = 64
LORA_A = 64
LORA_G = 128
RWKV_GN_EPS = 64e-5
H_HGRN = 4
HGRN_DK = 64
HGRN_DV = 64
HGRN_CHUNK = 64
RMS_EPS = 1e-6
LB_TINY = 1e-30
W_FOX = H_FOX * HEAD_DIM
W_DSA = H_DSA * HEAD_DIM
W_RWKV = H_RWKV * RWKV_N
W_HGRN = H_HGRN * HGRN_DV
W_BRANCH = W_FOX
N_BRANCH = 4
P_RWKV = 3 * W_RWKV + LORA_W + LORA_A + LORA_G
RWKV_SPLIT_IDX = (W_RWKV, 2 * W_RWKV, 3 * W_RWKV, 3 * W_RWKV + LORA_W, 3 * W_RWKV + LORA_W + LORA_A)
ROPE_THETA = 500000.0
ROPE_FRAC = 4
D_FF = 2816
CONV_W = 3
Q_BLOCK = 128
NEG = -1e30
ALPHA = (2 * DEPTH) ** 0.25
BETA = (8 * DEPTH) ** -0.25
LN_EPS = 1e-5

IN_SPLITS = (
    ('fox_q', W_FOX), ('fox_k', W_FOX), ('fox_v', W_FOX), ('fox_f', H_FOX),
    ('dsa_q', W_DSA), ('dsa_k', W_DSA), ('dsa_v', W_DSA),
    ('idx_q', H_IDX * D_IDX), ('idx_k', D_IDX), ('idx_w', H_IDX),
    ('rwkv', P_RWKV),
    ('hgrn_q', H_HGRN * HGRN_DK), ('hgrn_f', H_HGRN * HGRN_DK), ('hgrn_i', W_HGRN), ('hgrn_g', W_HGRN),
    ('gate', N_BRANCH * D_MODEL),
)
D_IN = sum(w for _, w in IN_SPLITS)

kernel_name = 'hybrid_fox_dsa_rwkv7_hgrn2_step'

F32 = jnp.float32


def _split_cols(z):
    out = {}
    o = 0
    for name, w in IN_SPLITS:
        out[name] = z[..., o:o + w]
        o += w
    return out


def _layer_norm(x, g, b):
    xf = x.astype(F32)
    mu = jnp.mean(xf, -1, keepdims=True)
    var = jnp.mean(jnp.square(xf - mu), -1, keepdims=True)
    return ((xf - mu) * lax.rsqrt(var + LN_EPS) * g + b).astype(x.dtype)


def _rope(x, pos):
    d = x.shape[-1]
    r = d // ROPE_FRAC
    half = r // 2
    inv = jnp.power(ROPE_THETA, -jnp.arange(half, dtype=F32) / half)
    ang = pos.astype(F32)[:, None] * inv[None]
    cos = jnp.cos(ang)[:, None, :]
    sin = jnp.sin(ang)[:, None, :]
    x1 = x[..., :half].astype(F32)
    x2 = x[..., half:r].astype(F32)
    return jnp.concatenate([(x1 * cos - x2 * sin).astype(x.dtype),
                            (x2 * cos + x1 * sin).astype(x.dtype), x[..., r:]], axis=-1)


def _query_blocks(fn, *qargs):
    t = qargs[0].shape[1]
    if t <= Q_BLOCK or t % Q_BLOCK != 0:
        return fn(*qargs)
    nb = t // Q_BLOCK

    def split(a):
        return jnp.moveaxis(a.reshape(a.shape[0], nb, Q_BLOCK, *a.shape[2:]), 1, 0)

    out = lax.map(lambda args: fn(*args), tuple(split(a) for a in qargs))
    out = jnp.moveaxis(out, 0, 1)
    return out.reshape(out.shape[0], t, *out.shape[3:])


def _fox_attend(q, cq, qpos, k, v, ck):
    s = jnp.einsum('bqhd,bkhd->bhqk', q, k).astype(F32) * HEAD_DIM ** -0.5
    s = s + jnp.transpose(cq, (0, 2, 1))[..., None] - jnp.transpose(ck, (0, 2, 1))[:, :, None, :]
    causal = jnp.arange(k.shape[1])[None, :] <= qpos[0][:, None]
    s = jnp.where(causal, s, NEG)
    p = jax.nn.softmax(s, axis=-1)
    return jnp.einsum('bhqk,bkhd->bqhd', p.astype(v.dtype), v)


def _dsa_attend(q, qi, wi, qpos, k, v, ki, topk):
    qp = qpos[0]
    isc = jnp.einsum('bqhd,bkd->bqhk', qi, ki).astype(F32) * D_IDX ** -0.5
    isc = jnp.einsum('bqhk,bqh->bqk', jax.nn.relu(isc), wi.astype(F32))
    causal = jnp.arange(k.shape[1])[None, :] <= qp[:, None]
    isc = jnp.where(causal, isc, NEG)
    _, idx = lax.top_k(isc, topk)
    take = jax.vmap(lambda a, i: a[i])
    kg = take(k, idx)
    vg = take(v, idx)
    s = jnp.einsum('bqhd,bqjhd->bhqj', q, kg).astype(F32) * HEAD_DIM ** -0.5
    valid = idx <= qp[:, None]
    s = jnp.where(valid[:, None], s, NEG)
    p = jax.nn.softmax(s, axis=-1)
    return jnp.einsum('bhqj,bqjhd->bqhd', p.astype(vg.dtype), vg)


def _rwkv7(pr, shift_prev, s0, p):
    B, T, _ = pr.shape
    prev = jnp.concatenate([shift_prev[:, None].astype(pr.dtype), pr[:, :-1]], axis=1)
    xs = pr + (prev - pr) * p['rwkv_mu']
    r, k, v, wl, al, gl = jnp.split(xs, RWKV_SPLIT_IDX, axis=-1)
    w_log = -jax.nn.softplus(-(p['rwkv_w0'] + jnp.tanh(wl) @ p['rwkv_w2']).astype(F32)) - 0.5
    decay = jnp.exp(-jnp.exp(w_log))
    a = jax.nn.sigmoid((p['rwkv_a0'] + al @ p['rwkv_a2']).astype(F32))
    g = (jax.nn.sigmoid(gl) @ p['rwkv_g2']).astype(F32)

    def heads(t):
        return t.reshape(B, T, H_RWKV, RWKV_N).astype(F32)

    r_, k_, v_, w_, a_ = heads(r), heads(k), heads(v), heads(decay), heads(a)
    kk = k_ * p['rwkv_kk'].reshape(H_RWKV, RWKV_N).astype(F32)
    kk = kk / jnp.maximum(jnp.sqrt(jnp.sum(kk * kk, -1, keepdims=True)), 1e-12)
    k_ = k_ * (1.0 + (a_ - 1.0) * p['rwkv_ka'].reshape(H_RWKV, RWKV_N).astype(F32))

    def step(S, inp):
        rt, wt, kt, vt, kkt, at = inp
        sk = jnp.einsum('bhvk,bhk->bhv', S, kkt)
        S = (S * wt[:, :, None, :] - sk[..., None] * (at * kkt)[:, :, None, :]
             + vt[..., None] * kt[:, :, None, :])
        return S, jnp.einsum('bhvk,bhk->bhv', S, rt)

    tm = lambda t: jnp.moveaxis(t, 1, 0)
    S, y = lax.scan(step, s0.astype(F32), (tm(r_), tm(w_), tm(k_), tm(v_), tm(kk), tm(a_)))
    y = jnp.moveaxis(y, 0, 1)
    mu = jnp.mean(y, -1, keepdims=True)
    var = jnp.mean(jnp.square(y - mu), -1, keepdims=True)
    y = ((y - mu) * lax.rsqrt(var + RWKV_GN_EPS)).reshape(B, T, W_RWKV)
    y = y * p['rwkv_ln_w'] + p['rwkv_ln_b']
    bonus = jnp.sum(r_ * k_ * p['rwkv_rk'].reshape(H_RWKV, RWKV_N).astype(F32), -1, keepdims=True) * v_
    y = (y + bonus.reshape(B, T, W_RWKV)) * g
    return y.astype(pr.dtype), S, pr[:, -1]


def _gla_chunked(q, k, v, g, s0):
    B, T, H, K = q.shape
    c = min(HGRN_CHUNK, T)
    n = -(-T // c)
    pad = n * c - T

    def chunks(a):
        a = jnp.pad(a, ((0, 0), (0, pad)) + ((0, 0),) * (a.ndim - 2))
        return jnp.moveaxis(a.reshape(B, n, c, *a.shape[2:]), 1, 0)

    tri = jnp.tril(jnp.ones((c, c), bool))[None, :, :, None, None]

    def step(S, inp):
        qc, kc, vc, gc = inp
        b = jnp.cumsum(gc, axis=1)
        dec = jnp.exp(jnp.where(tri, b[:, :, None] - b[:, None], NEG))
        att = jnp.einsum('btshk,bshk->bhts', qc[:, :, None] * dec, kc)
        o = (jnp.einsum('bhts,bshv->bthv', att, vc)
             + jnp.einsum('bthk,bhkv->bthv', qc * jnp.exp(b), S))
        bl = b[:, -1]
        S = S * jnp.exp(bl)[..., None] + jnp.einsum('bshk,bshv->bhkv', kc * jnp.exp(bl[:, None] - b), vc)
        return S, o

    S, o = lax.scan(step, s0, (chunks(q), chunks(k), chunks(v), chunks(g)))
    o = jnp.moveaxis(o, 0, 1).reshape(B, n * c, H, -1)[:, :T]
    return o, S


def _hgrn2(hq, hf, hi, hg, s0, p):
    B, T, _ = hq.shape
    q = jax.nn.silu(hq.astype(F32)).reshape(B, T, H_HGRN, HGRN_DK)
    fr = hf.astype(F32).reshape(B, T, H_HGRN, HGRN_DK)
    lb = p['hgrn_lb'].reshape(H_HGRN, HGRN_DK)
    ls = jax.nn.log_sigmoid(fr)
    lb_log = jnp.log(jnp.maximum(lb, LB_TINY))
    logf = jnp.where(lb > 0, jnp.logaddexp(lb_log, jnp.log1p(-lb) + ls), ls)
    kin = (1.0 - lb) * jax.nn.sigmoid(-fr)
    vin = hi.astype(F32).reshape(B, T, H_HGRN, HGRN_DV)
    o, S = _gla_chunked(q, kin, vin, logf, s0.astype(F32))
    o = o * lax.rsqrt(jnp.mean(o * o, -1, keepdims=True) + RMS_EPS)
    o = o.reshape(B, T, W_HGRN) * p['hgrn_norm_w'] * jax.nn.silu(hg.astype(F32))
    return o.astype(hq.dtype), S


def _conv_ffn(h, buf, p):
    T = h.shape[1]
    a, gt = jnp.split(h @ p['ffn_w_in'], 2, axis=-1)
    ext = jnp.concatenate([buf.astype(a.dtype), a], axis=1)
    conv = p['ffn_conv_b'] + ext[:, 0:T] * p['ffn_conv_w'][0]
    for j in range(1, CONV_W):
        conv = conv + ext[:, j:j + T] * p['ffn_conv_w'][j]
    hid = jax.nn.gelu(conv, approximate=False) * gt
    return hid @ p['ffn_w_out'], ext[:, T:]


def _layer(x, past_len, past, p):
    B, T, _ = x.shape
    dt = x.dtype
    pos = past_len + jnp.arange(T)
    qpos = pos[None]
    c = _split_cols(x @ p['w_in'])
    hd = lambda t, h: t.reshape(B, T, h, -1)

    qa, ka, va = hd(c['fox_q'], H_FOX), hd(c['fox_k'], H_FOX), hd(c['fox_v'], H_FOX)
    logf = jax.nn.log_sigmoid(c['fox_f'].astype(F32) + p['fox_bf'])
    ka_all = jnp.concatenate([past['fox_k'].astype(dt), ka], axis=1)
    va_all = jnp.concatenate([past['fox_v'].astype(dt), va], axis=1)
    cum = jnp.cumsum(jnp.concatenate([past['fox_logf'].astype(F32), logf], axis=1), axis=1)
    o_a = _query_blocks(lambda q_, c_, p_: _fox_attend(q_, c_, p_, ka_all, va_all, cum),
                        qa, cum[:, past_len:], qpos)

    qb = _rope(hd(c['dsa_q'], H_DSA), pos)
    kb = _rope(hd(c['dsa_k'], H_DSA), pos)
    vb = hd(c['dsa_v'], H_DSA)
    qi = _rope(hd(c['idx_q'], H_IDX), pos)
    ki = _rope(c['idx_k'][:, :, None], pos)[:, :, 0]
    wi = c['idx_w'] * H_IDX ** -0.5
    kb_all = jnp.concatenate([past['dsa_k'].astype(dt), kb], axis=1)
    vb_all = jnp.concatenate([past['dsa_v'].astype(dt), vb], axis=1)
    ki_all = jnp.concatenate([past['dsa_kidx'].astype(dt), ki], axis=1)
    topk = max(1, min(DSA_TOPK, kb_all.shape[1] // 4))
    o_b = _query_blocks(lambda q_, qi_, wi_, p_: _dsa_attend(q_, qi_, wi_, p_, kb_all, vb_all, ki_all, topk),
                        qb, qi, wi, qpos)

    o_c, s_rwkv, shift = _rwkv7(c['rwkv'], past['shift'], past['rwkv'], p)

    o_d, s_hgrn = _hgrn2(c['hgrn_q'], c['hgrn_f'], c['hgrn_i'], c['hgrn_g'], past['hgrn'], p)

    ob = jnp.stack([o_a.reshape(B, T, W_BRANCH), o_b.reshape(B, T, W_BRANCH), o_c, o_d], axis=2)
    gates = jax.nn.sigmoid(c['gate'].astype(F32)).reshape(B, T, N_BRANCH, D_MODEL).astype(dt)
    proj = jnp.einsum('btnc,ncd->btnd', ob, p['w_branch'])
    mix = jnp.sum(gates * proj, axis=2) @ p['w_o']
    h = _layer_norm(ALPHA * x + mix, p['ln1_g'], p['ln1_b'])
    f, conv_buf = _conv_ffn(h, past['conv'], p)
    y = _layer_norm(ALPHA * h + f, p['ln2_g'], p['ln2_b'])
    new = {'fox_kv': jnp.stack([ka, va], axis=2), 'fox_logf': logf,
           'dsa_kv': jnp.stack([kb, vb], axis=2), 'dsa_kidx': ki,
           'rwkv': s_rwkv, 'shift': shift, 'hgrn': s_hgrn, 'conv': conv_buf}
    return y, new


def setup_inputs(seed: int = 0) -> dict:
    key = jax.random.key(seed)
    ks = iter(jax.random.split(key, 48))

    def nrm(shape, s=1.0):
        return s * jax.random.normal(next(ks), shape, jnp.float32)

    n_pages = PAST_LEN // PAGE_SIZE
    n_used = DEC_BATCH * n_pages
    n_pool = n_used + max(1, n_used // 4)
    page_table = jax.random.permutation(next(ks), n_pool)[:n_used].reshape(DEC_BATCH, n_pages).astype(jnp.int32)
    D = D_MODEL
    return {
        'x_prompt': nrm((BATCH, SEQ, D)),
        'x_sample': nrm((DEC_BATCH, DEC_SEQ, D)),
        'cache_fox_kv': nrm((DEPTH, n_pool, PAGE_SIZE, 2, H_FOX, HEAD_DIM)),
        'cache_fox_logf': jax.nn.log_sigmoid(FOX_BIAS_MEAN + nrm((DEPTH, n_pool, PAGE_SIZE, H_FOX))),
        'cache_dsa_kv': nrm((DEPTH, n_pool, PAGE_SIZE, 2, H_DSA, HEAD_DIM)),
        'cache_dsa_kidx': nrm((DEPTH, n_pool, PAGE_SIZE, D_IDX)),
        'state_rwkv': nrm((DEPTH, DEC_BATCH, H_RWKV, RWKV_N, RWKV_N), 0.3),
        'state_rwkv_shift': nrm((DEPTH, DEC_BATCH, P_RWKV)),
        'state_hgrn': nrm((DEPTH, DEC_BATCH, H_HGRN, HGRN_DK, HGRN_DV), 0.3),
        'state_ffn_conv': nrm((DEPTH, DEC_BATCH, CONV_W - 1, D_FF)),
        'page_table': page_table,
        'w_in': nrm((DEPTH, D, D_IN), D ** -0.5),
        'fox_bf': FOX_BIAS_MEAN + nrm((DEPTH, H_FOX), 0.5),
        'rwkv_mu': jax.random.uniform(next(ks), (DEPTH, P_RWKV), jnp.float32),
        'rwkv_w0': -1.0 + nrm((DEPTH, W_RWKV), 0.5),
        'rwkv_w2': nrm((DEPTH, LORA_W, W_RWKV), 0.1),
        'rwkv_a0': nrm((DEPTH, W_RWKV), 0.5),
        'rwkv_a2': nrm((DEPTH, LORA_A, W_RWKV), 0.1),
        'rwkv_g2': nrm((DEPTH, LORA_G, W_RWKV), LORA_G ** -0.5),
        'rwkv_kk': 0.85 + nrm((DEPTH, W_RWKV), 0.05),
        'rwkv_ka': 1.0 + nrm((DEPTH, W_RWKV), 0.05),
        'rwkv_rk': nrm((DEPTH, W_RWKV), 0.1),
        'rwkv_ln_w': 1.0 + nrm((DEPTH, W_RWKV), 0.05),
        'rwkv_ln_b': nrm((DEPTH, W_RWKV), 0.02),
        'hgrn_lb': nrm((DEPTH, H_HGRN * HGRN_DK), 0.5),
        'hgrn_norm_w': 1.0 + nrm((DEPTH, W_HGRN), 0.05),
        'w_branch': nrm((DEPTH, N_BRANCH, W_BRANCH, D), W_BRANCH ** -0.5),
        'w_o': nrm((DEPTH, D, D), BETA * D ** -0.5),
        'ln1_g': 1.0 + nrm((DEPTH, D), 0.05),
        'ln1_b': nrm((DEPTH, D), 0.02),
        'ln2_g': 1.0 + nrm((DEPTH, D), 0.05),
        'ln2_b': nrm((DEPTH, D), 0.02),
        'ffn_w_in': nrm((DEPTH, D, 2 * D_FF), D ** -0.5),
        'ffn_conv_w': nrm((DEPTH, CONV_W, D_FF), CONV_W ** -0.5),
        'ffn_conv_b': nrm((DEPTH, D_FF), 0.02),
        'ffn_w_out': nrm((DEPTH, D_FF, D), BETA * D_FF ** -0.5),
    }


def reference(x_prompt, x_sample, cache_fox_kv, cache_fox_logf, cache_dsa_kv, cache_dsa_kidx,
              state_rwkv, state_rwkv_shift, state_hgrn, state_ffn_conv, page_table,
              w_in, fox_bf, rwkv_mu, rwkv_w0, rwkv_w2, rwkv_a0, rwkv_a2, rwkv_g2, rwkv_kk, rwkv_ka,
              rwkv_rk, rwkv_ln_w, rwkv_ln_b, hgrn_lb, hgrn_norm_w, w_branch, w_o,
              ln1_g, ln1_b, ln2_g, ln2_b, ffn_w_in, ffn_conv_w, ffn_conv_b, ffn_w_out):
    n_pages = page_table.shape[1]
    past_len = n_pages * cache_fox_kv.shape[2]
    bp = x_prompt.shape[0]
    dt = x_prompt.dtype
    lb_soft = jax.nn.softmax(hgrn_lb.astype(F32), axis=0)
    lb_all = jnp.maximum(jnp.cumsum(lb_soft, axis=0) - lb_soft[0], 0.0)

    def paged(cache, l):
        g = cache[l, page_table]
        return g.reshape(g.shape[0], past_len, *g.shape[3:])

    xp, xs = x_prompt, x_sample
    new_p, new_s = [], []
    for l in range(DEPTH):
        p = {'w_in': w_in[l], 'fox_bf': fox_bf[l], 'rwkv_mu': rwkv_mu[l], 'rwkv_w0': rwkv_w0[l],
             'rwkv_w2': rwkv_w2[l], 'rwkv_a0': rwkv_a0[l], 'rwkv_a2': rwkv_a2[l], 'rwkv_g2': rwkv_g2[l],
             'rwkv_kk': rwkv_kk[l], 'rwkv_ka': rwkv_ka[l], 'rwkv_rk': rwkv_rk[l],
             'rwkv_ln_w': rwkv_ln_w[l], 'rwkv_ln_b': rwkv_ln_b[l], 'hgrn_lb': lb_all[l],
             'hgrn_norm_w': hgrn_norm_w[l], 'w_branch': w_branch[l], 'w_o': w_o[l],
             'ln1_g': ln1_g[l], 'ln1_b': ln1_b[l], 'ln2_g': ln2_g[l], 'ln2_b': ln2_b[l],
             'ffn_w_in': ffn_w_in[l], 'ffn_conv_w': ffn_conv_w[l], 'ffn_conv_b': ffn_conv_b[l],
             'ffn_w_out': ffn_w_out[l]}
        past_p = {'fox_k': jnp.zeros((bp, 0, H_FOX, HEAD_DIM), dt),
                  'fox_v': jnp.zeros((bp, 0, H_FOX, HEAD_DIM), dt),
                  'fox_logf': jnp.zeros((bp, 0, H_FOX), F32),
                  'dsa_k': jnp.zeros((bp, 0, H_DSA, HEAD_DIM), dt),
                  'dsa_v': jnp.zeros((bp, 0, H_DSA, HEAD_DIM), dt),
                  'dsa_kidx': jnp.zeros((bp, 0, D_IDX), dt),
                  'rwkv': jnp.zeros((bp, H_RWKV, RWKV_N, RWKV_N), F32),
                  'shift': jnp.zeros((bp, P_RWKV), dt),
                  'hgrn': jnp.zeros((bp, H_HGRN, HGRN_DK, HGRN_DV), F32),
                  'conv': jnp.zeros((bp, CONV_W - 1, D_FF), dt)}
        fkv = paged(cache_fox_kv, l)
        dkv = paged(cache_dsa_kv, l)
        past_s = {'fox_k': fkv[:, :, 0], 'fox_v': fkv[:, :, 1], 'fox_logf': paged(cache_fox_logf, l),
                  'dsa_k': dkv[:, :, 0], 'dsa_v': dkv[:, :, 1], 'dsa_kidx': paged(cache_dsa_kidx, l),
                  'rwkv': state_rwkv[l], 'shift': state_rwkv_shift[l], 'hgrn': state_hgrn[l],
                  'conv': state_ffn_conv[l]}
        xp, st_p = _layer(xp, 0, past_p, p)
        xs, st_s = _layer(xs, past_len, past_s, p)
        new_p.append(st_p)
        new_s.append(st_s)

    def stk(lst, name):
        return jnp.stack([d[name] for d in lst])

    return (xp, xs,
            stk(new_p, 'fox_kv'), stk(new_p, 'fox_logf'), stk(new_p, 'dsa_kv'), stk(new_p, 'dsa_kidx'),
            stk(new_p, 'rwkv'), stk(new_p, 'shift'), stk(new_p, 'hgrn'), stk(new_p, 'conv'),
            stk(new_s, 'fox_kv'), stk(new_s, 'fox_logf'), stk(new_s, 'dsa_kv'), stk(new_s, 'dsa_kidx'),
            stk(new_s, 'rwkv'), stk(new_s, 'shift'), stk(new_s, 'hgrn'), stk(new_s, 'conv'))
```

```python
import functools
import math

import jax
import jax.numpy as jnp
import numpy as np
from jax import lax
from jax.experimental import pallas as pl
from jax.experimental.pallas import tpu as pltpu

F32 = jnp.float32
BF = jnp.bfloat16
I32 = jnp.int32

D_MODEL = 1024
N_HEAD = 4
HEAD_DIM = 64
W_BRANCH = N_HEAD * HEAD_DIM
D_IDX = 32
W_IDX = N_HEAD * D_IDX
LORA_W = 64
LORA_A = 64
LORA_G = 128
D_FF = 2816
N_GROUP = 4
W_GROUP = 1024
DSA_TOPK = 256
ROPE_THETA = 500000.0
ROPE_HALF_QK = 8
ROPE_HALF_IDX = 4
RWKV_GN_EPS = 64e-5
RMS_EPS = 1e-6
LB_TINY = 1e-30
LN_EPS = 1e-5
NEG = -1e30
ROW_TILE = 256
RWKV_CHUNK = 64
HGRN_BLOCK = 16
T_ALIGN = 8
INT_MIN = -2 ** 31


def _dot(a, b):
    return jnp.dot(a, b, preferred_element_type=F32)


def _dot_nt(a, b):
    return lax.dot_general(a, b, (((1,), (1,)), ((), ())), preferred_element_type=F32)


def _dot_tn(a, b):
    return lax.dot_general(a, b, (((0,), (0,)), ((), ())), preferred_element_type=F32)


def _split3(x):
    hi = x.astype(BF)
    r = x - hi.astype(F32)
    mid = r.astype(BF)
    lo = (r - mid.astype(F32)).astype(BF)
    return hi, mid, lo


def _dot_sel_l(sel_bf, x):
    hi, mid, lo = _split3(x)
    return _dot(sel_bf, hi) + _dot(sel_bf, mid) + _dot(sel_bf, lo)


def _dot_sel_r(x, sel_bf):
    hi, mid, lo = _split3(x)
    return _dot(hi, sel_bf) + _dot(mid, sel_bf) + _dot(lo, sel_bf)


def _dot_sel_tn(x, sel_bf):
    hi, mid, lo = _split3(x)
    return _dot_tn(hi, sel_bf) + _dot_tn(mid, sel_bf) + _dot_tn(lo, sel_bf)


def _iota(shape, dim):
    return lax.broadcasted_iota(I32, shape, dim)


def _tril(n, strict=False):
    r, c = _iota((n, n), 0), _iota((n, n), 1)
    return (r > c) if strict else (r >= c)


def _head_lane(width, head_width):
    return _iota((1, width), 1) // head_width


def _block_diag_mask(n, block):
    return (_iota((n, n), 0) // block) == (_iota((n, n), 1) // block)


def _stack_heads(x, head_width=HEAD_DIM):
    hl = _head_lane(x.shape[1], head_width)
    return jnp.concatenate([jnp.where(hl == h, x, jnp.zeros_like(x)) for h in range(N_HEAD)], axis=0)


def _unstack_heads(xs):
    c = xs.shape[0] // N_HEAD
    out = xs[0:c]
    for h in range(1, N_HEAD):
        out = out + xs[h * c:(h + 1) * c]
    return out


def _head_sum(x, ones_bd):
    return _dot_sel_r(x, ones_bd)


def _sigmoid(x):
    return 1.0 / (1.0 + jnp.exp(-x))


def _log_sigmoid(x):
    return jnp.minimum(x, 0.0) - jnp.log1p(jnp.exp(-jnp.abs(x)))


def _softplus(x):
    return jnp.maximum(x, 0.0) + jnp.log1p(jnp.exp(-jnp.abs(x)))


def _gelu(x):
    return 0.5 * x * (1.0 + lax.erf(x * (2.0 ** -0.5)))


def _layer_norm(x, g, b):
    mu = jnp.mean(x, axis=-1, keepdims=True)
    xc = x - mu
    var = jnp.mean(xc * xc, axis=-1, keepdims=True)
    return xc * lax.rsqrt(var + LN_EPS) * g + b


def _params(n_axes=1):
    return pltpu.CompilerParams(dimension_semantics=("arbitrary",) * n_axes)


def _const_spec(shape):
    nd = len(shape)
    return pl.BlockSpec(shape, lambda *_: (0,) * nd)


def _inproj_kernel(x_ref, w_ref, z_ref):
    z_ref[...] = _dot(x_ref[...].astype(BF), w_ref[...])


def _inproj(x, wz):
    n = x.shape[0]
    tm = min(ROW_TILE, n)
    nz = wz.shape[1]
    return pl.pallas_call(
        _inproj_kernel, grid=(n // tm,),
        in_specs=[pl.BlockSpec((tm, D_MODEL), lambda i: (i, 0)), _const_spec((D_MODEL, nz))],
        out_specs=pl.BlockSpec((tm, nz), lambda i: (i, 0)),
        out_shape=jax.ShapeDtypeStruct((n, nz), F32), compiler_params=_params(), name="inproj")(x, wz)


def _fox_prep_kernel(z_ref, bf_ref, q_ref, kv_ref, logf_ref, ccol_ref, crow_ref):
    t = z_ref.shape[0]
    blk = crow_ref.shape[2]
    kv_ref[...] = z_ref[:, 0:512].astype(BF)
    q_ref[...] = (z_ref[:, 512:768] * (HEAD_DIM ** -0.5)).astype(BF)
    logf = _log_sigmoid(z_ref[:, 768:896] + bf_ref[...])
    logf_ref[...] = logf
    tri = _tril(blk).astype(BF)
    carry = jnp.zeros((1, 128), F32)
    for c in range(t // blk):
        cum = _dot_sel_l(tri, logf[c * blk:(c + 1) * blk]) + carry
        carry = cum[blk - 1:blk]
        ccol_ref[c * blk:(c + 1) * blk, :] = cum
        crow_ref[c] = cum.T[0:8]


def _fox_prep(z, bf_row, b, t):
    blk = min(ROW_TILE, t)
    nb = t // blk
    return pl.pallas_call(
        _fox_prep_kernel, grid=(b,),
        in_specs=[pl.BlockSpec((t, W_GROUP), lambda i: (i, 0)), _const_spec((1, 128))],
        out_specs=[pl.BlockSpec((t, W_BRANCH), lambda i: (i, 0)), pl.BlockSpec((t, 512), lambda i: (i, 0)),
                   pl.BlockSpec((t, 128), lambda i: (i, 0)), pl.BlockSpec((t, 128), lambda i: (i, 0)),
                   pl.BlockSpec((None, nb, 8, blk), lambda i: (i, 0, 0, 0))],
        out_shape=[jax.ShapeDtypeStruct((b * t, W_BRANCH), BF), jax.ShapeDtypeStruct((b * t, 512), BF),
                   jax.ShapeDtypeStruct((b * t, 128), F32), jax.ShapeDtypeStruct((b * t, 128), F32),
                   jax.ShapeDtypeStruct((b, nb, 8, blk), F32)],
        compiler_params=_params(), name="fox_prep")(z, bf_row)


def _fox_attn_kernel(q_ref, kv_ref, ccol_ref, crow_ref, o_ref):
    tq = q_ref.shape[0]
    i = pl.program_id(1)
    q = q_ref[...]
    hl = _head_lane(W_BRANCH, HEAD_DIM)
    causal = _tril(tq)
    out = jnp.zeros((tq, W_BRANCH), F32)
    for h in range(N_HEAD):
        qh = jnp.where(hl == h, q, jnp.zeros_like(q))
        cq = ccol_ref[:, h:h + 1]

        def step(c, carry, diag, qh=qh, cq=cq, h=h):
            m, l, acc = carry
            kv = kv_ref[pl.ds(pl.multiple_of(c * tq, tq), tq), :]
            s = _dot_nt(qh, kv[:, 0:W_BRANCH]) + cq - crow_ref[c, h:h + 1, :]
            if diag:
                s = jnp.where(causal, s, NEG)
            m_new = jnp.maximum(m, jnp.max(s, axis=1, keepdims=True))
            a = jnp.exp(m - m_new)
            p = jnp.exp(s - m_new)
            l = a * l + jnp.sum(p, axis=1, keepdims=True)
            acc = a * acc + _dot(p.astype(BF), kv[:, W_BRANCH:])
            return m_new, l, acc

        init = (jnp.full((tq, 1), NEG, F32), jnp.zeros((tq, 1), F32), jnp.zeros((tq, W_BRANCH), F32))
        carry = lax.fori_loop(0, i, functools.partial(step, diag=False), init)
        m, l, acc = step(i, carry, True)
        out = jnp.where(hl == h, acc / l, out)
    o_ref[...] = out


def _fox_attn(q, kv, ccol, crow, b, t):
    tq = crow.shape[3]
    nq = t // tq
    return pl.pallas_call(
        _fox_attn_kernel, grid=(b, nq),
        in_specs=[pl.BlockSpec((tq, W_BRANCH), lambda bi, i: (bi * nq + i, 0)),
                  pl.BlockSpec((t, 512), lambda bi, i: (bi, 0)),
                  pl.BlockSpec((tq, 128), lambda bi, i: (bi * nq + i, 0)),
                  pl.BlockSpec((None, nq, 8, tq), lambda bi, i: (bi, 0, 0, 0))],
        out_specs=pl.BlockSpec((tq, W_BRANCH), lambda bi, i: (bi * nq + i, 0)),
        out_shape=jax.ShapeDtypeStruct((b * t, W_BRANCH), F32), compiler_params=_params(2), name="fox_attn")(q, kv, ccol, crow)


def _page_specs(n_pages, block, layer):
    nd = len(block)

    def mk(p):
        return pl.BlockSpec((None, None) + block, lambda b, pt: (layer, pt[b, p]) + (0,) * nd)

    return [mk(p) for p in range(n_pages)]


def _softmax_pv(s, v_past, v_new, past):
    m = jnp.max(s, axis=1, keepdims=True)
    p = jnp.exp(s - m)
    l = jnp.sum(p, axis=1, keepdims=True)
    pb = p.astype(BF)
    o = _dot(pb[:, :past], v_past) + _dot(pb[:, past:], v_new)
    o = _stack_mask(o / l)
    return _unstack_heads(o)


def _stack_mask(o):
    c = o.shape[0] // N_HEAD
    hl = _head_lane(o.shape[1], HEAD_DIM)
    row_h = _iota((o.shape[0], 1), 0) // c
    return jnp.where(row_h == hl, o, 0.0)


def _pad_rows(x, rows):
    return jnp.concatenate([x, jnp.zeros((rows - x.shape[0], x.shape[1]), x.dtype)], axis=0)


def _fox_decode_kernel(pt_ref, z_ref, bf_ref, *refs, n_pages):
    kv_refs = refs[:n_pages]
    lf_refs = refs[n_pages:2 * n_pages]
    o_ref, logf_ref = refs[2 * n_pages:]
    tt = z_ref.shape[0]
    page = kv_refs[0].shape[0]
    past = n_pages * page
    k_new = z_ref[:, 0:256]
    v_new = z_ref[:, 256:512]
    q = z_ref[:, 512:768] * (HEAD_DIM ** -0.5)
    logf = _log_sigmoid(z_ref[:, 768:896] + bf_ref[...])
    logf_ref[...] = logf

    lf = jnp.concatenate([r[...] for r in lf_refs], axis=0)
    n = lf.shape[0]
    tri_u = (_iota((page, page), 0) <= _iota((page, page), 1)).astype(BF)
    in_page = _dot_sel_r(lf, tri_u)
    tot = jnp.broadcast_to(in_page[:, page - 1:page], (n, page))
    r, c = _iota((n, n), 0), _iota((n, n), 1)
    later = jnp.where(((r % 8) == (c % 8)) & ((c // 8) >= (r // 8)), -1.0, 0.0).astype(BF)
    ck_rel = in_page + _dot_sel_l(later, tot)

    cn_col = _dot_sel_l(_tril(tt).astype(BF), logf)
    tri_pad = (_iota((tt, 128), 0) <= _iota((tt, 128), 1)).astype(BF)
    cn_row = _dot_sel_tn(logf, tri_pad)

    qbd = _stack_heads(q).astype(BF)
    k_past = jnp.concatenate([r[:, 0:256] for r in kv_refs], axis=0).astype(BF)
    v_past = jnp.concatenate([r[:, 256:512] for r in kv_refs], axis=0).astype(BF)
    s_past = _dot_nt(qbd, k_past)
    s_new = _dot_nt(qbd, _pad_rows(k_new, 128).astype(BF))

    cq = jnp.concatenate([cn_col[:, h:h + 1] for h in range(N_HEAD)], axis=0)
    bias_past = jnp.concatenate(
        [jnp.concatenate([jnp.broadcast_to(ck_rel[p * 8 + h:p * 8 + h + 1], (tt, page)) for h in range(N_HEAD)], axis=0)
         for p in range(n_pages)], axis=1)
    bias_new = jnp.concatenate([jnp.broadcast_to(cn_row[h:h + 1], (tt, 128)) for h in range(N_HEAD)], axis=0)
    tq = _iota((N_HEAD * tt, 128), 0) % tt
    ok_new = _iota((N_HEAD * tt, 128), 1) <= tq
    s = jnp.concatenate([s_past + cq - bias_past, jnp.where(ok_new, s_new + cq - bias_new, NEG)], axis=1)
    o_ref[...] = _softmax_pv(s, v_past, _pad_rows(v_new, 128).astype(BF), past)


def _fox_decode(z, bf_row, cache_kv, cache_lf, page_table, layer, b, tt):
    n_pages = page_table.shape[1]
    page = cache_kv.shape[2]
    kernel = functools.partial(_fox_decode_kernel, n_pages=n_pages)
    grid_spec = pltpu.PrefetchScalarGridSpec(
        num_scalar_prefetch=1, grid=(b,),
        in_specs=[pl.BlockSpec((tt, W_GROUP), lambda i, pt: (i, 0)), pl.BlockSpec((1, 128), lambda i, pt: (0, 0))]
        + _page_specs(n_pages, (page, 512), layer) + _page_specs(n_pages, (8, page), layer),
        out_specs=[pl.BlockSpec((tt, W_BRANCH), lambda i, pt: (i, 0)), pl.BlockSpec((tt, 128), lambda i, pt: (i, 0))])
    return pl.pallas_call(
        kernel, grid_spec=grid_spec,
        out_shape=[jax.ShapeDtypeStruct((b * tt, W_BRANCH), F32), jax.ShapeDtypeStruct((b * tt, 128), F32)],
        compiler_params=_params(), name="fox_decode")(page_table, z, bf_row, *([cache_kv] * n_pages), *([cache_lf] * n_pages))


def _rope_tables(pos, n_lanes, head_width, half):
    inv = jnp.power(ROPE_THETA, -jnp.arange(half, dtype=F32) / half)
    ang = pos.astype(F32)[:, None] * inv[None]
    d = np.arange(n_lanes) % head_width
    first = jnp.asarray(d < half)[None]
    second = jnp.asarray((d >= half) & (d < 2 * half))[None]
    cos = jnp.cos(ang)[:, d % half]
    sin = jnp.sin(ang)[:, d % half]
    c = jnp.where(first | second, cos, 1.0)
    s_up = jnp.where(first, -sin, 0.0)
    s_dn = jnp.where(second, sin, 0.0)
    return jnp.concatenate([c, s_up, s_dn], axis=1).astype(F32)


def _rope(x, tab, half):
    w = x.shape[1]
    return x * tab[:, 0:w] + pltpu.roll(x, w - half, 1) * tab[:, w:2 * w] + pltpu.roll(x, half, 1) * tab[:, 2 * w:3 * w]


def _dsa_prep_kernel(z_ref, tk_ref, ti_ref, kv_ref, kvb_ref, q_ref, qi_ref, misc_ref, ki4_ref):
    tk = tk_ref[...]
    ti = ti_ref[...]
    k = _rope(z_ref[:, 0:256], tk, ROPE_HALF_QK)
    v = z_ref[:, 256:512]
    q = _rope(z_ref[:, 512:768], tk, ROPE_HALF_QK) * (HEAD_DIM ** -0.5)
    qi = _rope(z_ref[:, 768:896], ti, ROPE_HALF_IDX)
    tail = z_ref[:, 896:1024]
    ki = _rope(tail, ti, ROPE_HALF_IDX)
    kv = jnp.concatenate([k, v], axis=1)
    kv_ref[...] = kv
    kvb_ref[...] = kv.astype(BF)
    q_ref[...] = q.astype(q_ref.dtype)
    qi_ref[...] = qi.astype(qi_ref.dtype)
    lane = _iota((1, 128), 1)
    kim = jnp.where(lane < D_IDX, ki, 0.0)
    ki4 = kim + pltpu.roll(kim, 32, 1) + pltpu.roll(kim, 64, 1) + pltpu.roll(kim, 96, 1)
    ki4_ref[...] = ki4.astype(BF)
    misc_ref[...] = jnp.where(lane < D_IDX, ki, tail * (N_HEAD ** -0.5))


def _dsa_prep(z, tab_k, tab_i, q_dtype):
    n = z.shape[0]
    tm = min(ROW_TILE, n)
    nt = tab_k.shape[0] // tm
    row = lambda w: pl.BlockSpec((tm, w), lambda i: (i, 0))
    return pl.pallas_call(
        _dsa_prep_kernel, grid=(n // tm,),
        in_specs=[pl.BlockSpec((tm, W_GROUP), lambda i: (i, 1)),
                  pl.BlockSpec((tm, 768), lambda i: (i % nt, 0)), pl.BlockSpec((tm, 384), lambda i: (i % nt, 0))],
        out_specs=[row(512), row(512), row(256), row(128), row(128), row(128)],
        out_shape=[jax.ShapeDtypeStruct((n, 512), F32), jax.ShapeDtypeStruct((n, 512), BF),
                   jax.ShapeDtypeStruct((n, 256), q_dtype), jax.ShapeDtypeStruct((n, 128), q_dtype),
                   jax.ShapeDtypeStruct((n, 128), F32), jax.ShapeDtypeStruct((n, 128), BF)],
        compiler_params=_params(), name="dsa_prep")(z, tab_k, tab_i)


def _sortable(x):
    b = pltpu.bitcast(x, I32)
    return b ^ ((b >> 31) & I32(0x7FFFFFFF))


def _topk_mask(score, idx, topk, n_idx_bits):
    key = _sortable(score)
    rows = score.shape[0]

    def count(pred):
        return jnp.sum(pred.astype(I32), axis=1, keepdims=True)

    def value_step(it, ans):
        cand = ans + jnp.left_shift(I32(1), I32(31) - it)
        return jnp.where(count(key >= cand) >= topk, cand, ans)

    thr = lax.fori_loop(0, 32, value_step, jnp.full((rows, 1), INT_MIN, I32))
    above = key > thr
    tie = key == thr
    need = topk - count(above)

    def index_step(it, lo):
        cand = lo + jnp.left_shift(I32(1), I32(n_idx_bits - 1) - it)
        return jnp.where(count(tie & (idx < cand)) < need, cand, lo)

    last = lax.fori_loop(0, n_idx_bits, index_step, jnp.zeros((rows, 1), I32))
    return above | (tie & (idx <= last))


def _dsa_attn_kernel(q_ref, qi_ref, misc_ref, kv_ref, ki4_ref, o_ref, *, topk):
    tq = q_ref.shape[0]
    t = kv_ref.shape[0]
    i = pl.program_id(1)
    qpos = i * tq + _iota((tq, 1), 0)
    kpos = _iota((1, t), 1)
    causal = kpos <= qpos
    qi = qi_ref[...]
    ki4 = ki4_ref[...]
    hl_i = _head_lane(W_IDX, D_IDX)
    score = jnp.zeros((tq, t), F32)
    for h in range(N_HEAD):
        sh = _dot_nt(jnp.where(hl_i == h, qi, jnp.zeros_like(qi)), ki4) * (D_IDX ** -0.5)
        score = score + jnp.maximum(sh, 0.0) * misc_ref[:, D_IDX + h:D_IDX + h + 1]
    score = jnp.where(causal, score, NEG)
    sel = _topk_mask(score, kpos, topk, max(1, (t - 1).bit_length())) & causal

    q = q_ref[...]
    k = kv_ref[:, 0:W_BRANCH]
    v = kv_ref[:, W_BRANCH:]
    hl = _head_lane(W_BRANCH, HEAD_DIM)
    out = jnp.zeros((tq, W_BRANCH), F32)
    for h in range(N_HEAD):
        s = jnp.where(sel, _dot_nt(jnp.where(hl == h, q, jnp.zeros_like(q)), k), NEG)
        p = jnp.exp(s - jnp.max(s, axis=1, keepdims=True))
        l = jnp.sum(p, axis=1, keepdims=True)
        out = jnp.where(hl == h, _dot(p.astype(BF), v) / l, out)
    o_ref[...] = out


def _dsa_attn(q, qi, misc, kvb, ki4, b, t, topk):
    tq = min(ROW_TILE, t)
    nq = t // tq
    qrow = lambda w: pl.BlockSpec((tq, w), lambda bi, i: (bi * nq + i, 0))
    seq = lambda w: pl.BlockSpec((t, w), lambda bi, i: (bi, 0))
    return pl.pallas_call(
        functools.partial(_dsa_attn_kernel, topk=topk), grid=(b, nq),
        in_specs=[qrow(256), qrow(128), qrow(128), seq(512), seq(128)],
        out_specs=qrow(256), out_shape=jax.ShapeDtypeStruct((b * t, W_BRANCH), F32),
        compiler_params=_params(2), name="dsa_attn")(q, qi, misc, kvb, ki4)


def _dsa_decode_kernel(pt_ref, q_ref, qi_ref, misc_ref, kvn_ref, *refs, n_pages, topk, t_valid):
    kv_refs = refs[:n_pages]
    ki_refs = refs[n_pages:2 * n_pages]
    o_ref = refs[2 * n_pages]
    tt = q_ref.shape[0]
    page = kv_refs[0].shape[0]
    past = n_pages * page
    rows = N_HEAD * tt

    qi = qi_ref[...]
    qi_h = jnp.concatenate([qi[:, h * D_IDX:(h + 1) * D_IDX] for h in range(N_HEAD)], axis=0).astype(BF)
    w_col = jnp.concatenate([misc_ref[:, D_IDX + h:D_IDX + h + 1] for h in range(N_HEAD)], axis=0)
    ki_past = jnp.concatenate([r[...] for r in ki_refs], axis=0).astype(BF)
    ki_new = _pad_rows(misc_ref[:, 0:D_IDX], 128).astype(BF)

    def idx_score(keys):
        s = jnp.maximum(_dot_nt(qi_h, keys) * (D_IDX ** -0.5), 0.0) * w_col
        return _unstack_heads(s)

    n_keys = past + 128
    idx = _iota((1, n_keys), 1)
    new_t = idx - past
    visible = (new_t <= _iota((tt, 1), 0)) & (new_t < t_valid)
    score = jnp.where(visible, jnp.concatenate([idx_score(ki_past), idx_score(ki_new)], axis=1), NEG)
    sel = _topk_mask(score, idx, topk, n_keys.bit_length()) & visible
    keep = jnp.where(sel, 0.0, NEG)

    qbd = _stack_heads(q_ref[...]).astype(BF)
    k_past = jnp.concatenate([r[:, 0:256] for r in kv_refs], axis=0).astype(BF)
    v_past = jnp.concatenate([r[:, 256:512] for r in kv_refs], axis=0).astype(BF)
    k_new = _pad_rows(kvn_ref[:, 0:256], 128).astype(BF)
    v_new = _pad_rows(kvn_ref[:, 256:512], 128).astype(BF)
    s = jnp.concatenate([_dot_nt(qbd, k_past), _dot_nt(qbd, k_new)], axis=1)
    keep4 = jnp.concatenate([keep] * N_HEAD, axis=0)
    o_ref[...] = _softmax_pv(jnp.where(keep4 == 0.0, s, NEG), v_past, v_new, past)


def _dsa_decode(q, qi, misc, kv_new, cache_kv, cache_ki, page_table, layer, b, tt, topk, t_valid):
    n_pages = page_table.shape[1]
    page = cache_kv.shape[2]
    kernel = functools.partial(_dsa_decode_kernel, n_pages=n_pages, topk=topk, t_valid=t_valid)
    row = lambda w: pl.BlockSpec((tt, w), lambda i, pt: (i, 0))
    grid_spec = pltpu.PrefetchScalarGridSpec(
        num_scalar_prefetch=1, grid=(b,),
        in_specs=[row(256), row(128), row(128), row(512)]
        + _page_specs(n_pages, (page, 512), layer) + _page_specs(n_pages, (page, D_IDX), layer),
        out_specs=row(256))
    return pl.pallas_call(
        kernel, grid_spec=grid_spec, out_shape=jax.ShapeDtypeStruct((b * tt, W_BRANCH), F32),
        compiler_params=_params(), name="dsa_decode")(page_table, q, qi, misc, kv_new, *([cache_kv] * n_pages), *([cache_ki] * n_pages))


def _rwkv_chunk(r, k, v, lw, kap, beta, s_big, n_double):
    c = r.shape[0]
    cc = N_HEAD * c
    g = _dot_sel_l(_tril(c).astype(BF), lw)
    g_end = g[c - 1:c]
    e_neg = jnp.exp(-g)
    e_end = jnp.exp(g_end - g)
    a_f = _stack_heads(kap * jnp.exp(g - lw))
    a_s = a_f.astype(BF)
    r_s = _stack_heads(r * jnp.exp(g))
    bb_s = _stack_heads(beta * e_neg).astype(BF)
    bk_s = _stack_heads(k * e_neg).astype(BF)
    v_s = _stack_heads(v).astype(BF)
    kh_s = _stack_heads(k * e_end).astype(BF)
    bh_s = _stack_heads(beta * e_end).astype(BF)
    strict = _tril(cc, strict=True)
    incl = _tril(cc)
    r_sb = r_s.astype(BF)
    l_b = jnp.where(strict, _dot_nt(a_s, bb_s), 0.0)
    l_k = jnp.where(strict, _dot_nt(a_s, bk_s), 0.0).astype(BF)
    w_b = jnp.where(incl, _dot_nt(r_sb, bb_s), 0.0).astype(BF)
    w_k = jnp.where(incl, _dot_nt(r_sb, bk_s), 0.0).astype(BF)
    y = -l_b
    n = y
    for _ in range(n_double):
        yb = y.astype(BF)
        y = _dot(yb, yb)
        n = n + y + _dot(n.astype(BF), y.astype(BF))
    nb = n.astype(BF)
    a_t = a_f + _dot(nb, a_s)
    lkv = _dot(l_k, v_s)
    u0 = lkv + _dot(nb, lkv.astype(BF))
    a_tb = a_t.astype(BF)
    u0b = u0.astype(BF)
    r_hat = r_s - _dot(w_b, a_tb)
    y0 = _dot(w_k, v_s) - _dot(w_b, u0b)
    h_mat = _dot_tn(a_tb, bh_s)
    s_add = _dot_tn(v_s, kh_s) - _dot_tn(u0b, bh_s)
    sb = s_big.astype(BF)
    ys = _dot_nt(r_hat.astype(BF), sb) + y0
    s_new = s_big * jnp.exp(g_end) - _dot(sb, h_mat.astype(BF)) + s_add
    return _unstack_heads(ys), s_new


def _to_block_diag(x):
    return jnp.where(_block_diag_mask(W_BRANCH, HEAD_DIM), jnp.concatenate([x] * N_HEAD, axis=1), 0.0)


def _from_block_diag(x):
    y = x + pltpu.roll(x, 64, 1) + pltpu.roll(x, 128, 1) + pltpu.roll(x, 192, 1)
    return y[:, 0:HEAD_DIM]


def _rwkv_kernel(*refs, seq_len, t_valid, chunk, n_double, has_state):
    if has_state:
        (z_ref, prev_ref, s_in_ref, mu_ref, vec_ref, wa_ref, g2_ref, o_ref, s_out_ref,
         r_s, k_s, v_s, lw_s, kap_s, beta_s, y_s) = refs
    else:
        (z_ref, mu_ref, vec_ref, wa_ref, g2_ref, o_ref, s_out_ref,
         r_s, k_s, v_s, lw_s, kap_s, beta_s, y_s, state_s, last_s) = refs
    rows = z_ref.shape[0]
    i = pl.program_id(0)
    t_row = (i * rows + _iota((rows, 1), 0)) % seq_len
    pr = z_ref[...]
    shifted = pltpu.roll(pr, 1, 0)
    if has_state:
        prev = jnp.where(t_row == 0, prev_ref[...], shifted)
    else:
        first = (i * rows) % seq_len == 0

        @pl.when(first)
        def _():
            last_s[...] = jnp.zeros_like(last_s)
            state_s[...] = jnp.zeros_like(state_s)

        prev = jnp.where(_iota((rows, 1), 0) == 0, last_s[7:8, :], shifted)
        last_s[...] = pr[rows - 8:rows]
    xs = pr + (prev - pr) * mu_ref[...]
    r, k, v = xs[:, 0:256], xs[:, 256:512], xs[:, 512:768]
    lora = xs[:, 768:896]
    lora = jnp.where(_iota((1, 128), 1) < LORA_W, jnp.tanh(lora), lora)
    wa = _dot(lora.astype(BF), wa_ref[...])
    w0, a0, kk_p, ka_p = vec_ref[0:1], vec_ref[1:2], vec_ref[2:3], vec_ref[3:4]
    rk_p, ln_w, ln_b = vec_ref[4:5], vec_ref[5:6], vec_ref[6:7]
    w_log = -_softplus(-(w0 + wa[:, 0:256])) - 0.5
    lw = -jnp.exp(w_log)
    a = _sigmoid(a0 + wa[:, 256:512])
    gate = _dot(_sigmoid(xs[:, 896:1024]).astype(BF), g2_ref[...])
    ones_bd = _block_diag_mask(W_BRANCH, HEAD_DIM).astype(BF)
    kk = k * kk_p
    kap = kk / jnp.maximum(jnp.sqrt(_head_sum(kk * kk, ones_bd)), 1e-12)
    k2 = k * (1.0 + (a - 1.0) * ka_p)
    bonus = _head_sum(r * k2 * rk_p, ones_bd) * v
    live = t_row < t_valid
    r_s[...] = r
    k_s[...] = jnp.where(live, k2, 0.0)
    v_s[...] = jnp.where(live, v, 0.0)
    lw_s[...] = jnp.where(live, lw, 0.0)
    kap_s[...] = jnp.where(live, kap, 0.0)
    beta_s[...] = jnp.where(live, a * kap, 0.0)

    def body(c, carry):
        sl = pl.ds(pl.multiple_of(c * chunk, chunk), chunk)
        if has_state:
            st = pl.ds(pl.multiple_of(c * W_BRANCH, W_BRANCH), W_BRANCH)
            s_big = _to_block_diag(s_in_ref[st, :])
        else:
            s_big = state_s[...]
        y, s_new = _rwkv_chunk(r_s[sl, :], k_s[sl, :], v_s[sl, :], lw_s[sl, :], kap_s[sl, :], beta_s[sl, :], s_big, n_double)
        y_s[sl, :] = y
        if has_state:
            s_out_ref[st, :] = _from_block_diag(s_new)
        else:
            state_s[...] = s_new
        return carry

    lax.fori_loop(0, rows // chunk, body, 0)
    if not has_state:
        s_out_ref[...] = _from_block_diag(state_s[...])
    y = y_s[...]
    mu = _head_sum(y, ones_bd) * (1.0 / HEAD_DIM)
    yc = y - mu
    var = _head_sum(yc * yc, ones_bd) * (1.0 / HEAD_DIM)
    yn = yc * lax.rsqrt(var + RWKV_GN_EPS) * ln_w + ln_b
    o_ref[...] = (yn + bonus) * gate


def _rwkv(z, prev_rows, state_in, mu, vec, wa, g2, b, seq_len, t_valid):
    n = z.shape[0]
    has_state = state_in is not None
    rows = min(ROW_TILE, n)
    chunk = min(RWKV_CHUNK, seq_len)
    n_double = max(0, int(math.log2(chunk)) - 1)
    steps = n // rows
    seq_per_tile = max(1, rows // seq_len)
    tiles_per_seq = max(1, seq_len // rows)
    kernel = functools.partial(_rwkv_kernel, seq_len=seq_len, t_valid=t_valid, chunk=chunk, n_double=n_double,
                               has_state=has_state)
    consts = [_const_spec((1, W_GROUP)), _const_spec((8, W_BRANCH)), _const_spec((128, 512)), _const_spec((128, W_BRANCH))]
    zspec = pl.BlockSpec((rows, W_GROUP), lambda i: (i, 2))
    scratch = [pltpu.VMEM((rows, W_BRANCH), F32)] * 7
    if has_state:
        srows = seq_per_tile * W_BRANCH
        in_specs = [zspec, pl.BlockSpec((rows, W_GROUP), lambda i: (i, 0)), pl.BlockSpec((srows, HEAD_DIM), lambda i: (i, 0))] + consts
        s_spec = pl.BlockSpec((srows, HEAD_DIM), lambda i: (i, 0))
        args = (z, prev_rows, state_in, mu, vec, wa, g2)
    else:
        in_specs = [zspec] + consts
        s_spec = pl.BlockSpec((W_BRANCH, HEAD_DIM), lambda i: (i // tiles_per_seq, 0))
        scratch = scratch + [pltpu.VMEM((W_BRANCH, W_BRANCH), F32), pltpu.VMEM((8, W_GROUP), F32)]
        args = (z, mu, vec, wa, g2)
    return pl.pallas_call(
        kernel, grid=(steps,), in_specs=in_specs,
        out_specs=[pl.BlockSpec((rows, W_BRANCH), lambda i: (i, 0)), s_spec],
        out_shape=[jax.ShapeDtypeStruct((n, W_BRANCH), F32), jax.ShapeDtypeStruct((b * W_BRANCH, HEAD_DIM), F32)],
        scratch_shapes=scratch, compiler_params=_params(), name="rwkv")(*args)


def _hgrn_lb_kernel(x_ref, o_ref):
    x = x_ref[...]
    depth = x.shape[0]
    e = jnp.exp(x - jnp.max(x, axis=0, keepdims=True))
    soft = e / jnp.sum(e, axis=0, keepdims=True)
    cum = jnp.zeros((1, x.shape[1]), F32)
    for l in range(depth):
        cum = cum + soft[l:l + 1]
        lb = jnp.maximum(cum - soft[0:1], 0.0)
        o_ref[l, 0:1, :] = lb
        o_ref[l, 1:2, :] = jnp.log(jnp.maximum(lb, LB_TINY))
        o_ref[l, 2:3, :] = jnp.log1p(-lb)
        o_ref[l, 3:8, :] = jnp.zeros((5, x.shape[1]), F32)


def _hgrn_lb(hgrn_lb):
    depth, w = hgrn_lb.shape
    return pl.pallas_call(_hgrn_lb_kernel, out_shape=jax.ShapeDtypeStruct((depth, 8, w), F32), name="hgrn_lb")(hgrn_lb)


def _hgrn_block(q, k, v, g, st, ones_bd, bd_mask):
    c = q.shape[0]
    b = _dot_sel_l(_tril(c).astype(BF), g)
    b_end = b[c - 1:c]
    o = _dot_nt((q * jnp.exp(b)).astype(BF), st.astype(BF))
    t_idx = _iota((c, 1), 0)
    xs = []
    for s in range(c):
        e = jnp.exp(jnp.where(t_idx >= s, b - b[s:s + 1], NEG))
        xs.append((q * e * k[s:s + 1]).astype(BF))
    col = _dot(jnp.concatenate(xs, axis=0), ones_bd)
    for s in range(c):
        o = o + col[s * c:(s + 1) * c] * v[s:s + 1]
    kh = (k * jnp.exp(b_end - b)).astype(BF)
    st_new = st * jnp.exp(b_end) + jnp.where(bd_mask, _dot_tn(v.astype(BF), kh), 0.0)
    return o, st_new


def _hgrn_kernel(*refs, seq_len, t_valid, block, has_state):
    if has_state:
        z_ref, s_in_ref, lb_ref, nw_ref, o_ref, s_out_ref, q_s, k_s, v_s, g_s, y_s = refs
    else:
        z_ref, lb_ref, nw_ref, o_ref, s_out_ref, q_s, k_s, v_s, g_s, y_s, state_s = refs
    rows = z_ref.shape[0]
    i = pl.program_id(0)
    t_row = (i * rows + _iota((rows, 1), 0)) % seq_len
    live = t_row < t_valid
    hq, hf, hi, hg = z_ref[:, 0:256], z_ref[:, 256:512], z_ref[:, 512:768], z_ref[:, 768:1024]
    lb, lb_log, l1m = lb_ref[0:1], lb_ref[1:2], lb_ref[2:3]
    ls = _log_sigmoid(hf)
    x2 = l1m + ls
    lae = jnp.maximum(lb_log, x2) + jnp.log1p(jnp.exp(-jnp.abs(lb_log - x2)))
    logf = jnp.where(lb > 0.0, lae, ls)
    q_s[...] = hq * _sigmoid(hq)
    k_s[...] = jnp.where(live, (1.0 - lb) * _sigmoid(-hf), 0.0)
    v_s[...] = hi
    g_s[...] = jnp.where(live, logf, 0.0)
    ones_bd = _block_diag_mask(W_BRANCH, HEAD_DIM).astype(BF)
    bd_mask = _block_diag_mask(W_BRANCH, HEAD_DIM)

    if not has_state:
        @pl.when((i * rows) % seq_len == 0)
        def _():
            state_s[...] = jnp.zeros_like(state_s)

    def body(c, carry):
        sl = pl.ds(pl.multiple_of(c * block, block), block)
        if has_state:
            sr = pl.ds(pl.multiple_of(c * W_BRANCH, W_BRANCH), W_BRANCH)
            st = _to_block_diag(s_in_ref[sr, :]).T
        else:
            st = state_s[...]
        o, st_new = _hgrn_block(q_s[sl, :], k_s[sl, :], v_s[sl, :], g_s[sl, :], st, ones_bd, bd_mask)
        y_s[sl, :] = o
        if has_state:
            s_out_ref[sr, :] = _from_block_diag(st_new.T)
        else:
            state_s[...] = st_new
        return carry

    lax.fori_loop(0, rows // block, body, 0)
    if not has_state:
        s_out_ref[...] = _from_block_diag(state_s[...].T)
    o = y_s[...]
    ms = _head_sum(o * o, ones_bd) * (1.0 / HEAD_DIM)
    o_ref[...] = o * lax.rsqrt(ms + RMS_EPS) * nw_ref[...] * (hg * _sigmoid(hg))


def _hgrn(z, state_in, lb_rows, norm_w, b, seq_len, t_valid):
    n = z.shape[0]
    has_state = state_in is not None
    rows = min(ROW_TILE, n)
    block = min(HGRN_BLOCK, seq_len)
    seq_per_tile = max(1, rows // seq_len)
    tiles_per_seq = max(1, seq_len // rows)
    kernel = functools.partial(_hgrn_kernel, seq_len=seq_len, t_valid=t_valid, block=block, has_state=has_state)
    zspec = pl.BlockSpec((rows, W_GROUP), lambda i: (i, 3))
    consts = [_const_spec((8, W_BRANCH)), _const_spec((1, W_BRANCH))]
    scratch = [pltpu.VMEM((rows, W_BRANCH), F32)] * 5
    if has_state:
        srows = seq_per_tile * W_BRANCH
        in_specs = [zspec, pl.BlockSpec((srows, HEAD_DIM), lambda i: (i, 0))] + consts
        s_spec = pl.BlockSpec((srows, HEAD_DIM), lambda i: (i, 0))
        args = (z, state_in, lb_rows, norm_w)
    else:
        in_specs = [zspec] + consts
        s_spec = pl.BlockSpec((W_BRANCH, HEAD_DIM), lambda i: (i // tiles_per_seq, 0))
        scratch = scratch + [pltpu.VMEM((W_BRANCH, W_BRANCH), F32)]
        args = (z, lb_rows, norm_w)
    return pl.pallas_call(
        kernel, grid=(n // rows,), in_specs=in_specs,
        out_specs=[pl.BlockSpec((rows, W_BRANCH), lambda i: (i, 0)), s_spec],
        out_shape=[jax.ShapeDtypeStruct((n, W_BRANCH), F32), jax.ShapeDtypeStruct((b * W_BRANCH, HEAD_DIM), F32)],
        scratch_shapes=scratch, compiler_params=_params(), name="hgrn")(*args)


def _merge_kernel(x_ref, oa_ref, ob_ref, oc_ref, od_ref, wg_ref, wb_ref, wo_ref, ln_ref, h_ref, *, alpha):
    x = x_ref[...]
    xb = x.astype(BF)
    m = jnp.zeros(x.shape, F32)
    for n, o_ref in enumerate((oa_ref, ob_ref, oc_ref, od_ref)):
        gate = _sigmoid(_dot(xb, wg_ref[:, n * D_MODEL:(n + 1) * D_MODEL]))
        m = m + gate * _dot(o_ref[...].astype(BF), wb_ref[n])
    mix = _dot(m.astype(BF), wo_ref[...])
    h_ref[...] = _layer_norm(alpha * x + mix, ln_ref[0:1], ln_ref[1:2])


def _merge(x, oa, ob, oc, od, wg, wb, wo, ln, alpha):
    n = x.shape[0]
    tm = min(ROW_TILE, n)
    row = lambda w: pl.BlockSpec((tm, w), lambda i: (i, 0))
    return pl.pallas_call(
        functools.partial(_merge_kernel, alpha=alpha), grid=(n // tm,),
        in_specs=[row(D_MODEL), row(256), row(256), row(256), row(256), _const_spec(wg.shape), _const_spec(wb.shape),
                  _const_spec(wo.shape), _const_spec((8, D_MODEL))],
        out_specs=row(D_MODEL), out_shape=jax.ShapeDtypeStruct((n, D_MODEL), F32),
        compiler_params=_params(), name="merge")(x, oa, ob, oc, od, wg, wb, wo, ln)


def _ffn_kernel(*refs, seq_len, has_state, n_split, alpha):
    if has_state:
        h_ref, p1_ref, p2_ref, wup_ref, wdn_ref, cv_ref, ln_ref, y_ref, a_ref = refs
    else:
        h_ref, wup_ref, wdn_ref, cv_ref, ln_ref, y_ref, a_ref, last_s = refs
    rows = h_ref.shape[0]
    i = pl.program_id(0)
    h = h_ref[...]
    hb = h.astype(BF)
    ridx = _iota((rows, 1), 0)
    t_row = (i * rows + ridx) % seq_len
    wf = D_FF // n_split
    f = jnp.zeros((rows, D_MODEL), F32)
    if not has_state:
        @pl.when((i * rows) % seq_len == 0)
        def _():
            last_s[...] = jnp.zeros_like(last_s)
    for j in range(n_split):
        lo, hi = j * wf, (j + 1) * wf
        a = _dot(hb, wup_ref[:, lo:hi])
        gt = _dot(hb, wup_ref[:, D_FF + lo:D_FF + hi])
        r1 = pltpu.roll(a, 1, 0)
        r2 = pltpu.roll(a, 2, 0)
        if has_state:
            prev1 = jnp.where(t_row == 0, p1_ref[:, lo:hi], r1)
            prev2 = jnp.where(t_row < 2, p2_ref[:, lo:hi], r2)
            a_ref[:, lo:hi] = a
        else:
            c6 = last_s[6:7, lo:hi]
            c7 = last_s[7:8, lo:hi]
            prev1 = jnp.where(ridx == 0, c7, r1)
            prev2 = jnp.where(ridx == 0, c6, jnp.where(ridx == 1, c7, r2))
            last_s[:, lo:hi] = a[rows - 8:rows]
            a_ref[:, lo:hi] = a[rows - 8:rows]
        conv = cv_ref[3:4, lo:hi] + prev2 * cv_ref[0:1, lo:hi] + prev1 * cv_ref[1:2, lo:hi] + a * cv_ref[2:3, lo:hi]
        hid = _gelu(conv) * gt
        f = f + _dot(hid.astype(BF), wdn_ref[lo:hi, :])
    y_ref[...] = _layer_norm(alpha * h + f, ln_ref[0:1], ln_ref[1:2])


def _ffn(h, p1, p2, wup, wdn, cv, ln, seq_len, alpha):
    n = h.shape[0]
    has_state = p1 is not None
    tm = min(ROW_TILE, n)
    row = lambda w: pl.BlockSpec((tm, w), lambda i: (i, 0))
    kernel = functools.partial(_ffn_kernel, seq_len=seq_len, has_state=has_state, n_split=2, alpha=alpha)
    consts = [_const_spec(wup.shape), _const_spec(wdn.shape), _const_spec((8, D_FF)), _const_spec((8, D_MODEL))]
    if has_state:
        in_specs = [row(D_MODEL), row(D_FF), row(D_FF)] + consts
        a_spec, a_rows, scratch = row(D_FF), n, []
        args = (h, p1, p2, wup, wdn, cv, ln)
    else:
        in_specs = [row(D_MODEL)] + consts
        a_spec, a_rows = pl.BlockSpec((8, D_FF), lambda i: (i, 0)), (n // tm) * 8
        scratch = [pltpu.VMEM((8, D_FF), F32)]
        args = (h, wup, wdn, cv, ln)
    return pl.pallas_call(
        kernel, grid=(n // tm,), in_specs=in_specs, out_specs=[row(D_MODEL), a_spec],
        out_shape=[jax.ShapeDtypeStruct((n, D_MODEL), F32), jax.ShapeDtypeStruct((a_rows, D_FF), F32)],
        scratch_shapes=scratch, compiler_params=_params(), name="ffn")(*args)


def _pad_lanes(x, width):
    return jnp.pad(x, [(0, 0)] * (x.ndim - 1) + [(0, width - x.shape[-1])])


def _regroup_w_in(w_in):
    o = 0
    cols = {}
    for name, w in (('fox_q', 256), ('fox_k', 256), ('fox_v', 256), ('fox_f', 4), ('dsa_q', 256), ('dsa_k', 256),
                    ('dsa_v', 256), ('idx_q', 128), ('idx_k', 32), ('idx_w', 4), ('rwkv', 1024), ('hgrn', 1024),
                    ('gate', 4096)):
        cols[name] = w_in[..., o:o + w]
        o += w
    g0 = _pad_lanes(jnp.concatenate([cols['fox_k'], cols['fox_v'], cols['fox_q'], cols['fox_f']], -1), W_GROUP)
    g1 = _pad_lanes(jnp.concatenate([cols['dsa_k'], cols['dsa_v'], cols['dsa_q'], cols['idx_q'], cols['idx_k'],
                                     cols['idx_w']], -1), W_GROUP)
    wz = jnp.concatenate([g0, g1, cols['rwkv'], cols['hgrn']], -1).astype(BF)
    return wz, cols['gate'].astype(BF)


def _rows8(*vecs, width):
    rows = [v.reshape(1, width) for v in vecs]
    rows.append(jnp.zeros((8 - len(rows), width), F32))
    return jnp.concatenate(rows, axis=0)


def _expand_first_rows(state, tt, offsets):
    b, _, w = state.shape
    out = jnp.zeros((b, tt, w), state.dtype)
    for s, ts in offsets:
        out = out.at[:, ts].set(state[:, s])
    return out.reshape(b * tt, w)


def _layer(x, cfg, lw):
    b, tt, tv, past = cfg['b'], cfg['tt'], cfg['tv'], cfg['past']
    decode = past > 0
    z = _inproj(x, lw['wz'])
    topk = max(1, min(DSA_TOPK, (past + tv) // 4))
    new = {}
    if decode:
        o_a, logf = _fox_decode(z, lw['fox_bf'], cfg['fox_kv'], cfg['fox_lf'], cfg['page_table'], cfg['layer'], b, tt)
    else:
        q_a, kv_a, logf, ccol, crow = _fox_prep(z, lw['fox_bf'], b, tt)
        o_a = _fox_attn(q_a, kv_a, ccol, crow, b, tt)
    new['fox_kv'] = z[:, 0:512]
    new['fox_logf'] = logf[:, 0:N_HEAD]
    kv_b, kvb_b, q_b, qi_b, misc_b, ki4_b = _dsa_prep(z, cfg['tab_k'], cfg['tab_i'], F32 if decode else BF)
    if decode:
        o_b = _dsa_decode(q_b, qi_b, misc_b, kv_b, cfg['dsa_kv'], cfg['dsa_ki'], cfg['page_table'], cfg['layer'], b, tt, topk, tv)
    else:
        o_b = _dsa_attn(q_b, qi_b, misc_b, kvb_b, ki4_b, b, tt, topk)
    new['dsa_kv'] = kv_b
    new['dsa_kidx'] = misc_b[:, 0:D_IDX]
    o_c, new['rwkv'] = _rwkv(z, cfg.get('shift_rows'), cfg.get('rwkv_state'), lw['rwkv_mu'], lw['rwkv_vec'], lw['rwkv_wa'],
                             lw['rwkv_g2'], b, tt, tv)
    new['shift'] = z[:, 2 * W_GROUP:3 * W_GROUP]
    o_d, new['hgrn'] = _hgrn(z, cfg.get('hgrn_state'), lw['hgrn_lb'], lw['hgrn_nw'], b, tt, tv)
    h = _merge(x, o_a, o_b, o_c, o_d, lw['wg'], lw['wb'], lw['wo'], lw['ln1'], cfg['alpha'])
    y, new['conv'] = _ffn(h, cfg.get('conv_p1'), cfg.get('conv_p2'), lw['wup'], lw['wdn'], lw['conv'], lw['ln2'], tt, cfg['alpha'])
    return y, new


def kernel(x_prompt, x_sample, cache_fox_kv, cache_fox_logf, cache_dsa_kv, cache_dsa_kidx, state_rwkv, state_rwkv_shift, state_hgrn, state_ffn_conv, page_table, w_in, fox_bf, rwkv_mu, rwkv_w0, rwkv_w2, rwkv_a0, rwkv_a2, rwkv_g2, rwkv_kk, rwkv_ka, rwkv_rk, rwkv_ln_w, rwkv_ln_b, hgrn_lb, hgrn_norm_w, w_branch, w_o, ln1_g, ln1_b, ln2_g, ln2_b, ffn_w_in, ffn_conv_w, ffn_conv_b, ffn_w_out):
    depth = w_in.shape[0]
    bp, tp, d = x_prompt.shape
    bs, ts, _ = x_sample.shape
    n_pool, page = cache_fox_kv.shape[1], cache_fox_kv.shape[2]
    n_pages = page_table.shape[1]
    past = n_pages * page
    tsp = -(-ts // T_ALIGN) * T_ALIGN
    assert d == D_MODEL and tp % min(ROW_TILE, tp) == 0 and (bs * tsp) % min(ROW_TILE, bs * tsp) == 0
    assert tsp <= 128 and ts >= 2

    wz_all, wg_all = _regroup_w_in(w_in)
    wb_all, wo_all = w_branch.astype(BF), w_o.astype(BF)
    wup_all, wdn_all = ffn_w_in.astype(BF), ffn_w_out.astype(BF)
    zero_l = jnp.zeros((depth, LORA_W, W_BRANCH), F32)
    wa_all = jnp.concatenate([jnp.concatenate([rwkv_w2, zero_l], 2), jnp.concatenate([zero_l, rwkv_a2], 2)], 1).astype(BF)
    g2_all = rwkv_g2.astype(BF)
    lb_all = _hgrn_lb(hgrn_lb)
    page_table = page_table.astype(I32)

    fox_kv_pages = cache_fox_kv.reshape(depth, n_pool, page, 512)
    dsa_kv_pages = cache_dsa_kv.reshape(depth, n_pool, page, 512)
    fox_lf_pages = _pad_rows_nd(jnp.swapaxes(cache_fox_logf, 2, 3), 8)

    pos_p = jnp.arange(tp)
    pos_s = past + (jnp.arange(bs * tsp) % tsp)[:min(ROW_TILE, bs * tsp)]
    alpha = (2 * depth) ** 0.25
    cfg_p = dict(b=bp, tt=tp, tv=tp, past=0, alpha=alpha,
                 tab_k=_rope_tables(pos_p, 256, HEAD_DIM, ROPE_HALF_QK), tab_i=_rope_tables(pos_p, 128, D_IDX, ROPE_HALF_IDX))
    cfg_s = dict(b=bs, tt=tsp, tv=ts, past=past, alpha=alpha, page_table=page_table, fox_kv=fox_kv_pages, fox_lf=fox_lf_pages,
                 dsa_kv=dsa_kv_pages, dsa_ki=cache_dsa_kidx,
                 tab_k=_rope_tables(pos_s, 256, HEAD_DIM, ROPE_HALF_QK), tab_i=_rope_tables(pos_s, 128, D_IDX, ROPE_HALF_IDX))

    xp = x_prompt.reshape(bp * tp, d)
    xs = jnp.pad(x_sample, ((0, 0), (0, tsp - ts), (0, 0))).reshape(bs * tsp, d)
    new_p, new_s = [], []
    for l in range(depth):
        lw = dict(wz=wz_all[l], wg=wg_all[l], wb=wb_all[l], wo=wo_all[l], wup=wup_all[l], wdn=wdn_all[l],
                  fox_bf=_pad_lanes(fox_bf[l][None], 128), rwkv_mu=rwkv_mu[l][None],
                  rwkv_vec=_rows8(rwkv_w0[l], rwkv_a0[l], rwkv_kk[l], rwkv_ka[l], rwkv_rk[l], rwkv_ln_w[l], rwkv_ln_b[l],
                                  width=W_BRANCH),
                  rwkv_wa=wa_all[l], rwkv_g2=g2_all[l], hgrn_lb=lb_all[l], hgrn_nw=hgrn_norm_w[l][None],
                  ln1=_rows8(ln1_g[l], ln1_b[l], width=D_MODEL), ln2=_rows8(ln2_g[l], ln2_b[l], width=D_MODEL),
                  conv=_rows8(ffn_conv_w[l, 0], ffn_conv_w[l, 1], ffn_conv_w[l, 2], ffn_conv_b[l], width=D_FF))
        xp, st_p = _layer(xp, cfg_p, lw)
        cfg_l = dict(cfg_s, layer=l,
                     shift_rows=_expand_first_rows(state_rwkv_shift[l][:, None], tsp, ((0, 0),)),
                     rwkv_state=state_rwkv[l].reshape(bs * W_BRANCH, HEAD_DIM),
                     hgrn_state=state_hgrn[l].reshape(bs * W_BRANCH, HEAD_DIM),
                     conv_p1=_expand_first_rows(state_ffn_conv[l], tsp, ((1, 0),)),
                     conv_p2=_expand_first_rows(state_ffn_conv[l], tsp, ((0, 0), (1, 1))))
        xs, st_s = _layer(xs, cfg_l, lw)
        new_p.append(st_p)
        new_s.append(st_s)

    def assemble(new, b, tt, tv, decode):
        def rows(name, shape):
            a = jnp.stack([n[name] for n in new]).reshape(depth, b, tt, -1)[:, :, :tv]
            return a.reshape((depth, b, tv) + shape)

        fox_kv = rows('fox_kv', (2, N_HEAD, HEAD_DIM))
        fox_logf = rows('fox_logf', (N_HEAD,))
        dsa_kv = rows('dsa_kv', (2, N_HEAD, HEAD_DIM))
        dsa_kidx = rows('dsa_kidx', (D_IDX,))
        rwkv = jnp.stack([n['rwkv'] for n in new]).reshape(depth, b, N_HEAD, HEAD_DIM, HEAD_DIM)
        hgrn = jnp.stack([n['hgrn'] for n in new]).reshape(depth, b, N_HEAD, HEAD_DIM, HEAD_DIM)
        shift = jnp.stack([n['shift'] for n in new]).reshape(depth, b, tt, W_GROUP)[:, :, tv - 1]
        conv = jnp.stack([n['conv'] for n in new])
        if decode:
            conv = conv.reshape(depth, b, tt, D_FF)[:, :, tv - 2:tv]
        else:
            conv = conv.reshape(depth, b, -1, 8, D_FF)[:, :, -1, 6:8]
        return fox_kv, fox_logf, dsa_kv, dsa_kidx, rwkv, shift, hgrn, conv

    y_p = xp.reshape(bp, tp, d)
    y_s = xs.reshape(bs, tsp, d)[:, :ts]
    return (y_p, y_s) + assemble(new_p, bp, tp, tp, False) + assemble(new_s, bs, tsp, ts, True)


def _pad_rows_nd(x, rows):
    pad = [(0, 0)] * x.ndim
    pad[-2] = (0, rows - x.shape[-2])
    return jnp.pad(x, pad)
```

```python
import functools
import math

import jax
import jax.numpy as jnp
import numpy as np
from jax import lax
from jax.experimental import pallas as pl
from jax.experimental.pallas import tpu as pltpu

F32 = jnp.float32
BF = jnp.bfloat16
I32 = jnp.int32
I16 = jnp.int16

D_MODEL = 1024
N_HEAD = 4
HEAD_DIM = 64
W_BRANCH = N_HEAD * HEAD_DIM
D_IDX = 32
W_IDX = N_HEAD * D_IDX
LORA_W = 64
LORA_A = 64
LORA_G = 128
D_FF = 2816
N_GROUP = 4
W_GROUP = 1024
DSA_TOPK = 256
ROPE_THETA = 500000.0
ROPE_HALF_QK = 8
ROPE_HALF_IDX = 4
RWKV_GN_EPS = 64e-5
RMS_EPS = 1e-6
LB_TINY = 1e-30
LN_EPS = 1e-5
NEG = -1e30
ROW_TILE = 256
KEY_CHUNK = 512
SELECT_SEQS = 8
RWKV_CHUNK = 64
HGRN_BLOCK = 16
T_ALIGN = 8
INT_MIN = -2 ** 31


def _dot(a, b):
    return jnp.dot(a, b, preferred_element_type=F32)


def _dot_nt(a, b):
    return lax.dot_general(a, b, (((1,), (1,)), ((), ())), preferred_element_type=F32)


def _dot_tn(a, b):
    return lax.dot_general(a, b, (((0,), (0,)), ((), ())), preferred_element_type=F32)


def _split3(x):
    hi = x.astype(BF)
    r = x - hi.astype(F32)
    mid = r.astype(BF)
    lo = (r - mid.astype(F32)).astype(BF)
    return hi, mid, lo


def _dot_sel_l(sel_bf, x):
    hi, mid, lo = _split3(x)
    return _dot(sel_bf, hi) + _dot(sel_bf, mid) + _dot(sel_bf, lo)


def _dot_sel_r(x, sel_bf):
    hi, mid, lo = _split3(x)
    return _dot(hi, sel_bf) + _dot(mid, sel_bf) + _dot(lo, sel_bf)


def _dot_sel_tn(x, sel_bf):
    hi, mid, lo = _split3(x)
    return _dot_tn(hi, sel_bf) + _dot_tn(mid, sel_bf) + _dot_tn(lo, sel_bf)


def _iota(shape, dim):
    return lax.broadcasted_iota(I32, shape, dim)


def _tril(n, strict=False):
    r, c = _iota((n, n), 0), _iota((n, n), 1)
    return (r > c) if strict else (r >= c)


def _head_lane(width, head_width):
    return _iota((1, width), 1) // head_width


def _block_diag_mask(n, block):
    return (_iota((n, n), 0) // block) == (_iota((n, n), 1) // block)


def _stack_heads(x, head_width=HEAD_DIM):
    hl = _head_lane(x.shape[1], head_width)
    return jnp.concatenate([jnp.where(hl == h, x, jnp.zeros_like(x)) for h in range(N_HEAD)], axis=0)


def _unstack_heads(xs):
    c = xs.shape[0] // N_HEAD
    out = xs[0:c]
    for h in range(1, N_HEAD):
        out = out + xs[h * c:(h + 1) * c]
    return out


def _head_sum(x, ones_bd):
    return _dot_sel_r(x, ones_bd)


def _sigmoid(x):
    return 1.0 / (1.0 + jnp.exp(-x))


def _log_sigmoid(x):
    return jnp.minimum(x, 0.0) - jnp.log1p(jnp.exp(-jnp.abs(x)))


def _softplus(x):
    return jnp.maximum(x, 0.0) + jnp.log1p(jnp.exp(-jnp.abs(x)))


def _gelu(x):
    return 0.5 * x * (1.0 + lax.erf(x * (2.0 ** -0.5)))


def _layer_norm(x, g, b):
    mu = jnp.mean(x, axis=-1, keepdims=True)
    xc = x - mu
    var = jnp.mean(xc * xc, axis=-1, keepdims=True)
    return xc * lax.rsqrt(var + LN_EPS) * g + b


def _params(n_axes=1):
    return pltpu.CompilerParams(dimension_semantics=("arbitrary",) * n_axes)


def _const_spec(shape):
    nd = len(shape)
    return pl.BlockSpec(shape, lambda *_: (0,) * nd)


def _inproj_kernel(x_ref, w_ref, z_ref):
    z_ref[...] = _dot(x_ref[...].astype(BF), w_ref[...])


def _inproj(x, wz):
    n = x.shape[0]
    tm = min(ROW_TILE, n)
    nz = wz.shape[1]
    return pl.pallas_call(
        _inproj_kernel, grid=(n // tm,),
        in_specs=[pl.BlockSpec((tm, D_MODEL), lambda i: (i, 0)), _const_spec((D_MODEL, nz))],
        out_specs=pl.BlockSpec((tm, nz), lambda i: (i, 0)),
        out_shape=jax.ShapeDtypeStruct((n, nz), F32), compiler_params=_params(), name="inproj")(x, wz)


def _fox_prep_kernel(z_ref, bf_ref, q_ref, kv_ref, logf_ref, ccol_ref, crow_ref):
    t = z_ref.shape[0]
    blk = crow_ref.shape[2]
    kv_ref[...] = z_ref[:, 0:512].astype(BF)
    q_ref[...] = (z_ref[:, 512:768] * (HEAD_DIM ** -0.5)).astype(BF)
    logf = _log_sigmoid(z_ref[:, 768:896] + bf_ref[...])
    logf_ref[...] = logf
    tri = _tril(blk).astype(BF)
    carry = jnp.zeros((1, 128), F32)
    for c in range(t // blk):
        cum = _dot_sel_l(tri, logf[c * blk:(c + 1) * blk]) + carry
        carry = cum[blk - 1:blk]
        ccol_ref[c * blk:(c + 1) * blk, :] = cum
        crow_ref[c] = cum.T[0:8]


def _fox_prep(z, bf_row, b, t):
    blk = min(KEY_CHUNK, t)
    nb = t // blk
    return pl.pallas_call(
        _fox_prep_kernel, grid=(b,),
        in_specs=[pl.BlockSpec((t, W_GROUP), lambda i: (i, 0)), _const_spec((1, 128))],
        out_specs=[pl.BlockSpec((t, W_BRANCH), lambda i: (i, 0)), pl.BlockSpec((t, 512), lambda i: (i, 0)),
                   pl.BlockSpec((t, 128), lambda i: (i, 0)), pl.BlockSpec((t, 128), lambda i: (i, 0)),
                   pl.BlockSpec((None, nb, 8, blk), lambda i: (i, 0, 0, 0))],
        out_shape=[jax.ShapeDtypeStruct((b * t, W_BRANCH), BF), jax.ShapeDtypeStruct((b * t, 512), BF),
                   jax.ShapeDtypeStruct((b * t, 128), F32), jax.ShapeDtypeStruct((b * t, 128), F32),
                   jax.ShapeDtypeStruct((b, nb, 8, blk), F32)],
        compiler_params=_params(), name="fox_prep")(z, bf_row)


def _flash_init(m_s, l_s, acc_s):
    m_s[...] = jnp.full(m_s.shape, NEG, F32)
    l_s[...] = jnp.zeros(l_s.shape, F32)
    acc_s[...] = jnp.zeros(acc_s.shape, F32)


def _flash_update(h, s, v, m_s, l_s, acc_s):
    m_old = m_s[h]
    m_new = jnp.maximum(m_old, jnp.max(s, axis=1, keepdims=True))
    a = jnp.exp(m_old - m_new)
    p = jnp.exp(s - m_new)
    l_s[h] = a * l_s[h] + jnp.sum(p, axis=1, keepdims=True)
    acc_s[h] = a * acc_s[h] + _dot(p.astype(BF), v)
    m_s[h] = m_new


def _flash_result(l_s, acc_s):
    hl = _head_lane(W_BRANCH, HEAD_DIM)
    out = jnp.zeros(acc_s.shape[1:], F32)
    for h in range(N_HEAD):
        out = jnp.where(hl == h, acc_s[h] / l_s[h], out)
    return out


def _flash_scratch(tq):
    return [pltpu.VMEM((N_HEAD, tq, 1), F32), pltpu.VMEM((N_HEAD, tq, 1), F32), pltpu.VMEM((N_HEAD, tq, W_BRANCH), F32)]


def _causal_chunks(i, tq, ck):
    return (i * tq) // ck + 1


def _fox_attn_kernel(q_ref, kv_ref, ccol_ref, crow_ref, o_ref, m_s, l_s, acc_s):
    tq = q_ref.shape[0]
    ck = crow_ref.shape[2]
    i = pl.program_id(1)
    q = q_ref[...]
    hl = _head_lane(W_BRANCH, HEAD_DIM)
    qpos = i * tq + _iota((tq, 1), 0)
    qh = [jnp.where(hl == h, q, jnp.zeros_like(q)) for h in range(N_HEAD)]
    cq = [ccol_ref[:, h:h + 1] for h in range(N_HEAD)]
    _flash_init(m_s, l_s, acc_s)

    def chunk(c, carry):
        kv = kv_ref[pl.ds(pl.multiple_of(c * ck, ck), ck), :]
        visible = (c * ck + _iota((1, ck), 1)) <= qpos
        for h in range(N_HEAD):
            s = _dot_nt(qh[h], kv[:, 0:W_BRANCH]) + cq[h] - crow_ref[c, h:h + 1, :]
            _flash_update(h, jnp.where(visible, s, NEG), kv[:, W_BRANCH:], m_s, l_s, acc_s)
        return carry

    lax.fori_loop(0, _causal_chunks(i, tq, ck), chunk, 0)
    o_ref[...] = _flash_result(l_s, acc_s)


def _fox_attn(q, kv, ccol, crow, b, t):
    tq = min(ROW_TILE, t)
    nq = t // tq
    nc, ck = crow.shape[1], crow.shape[3]
    return pl.pallas_call(
        _fox_attn_kernel, grid=(b, nq),
        in_specs=[pl.BlockSpec((tq, W_BRANCH), lambda bi, i: (bi * nq + i, 0)),
                  pl.BlockSpec((t, 512), lambda bi, i: (bi, 0)),
                  pl.BlockSpec((tq, 128), lambda bi, i: (bi * nq + i, 0)),
                  pl.BlockSpec((None, nc, 8, ck), lambda bi, i: (bi, 0, 0, 0))],
        out_specs=pl.BlockSpec((tq, W_BRANCH), lambda bi, i: (bi * nq + i, 0)),
        out_shape=jax.ShapeDtypeStruct((b * t, W_BRANCH), F32), scratch_shapes=_flash_scratch(tq),
        compiler_params=_params(2), name="fox_attn")(q, kv, ccol, crow)


def _page_specs(n_pages, block, layer):
    nd = len(block)

    def mk(p):
        return pl.BlockSpec((None, None) + block, lambda b, pt: (layer, pt[b, p]) + (0,) * nd)

    return [mk(p) for p in range(n_pages)]


def _softmax_pv(s, vt_past, v_new, past):
    m = jnp.max(s, axis=1, keepdims=True)
    p = jnp.exp(s - m)
    l = jnp.sum(p, axis=1, keepdims=True)
    pb = p.astype(BF)
    o = _dot_nt(pb[:, :past], vt_past) + _dot(pb[:, past:], v_new)
    o = _stack_mask(o / l)
    return _unstack_heads(o)


def _cat_pages(refs, lo, hi):
    return jnp.concatenate([r[lo:hi, :] for r in refs], axis=1)


def _stack_mask(o):
    c = o.shape[0] // N_HEAD
    hl = _head_lane(o.shape[1], HEAD_DIM)
    row_h = _iota((o.shape[0], 1), 0) // c
    return jnp.where(row_h == hl, o, 0.0)


def _pad_rows(x, rows):
    return jnp.concatenate([x, jnp.zeros((rows - x.shape[0], x.shape[1]), x.dtype)], axis=0)


def _fox_decode_kernel(pt_ref, z_ref, bf_ref, *refs, n_pages):
    kv_refs = refs[:n_pages]
    lf_refs = refs[n_pages:2 * n_pages]
    o_ref, logf_ref = refs[2 * n_pages:]
    tt = z_ref.shape[0]
    page = kv_refs[0].shape[1]
    past = n_pages * page
    k_new = z_ref[:, 0:256]
    v_new = z_ref[:, 256:512]
    q = z_ref[:, 512:768] * (HEAD_DIM ** -0.5)
    logf = _log_sigmoid(z_ref[:, 768:896] + bf_ref[...])
    logf_ref[...] = logf

    lf = jnp.concatenate([r[...] for r in lf_refs], axis=0)
    n = lf.shape[0]
    tri_u = (_iota((page, page), 0) <= _iota((page, page), 1)).astype(BF)
    in_page = _dot_sel_r(lf, tri_u)
    tot = jnp.broadcast_to(in_page[:, page - 1:page], (n, page))
    r, c = _iota((n, n), 0), _iota((n, n), 1)
    later = jnp.where(((r % 8) == (c % 8)) & ((c // 8) >= (r // 8)), -1.0, 0.0).astype(BF)
    ck_rel = in_page + _dot_sel_l(later, tot)

    cn_col = _dot_sel_l(_tril(tt).astype(BF), logf)
    tri_pad = (_iota((tt, 128), 0) <= _iota((tt, 128), 1)).astype(BF)
    cn_row = _dot_sel_tn(logf, tri_pad)

    qbd = _stack_heads(q).astype(BF)
    s_past = _dot(qbd, _cat_pages(kv_refs, 0, W_BRANCH).astype(BF))
    vt_past = _cat_pages(kv_refs, W_BRANCH, 2 * W_BRANCH).astype(BF)
    s_new = _dot_nt(qbd, _pad_rows(k_new, 128).astype(BF))

    cq = jnp.concatenate([cn_col[:, h:h + 1] for h in range(N_HEAD)], axis=0)
    bias_past = jnp.concatenate(
        [jnp.concatenate([jnp.broadcast_to(ck_rel[p * 8 + h:p * 8 + h + 1], (tt, page)) for h in range(N_HEAD)], axis=0)
         for p in range(n_pages)], axis=1)
    bias_new = jnp.concatenate([jnp.broadcast_to(cn_row[h:h + 1], (tt, 128)) for h in range(N_HEAD)], axis=0)
    tq = _iota((N_HEAD * tt, 128), 0) % tt
    ok_new = _iota((N_HEAD * tt, 128), 1) <= tq
    s = jnp.concatenate([s_past + cq - bias_past, jnp.where(ok_new, s_new + cq - bias_new, NEG)], axis=1)
    o_ref[...] = _softmax_pv(s, vt_past, _pad_rows(v_new, 128).astype(BF), past)


def _fox_decode(z, bf_row, cache_kv, cache_lf, page_table, layer, b, tt):
    n_pages = page_table.shape[1]
    page = cache_kv.shape[3]
    kernel = functools.partial(_fox_decode_kernel, n_pages=n_pages)
    grid_spec = pltpu.PrefetchScalarGridSpec(
        num_scalar_prefetch=1, grid=(b,),
        in_specs=[pl.BlockSpec((tt, W_GROUP), lambda i, pt: (i, 0)), pl.BlockSpec((1, 128), lambda i, pt: (0, 0))]
        + _page_specs(n_pages, (2 * W_BRANCH, page), layer) + _page_specs(n_pages, (8, page), layer),
        out_specs=[pl.BlockSpec((tt, W_BRANCH), lambda i, pt: (i, 0)), pl.BlockSpec((tt, 128), lambda i, pt: (i, 0))])
    return pl.pallas_call(
        kernel, grid_spec=grid_spec,
        out_shape=[jax.ShapeDtypeStruct((b * tt, W_BRANCH), F32), jax.ShapeDtypeStruct((b * tt, 128), F32)],
        compiler_params=_params(), name="fox_decode")(page_table, z, bf_row, *([cache_kv] * n_pages), *([cache_lf] * n_pages))


def _rope_tables(pos, n_lanes, head_width, half):
    inv = jnp.power(ROPE_THETA, -jnp.arange(half, dtype=F32) / half)
    ang = pos.astype(F32)[:, None] * inv[None]
    d = np.arange(n_lanes) % head_width
    first = jnp.asarray(d < half)[None]
    second = jnp.asarray((d >= half) & (d < 2 * half))[None]
    cos = jnp.cos(ang)[:, d % half]
    sin = jnp.sin(ang)[:, d % half]
    c = jnp.where(first | second, cos, 1.0)
    s_up = jnp.where(first, -sin, 0.0)
    s_dn = jnp.where(second, sin, 0.0)
    return jnp.concatenate([c, s_up, s_dn], axis=1).astype(F32)


def _rope(x, tab, half):
    w = x.shape[1]
    return x * tab[:, 0:w] + pltpu.roll(x, w - half, 1) * tab[:, w:2 * w] + pltpu.roll(x, half, 1) * tab[:, 2 * w:3 * w]


def _dsa_prep_kernel(z_ref, tk_ref, ti_ref, kv_ref, kvb_ref, q_ref, qi_ref, misc_ref, ki4_ref):
    tk = tk_ref[...]
    ti = ti_ref[...]
    k = _rope(z_ref[:, 0:256], tk, ROPE_HALF_QK)
    v = z_ref[:, 256:512]
    q = _rope(z_ref[:, 512:768], tk, ROPE_HALF_QK) * (HEAD_DIM ** -0.5)
    qi = _rope(z_ref[:, 768:896], ti, ROPE_HALF_IDX)
    tail = z_ref[:, 896:1024]
    ki = _rope(tail, ti, ROPE_HALF_IDX)
    kv = jnp.concatenate([k, v], axis=1)
    kv_ref[...] = kv
    kvb_ref[...] = kv.astype(BF)
    q_ref[...] = q.astype(q_ref.dtype)
    qi_ref[...] = qi.astype(qi_ref.dtype)
    lane = _iota((1, 128), 1)
    kim = jnp.where(lane < D_IDX, ki, 0.0)
    ki4 = kim + pltpu.roll(kim, 32, 1) + pltpu.roll(kim, 64, 1) + pltpu.roll(kim, 96, 1)
    ki4_ref[...] = ki4.astype(BF)
    misc_ref[...] = jnp.where(lane < D_IDX, ki, tail * (N_HEAD ** -0.5))


def _dsa_prep(z, tab_k, tab_i, q_dtype):
    n = z.shape[0]
    tm = min(ROW_TILE, n)
    nt = tab_k.shape[0] // tm
    row = lambda w: pl.BlockSpec((tm, w), lambda i: (i, 0))
    return pl.pallas_call(
        _dsa_prep_kernel, grid=(n // tm,),
        in_specs=[pl.BlockSpec((tm, W_GROUP), lambda i: (i, 1)),
                  pl.BlockSpec((tm, 768), lambda i: (i % nt, 0)), pl.BlockSpec((tm, 384), lambda i: (i % nt, 0))],
        out_specs=[row(512), row(512), row(256), row(128), row(128), row(128)],
        out_shape=[jax.ShapeDtypeStruct((n, 512), F32), jax.ShapeDtypeStruct((n, 512), BF),
                   jax.ShapeDtypeStruct((n, 256), q_dtype), jax.ShapeDtypeStruct((n, 128), q_dtype),
                   jax.ShapeDtypeStruct((n, 128), F32), jax.ShapeDtypeStruct((n, 128), BF)],
        compiler_params=_params(), name="dsa_prep")(z, tab_k, tab_i)


def _sortable(x):
    b = pltpu.bitcast(x, I32)
    return b ^ ((b >> 31) & I32(0x7FFFFFFF))


def _topk_mask(score, idx, topk, n_idx_bits):
    key = _sortable(score)
    rows = score.shape[0]

    def count(pred):
        return jnp.sum(pred.astype(I32), axis=1, keepdims=True)

    def value_step(it, ans):
        cand = ans + jnp.left_shift(I32(1), I32(31) - it)
        return jnp.where(count(key >= cand) >= topk, cand, ans)

    thr = lax.fori_loop(0, 32, value_step, jnp.full((rows, 1), INT_MIN, I32))
    above = key > thr
    tie = key == thr
    need = topk - count(above)

    def index_step(it, lo):
        cand = lo + jnp.left_shift(I32(1), I32(n_idx_bits - 1) - it)
        return jnp.where(count(tie & (idx < cand)) < need, cand, lo)

    last = lax.fori_loop(0, n_idx_bits, index_step, jnp.zeros((rows, 1), I32))
    return above | (tie & (idx <= last))


def _dsa_attn_kernel(q_ref, qi_ref, misc_ref, kv_ref, ki4_ref, o_ref, hi_s, lo_s, tie_s, keep_s, m_s, l_s, acc_s, *,
                     topk, n_idx_bits):
    tq = q_ref.shape[0]
    ck = hi_s.shape[2]
    i = pl.program_id(1)
    nc = _causal_chunks(i, tq, ck)
    qpos = i * tq + _iota((tq, 1), 0)
    qpos16 = qpos.astype(I16)
    qi = qi_ref[...]
    hl_i = _head_lane(W_IDX, D_IDX)
    qih = [jnp.where(hl_i == h, qi, jnp.zeros_like(qi)) for h in range(N_HEAD)]
    w_col = [misc_ref[:, D_IDX + h:D_IDX + h + 1] for h in range(N_HEAD)]
    lane = _iota((1, ck), 1)
    i16_min, i16_max = I16(-32768), I16(32767)

    def score_chunk(c, carry):
        ki4 = ki4_ref[pl.ds(pl.multiple_of(c * ck, ck), ck), :]
        sc = jnp.zeros((tq, ck), F32)
        for h in range(N_HEAD):
            sc = sc + jnp.maximum(_dot_nt(qih[h], ki4) * (D_IDX ** -0.5), 0.0) * w_col[h]
        sc = jnp.where((c * ck + lane) <= qpos, sc, NEG)
        key = _sortable(jnp.where(sc == 0.0, 0.0, sc))
        hi_s[c] = (key >> 16).astype(I16)
        lo_s[c] = ((key & I32(0xFFFF)) - I32(32768)).astype(I16)
        return carry

    lax.fori_loop(0, nc, score_chunk, 0)

    def count(pred):
        def body(c, acc):
            ones = jnp.where(pred(c), I16(1), I16(0))
            for j in range(ck // 128):
                acc = acc + ones[:, j * 128:(j + 1) * 128]
            return acc
        acc = lax.fori_loop(0, nc, body, jnp.zeros((tq, 128), I16))
        return jnp.sum(acc.astype(I32), axis=1, keepdims=True)

    def search16(ref, want):
        def step(it, ans):
            cand = ans + jnp.left_shift(I32(1), I32(15) - it)
            c16 = cand.astype(I16)
            return jnp.where(count(lambda c: ref[c] >= c16) >= want, cand, ans)
        return lax.fori_loop(0, 16, step, jnp.full((tq, 1), -32768, I32)).astype(I16)

    h16 = search16(hi_s, topk)
    n_above = count(lambda c: hi_s[c] > h16)

    def mask_lo(c, carry):
        lo_s[c] = jnp.where(hi_s[c] == h16, lo_s[c], i16_min)
        return carry

    lax.fori_loop(0, nc, mask_lo, 0)
    l16 = search16(lo_s, topk - n_above)
    need = topk - n_above - count(lambda c: lo_s[c] > l16)

    def tie_chunk(c, carry):
        kpos16 = (c * ck + lane).astype(I16)
        tie = (hi_s[c] == h16) & (lo_s[c] == l16) & (kpos16 <= qpos16)
        tie_s[c] = jnp.where(tie, kpos16, i16_max)
        return carry

    lax.fori_loop(0, nc, tie_chunk, 0)

    def index_step(it, lo):
        cand = lo + jnp.left_shift(I32(1), I32(n_idx_bits - 1) - it)
        c16 = cand.astype(I16)
        return jnp.where(count(lambda c: tie_s[c] < c16) < need, cand, lo)

    last16 = lax.fori_loop(0, n_idx_bits, index_step, jnp.zeros((tq, 1), I32)).astype(I16)

    def keep_chunk(c, carry):
        kpos16 = (c * ck + lane).astype(I16)
        sel = jnp.where(hi_s[c] > h16, I16(1), I16(0)) + jnp.where(lo_s[c] > l16, I16(1), I16(0)) \
            + jnp.where(tie_s[c] <= last16, I16(1), I16(0))
        sel = jnp.where(kpos16 <= qpos16, sel, I16(0))
        keep_s[c] = jnp.where(sel.astype(I32) > 0, 0.0, NEG)
        return carry

    lax.fori_loop(0, nc, keep_chunk, 0)

    q = q_ref[...]
    hl = _head_lane(W_BRANCH, HEAD_DIM)
    qh = [jnp.where(hl == h, q, jnp.zeros_like(q)) for h in range(N_HEAD)]
    _flash_init(m_s, l_s, acc_s)

    def attend_chunk(c, carry):
        kv = kv_ref[pl.ds(pl.multiple_of(c * ck, ck), ck), :]
        keep = keep_s[c]
        for h in range(N_HEAD):
            _flash_update(h, _dot_nt(qh[h], kv[:, 0:W_BRANCH]) + keep, kv[:, W_BRANCH:], m_s, l_s, acc_s)
        return carry

    lax.fori_loop(0, nc, attend_chunk, 0)
    o_ref[...] = _flash_result(l_s, acc_s)


def _dsa_attn(q, qi, misc, kvb, ki4, b, t, topk):
    tq = min(ROW_TILE, t)
    ck = min(KEY_CHUNK, t)
    nq, nc = t // tq, t // ck
    assert topk <= ck and t < 2 ** 15
    qrow = lambda w: pl.BlockSpec((tq, w), lambda bi, i: (bi * nq + i, 0))
    seq = lambda w: pl.BlockSpec((t, w), lambda bi, i: (bi, 0))
    kernel = functools.partial(_dsa_attn_kernel, topk=topk, n_idx_bits=max(1, (t - 1).bit_length()))
    scratch = [pltpu.VMEM((nc, tq, ck), I16)] * 3 + [pltpu.VMEM((nc, tq, ck), F32)] + _flash_scratch(tq)
    return pl.pallas_call(
        kernel, grid=(b, nq), in_specs=[qrow(256), qrow(128), qrow(128), seq(512), seq(128)],
        out_specs=qrow(256), out_shape=jax.ShapeDtypeStruct((b * t, W_BRANCH), F32), scratch_shapes=scratch,
        compiler_params=_params(2), name="dsa_attn")(q, qi, misc, kvb, ki4)


def _dsa_select_kernel(pt_ref, qi_ref, misc_ref, *refs, n_pages, n_seq, tt, topk, t_valid):
    ki_refs = refs[:n_seq * n_pages]
    keep_ref = refs[n_seq * n_pages]
    page = ki_refs[0].shape[1]
    past = n_pages * page
    scores = []
    for j in range(n_seq):
        rows = slice(j * tt, (j + 1) * tt)
        qi = qi_ref[rows, :]
        qi_h = jnp.concatenate([qi[:, h * D_IDX:(h + 1) * D_IDX] for h in range(N_HEAD)], axis=0).astype(BF)
        w_col = jnp.concatenate([misc_ref[rows, D_IDX + h:D_IDX + h + 1] for h in range(N_HEAD)], axis=0)
        kit_past = _cat_pages(ki_refs[j * n_pages:(j + 1) * n_pages], 0, D_IDX).astype(BF)
        ki_new = _pad_rows(misc_ref[rows, 0:D_IDX], 128).astype(BF)
        s = jnp.concatenate([_dot(qi_h, kit_past), _dot_nt(qi_h, ki_new)], axis=1)
        scores.append(_unstack_heads(jnp.maximum(s * (D_IDX ** -0.5), 0.0) * w_col))
    score = jnp.concatenate(scores, axis=0)
    n_keys = past + 128
    idx = _iota((1, n_keys), 1)
    new_t = idx - past
    t_q = _iota((n_seq * tt, 1), 0) % tt
    visible = (new_t <= t_q) & (new_t < t_valid)
    score = jnp.where(visible, jnp.where(score == 0.0, 0.0, score), NEG)
    sel = _topk_mask(score, idx, topk, n_keys.bit_length()) & visible
    keep_ref[...] = jnp.where(sel, 0.0, NEG)


def _dsa_select(qi, misc, cache_ki, page_table, layer, b, tt, topk, t_valid):
    n_pages = page_table.shape[1]
    page = cache_ki.shape[3]
    n_seq = min(SELECT_SEQS, b)
    n_keys = n_pages * page + 128
    kernel = functools.partial(_dsa_select_kernel, n_pages=n_pages, n_seq=n_seq, tt=tt, topk=topk, t_valid=t_valid)

    def page_spec(j, p):
        return pl.BlockSpec((None, None, D_IDX, page), lambda g, pt: (layer, pt[g * n_seq + j, p], 0, 0))

    rows = lambda w: pl.BlockSpec((n_seq * tt, w), lambda g, pt: (g, 0))
    grid_spec = pltpu.PrefetchScalarGridSpec(
        num_scalar_prefetch=1, grid=(b // n_seq,),
        in_specs=[rows(128), rows(128)] + [page_spec(j, p) for j in range(n_seq) for p in range(n_pages)],
        out_specs=rows(n_keys))
    return pl.pallas_call(
        kernel, grid_spec=grid_spec, out_shape=jax.ShapeDtypeStruct((b * tt, n_keys), F32),
        compiler_params=_params(), name="dsa_select")(page_table, qi, misc, *([cache_ki] * (n_seq * n_pages)))


def _dsa_decode_kernel(pt_ref, q_ref, keep_ref, kvn_ref, *refs, n_pages):
    kv_refs = refs[:n_pages]
    o_ref = refs[n_pages]
    past = n_pages * kv_refs[0].shape[1]
    qbd = _stack_heads(q_ref[...]).astype(BF)
    k_new = _pad_rows(kvn_ref[:, 0:256], 128).astype(BF)
    v_new = _pad_rows(kvn_ref[:, 256:512], 128).astype(BF)
    s = jnp.concatenate([_dot(qbd, _cat_pages(kv_refs, 0, W_BRANCH).astype(BF)), _dot_nt(qbd, k_new)], axis=1)
    keep4 = jnp.concatenate([keep_ref[...]] * N_HEAD, axis=0)
    vt_past = _cat_pages(kv_refs, W_BRANCH, 2 * W_BRANCH).astype(BF)
    o_ref[...] = _softmax_pv(jnp.where(keep4 == 0.0, s, NEG), vt_past, v_new, past)


def _dsa_decode(q, keep, kv_new, cache_kv, page_table, layer, b, tt):
    n_pages = page_table.shape[1]
    page = cache_kv.shape[3]
    kernel = functools.partial(_dsa_decode_kernel, n_pages=n_pages)
    row = lambda w: pl.BlockSpec((tt, w), lambda i, pt: (i, 0))
    grid_spec = pltpu.PrefetchScalarGridSpec(
        num_scalar_prefetch=1, grid=(b,),
        in_specs=[row(256), row(keep.shape[1]), row(512)] + _page_specs(n_pages, (2 * W_BRANCH, page), layer),
        out_specs=row(256))
    return pl.pallas_call(
        kernel, grid_spec=grid_spec, out_shape=jax.ShapeDtypeStruct((b * tt, W_BRANCH), F32),
        compiler_params=_params(), name="dsa_decode")(page_table, q, keep, kv_new, *([cache_kv] * n_pages))


def _rwkv_chunk(r, k, v, lw, kap, beta, s_big, n_double):
    c = r.shape[0]
    cc = N_HEAD * c
    g = _dot_sel_l(_tril(c).astype(BF), lw)
    g_end = g[c - 1:c]
    e_neg = jnp.exp(-g)
    e_end = jnp.exp(g_end - g)
    a_f = _stack_heads(kap * jnp.exp(g - lw))
    a_s = a_f.astype(BF)
    r_s = _stack_heads(r * jnp.exp(g))
    bb_s = _stack_heads(beta * e_neg).astype(BF)
    bk_s = _stack_heads(k * e_neg).astype(BF)
    v_s = _stack_heads(v).astype(BF)
    kh_s = _stack_heads(k * e_end).astype(BF)
    bh_s = _stack_heads(beta * e_end).astype(BF)
    strict = _tril(cc, strict=True)
    incl = _tril(cc)
    r_sb = r_s.astype(BF)
    l_b = jnp.where(strict, _dot_nt(a_s, bb_s), 0.0)
    l_k = jnp.where(strict, _dot_nt(a_s, bk_s), 0.0).astype(BF)
    w_b = jnp.where(incl, _dot_nt(r_sb, bb_s), 0.0).astype(BF)
    w_k = jnp.where(incl, _dot_nt(r_sb, bk_s), 0.0).astype(BF)
    y = -l_b
    n = y
    for _ in range(n_double):
        yb = y.astype(BF)
        y = _dot(yb, yb)
        n = n + y + _dot(n.astype(BF), y.astype(BF))
    nb = n.astype(BF)
    a_t = a_f + _dot(nb, a_s)
    lkv = _dot(l_k, v_s)
    u0 = lkv + _dot(nb, lkv.astype(BF))
    a_tb = a_t.astype(BF)
    u0b = u0.astype(BF)
    r_hat = r_s - _dot(w_b, a_tb)
    y0 = _dot(w_k, v_s) - _dot(w_b, u0b)
    h_mat = _dot_tn(a_tb, bh_s)
    s_add = _dot_tn(v_s, kh_s) - _dot_tn(u0b, bh_s)
    sb = s_big.astype(BF)
    ys = _dot_nt(r_hat.astype(BF), sb) + y0
    s_new = s_big * jnp.exp(g_end) - _dot(sb, h_mat.astype(BF)) + s_add
    return _unstack_heads(ys), s_new


def _to_block_diag(x):
    return jnp.where(_block_diag_mask(W_BRANCH, HEAD_DIM), jnp.concatenate([x] * N_HEAD, axis=1), 0.0)


def _from_block_diag(x):
    y = x + pltpu.roll(x, 64, 1) + pltpu.roll(x, 128, 1) + pltpu.roll(x, 192, 1)
    return y[:, 0:HEAD_DIM]


def _rwkv_kernel(*refs, seq_len, t_valid, chunk, n_double, has_state):
    if has_state:
        (z_ref, prev_ref, s_in_ref, mu_ref, vec_ref, wa_ref, g2_ref, o_ref, s_out_ref,
         r_s, k_s, v_s, lw_s, kap_s, beta_s, y_s) = refs
    else:
        (z_ref, mu_ref, vec_ref, wa_ref, g2_ref, o_ref, s_out_ref,
         r_s, k_s, v_s, lw_s, kap_s, beta_s, y_s, state_s, last_s) = refs
    rows = z_ref.shape[0]
    i = pl.program_id(0)
    t_row = (i * rows + _iota((rows, 1), 0)) % seq_len
    pr = z_ref[...]
    shifted = pltpu.roll(pr, 1, 0)
    if has_state:
        prev = jnp.where(t_row == 0, prev_ref[...], shifted)
    else:
        first = (i * rows) % seq_len == 0

        @pl.when(first)
        def _():
            last_s[...] = jnp.zeros_like(last_s)
            state_s[...] = jnp.zeros_like(state_s)

        prev = jnp.where(_iota((rows, 1), 0) == 0, last_s[7:8, :], shifted)
        last_s[...] = pr[rows - 8:rows]
    xs = pr + (prev - pr) * mu_ref[...]
    r, k, v = xs[:, 0:256], xs[:, 256:512], xs[:, 512:768]
    lora = xs[:, 768:896]
    lora = jnp.where(_iota((1, 128), 1) < LORA_W, jnp.tanh(lora), lora)
    wa = _dot(lora.astype(BF), wa_ref[...])
    w0, a0, kk_p, ka_p = vec_ref[0:1], vec_ref[1:2], vec_ref[2:3], vec_ref[3:4]
    rk_p, ln_w, ln_b = vec_ref[4:5], vec_ref[5:6], vec_ref[6:7]
    w_log = -_softplus(-(w0 + wa[:, 0:256])) - 0.5
    lw = -jnp.exp(w_log)
    a = _sigmoid(a0 + wa[:, 256:512])
    gate = _dot(_sigmoid(xs[:, 896:1024]).astype(BF), g2_ref[...])
    ones_bd = _block_diag_mask(W_BRANCH, HEAD_DIM).astype(BF)
    kk = k * kk_p
    kap = kk / jnp.maximum(jnp.sqrt(_head_sum(kk * kk, ones_bd)), 1e-12)
    k2 = k * (1.0 + (a - 1.0) * ka_p)
    bonus = _head_sum(r * k2 * rk_p, ones_bd) * v
    live = t_row < t_valid
    r_s[...] = r
    k_s[...] = jnp.where(live, k2, 0.0)
    v_s[...] = jnp.where(live, v, 0.0)
    lw_s[...] = jnp.where(live, lw, 0.0)
    kap_s[...] = jnp.where(live, kap, 0.0)
    beta_s[...] = jnp.where(live, a * kap, 0.0)

    def body(c, carry):
        sl = pl.ds(pl.multiple_of(c * chunk, chunk), chunk)
        if has_state:
            st = pl.ds(pl.multiple_of(c * W_BRANCH, W_BRANCH), W_BRANCH)
            s_big = _to_block_diag(s_in_ref[st, :])
        else:
            s_big = state_s[...]
        y, s_new = _rwkv_chunk(r_s[sl, :], k_s[sl, :], v_s[sl, :], lw_s[sl, :], kap_s[sl, :], beta_s[sl, :], s_big, n_double)
        y_s[sl, :] = y
        if has_state:
            s_out_ref[st, :] = _from_block_diag(s_new)
        else:
            state_s[...] = s_new
        return carry

    lax.fori_loop(0, rows // chunk, body, 0)
    if not has_state:
        s_out_ref[...] = _from_block_diag(state_s[...])
    y = y_s[...]
    mu = _head_sum(y, ones_bd) * (1.0 / HEAD_DIM)
    yc = y - mu
    var = _head_sum(yc * yc, ones_bd) * (1.0 / HEAD_DIM)
    yn = yc * lax.rsqrt(var + RWKV_GN_EPS) * ln_w + ln_b
    o_ref[...] = (yn + bonus) * gate


def _rwkv(z, prev_rows, state_in, mu, vec, wa, g2, b, seq_len, t_valid):
    n = z.shape[0]
    has_state = state_in is not None
    rows = min(ROW_TILE, n)
    chunk = min(RWKV_CHUNK, seq_len)
    n_double = max(0, int(math.log2(chunk)) - 1)
    steps = n // rows
    seq_per_tile = max(1, rows // seq_len)
    tiles_per_seq = max(1, seq_len // rows)
    kernel = functools.partial(_rwkv_kernel, seq_len=seq_len, t_valid=t_valid, chunk=chunk, n_double=n_double,
                               has_state=has_state)
    consts = [_const_spec((1, W_GROUP)), _const_spec((8, W_BRANCH)), _const_spec((128, 512)), _const_spec((128, W_BRANCH))]
    zspec = pl.BlockSpec((rows, W_GROUP), lambda i: (i, 2))
    scratch = [pltpu.VMEM((rows, W_BRANCH), F32)] * 7
    if has_state:
        srows = seq_per_tile * W_BRANCH
        in_specs = [zspec, pl.BlockSpec((rows, W_GROUP), lambda i: (i, 0)), pl.BlockSpec((srows, HEAD_DIM), lambda i: (i, 0))] + consts
        s_spec = pl.BlockSpec((srows, HEAD_DIM), lambda i: (i, 0))
        args = (z, prev_rows, state_in, mu, vec, wa, g2)
    else:
        in_specs = [zspec] + consts
        s_spec = pl.BlockSpec((W_BRANCH, HEAD_DIM), lambda i: (i // tiles_per_seq, 0))
        scratch = scratch + [pltpu.VMEM((W_BRANCH, W_BRANCH), F32), pltpu.VMEM((8, W_GROUP), F32)]
        args = (z, mu, vec, wa, g2)
    return pl.pallas_call(
        kernel, grid=(steps,), in_specs=in_specs,
        out_specs=[pl.BlockSpec((rows, W_BRANCH), lambda i: (i, 0)), s_spec],
        out_shape=[jax.ShapeDtypeStruct((n, W_BRANCH), F32), jax.ShapeDtypeStruct((b * W_BRANCH, HEAD_DIM), F32)],
        scratch_shapes=scratch, compiler_params=_params(), name="rwkv")(*args)


def _hgrn_lb_kernel(x_ref, o_ref):
    x = x_ref[...]
    depth = x.shape[0]
    e = jnp.exp(x - jnp.max(x, axis=0, keepdims=True))
    soft = e / jnp.sum(e, axis=0, keepdims=True)
    cum = jnp.zeros((1, x.shape[1]), F32)
    for l in range(depth):
        cum = cum + soft[l:l + 1]
        lb = jnp.maximum(cum - soft[0:1], 0.0)
        o_ref[l, 0:1, :] = lb
        o_ref[l, 1:2, :] = jnp.log(jnp.maximum(lb, LB_TINY))
        o_ref[l, 2:3, :] = jnp.log1p(-lb)
        o_ref[l, 3:8, :] = jnp.zeros((5, x.shape[1]), F32)


def _hgrn_lb(hgrn_lb):
    depth, w = hgrn_lb.shape
    return pl.pallas_call(_hgrn_lb_kernel, out_shape=jax.ShapeDtypeStruct((depth, 8, w), F32), name="hgrn_lb")(hgrn_lb)


def _hgrn_block(q, k, v, g, st, ones_bd, bd_mask):
    c = q.shape[0]
    b = _dot_sel_l(_tril(c).astype(BF), g)
    b_end = b[c - 1:c]
    o = _dot_nt((q * jnp.exp(b)).astype(BF), st.astype(BF))
    t_idx = _iota((c, 1), 0)
    xs = []
    for s in range(c):
        e = jnp.exp(jnp.where(t_idx >= s, b - b[s:s + 1], NEG))
        xs.append((q * e * k[s:s + 1]).astype(BF))
    col = _dot(jnp.concatenate(xs, axis=0), ones_bd)
    for s in range(c):
        o = o + col[s * c:(s + 1) * c] * v[s:s + 1]
    kh = (k * jnp.exp(b_end - b)).astype(BF)
    st_new = st * jnp.exp(b_end) + jnp.where(bd_mask, _dot_tn(v.astype(BF), kh), 0.0)
    return o, st_new


def _hgrn_kernel(*refs, seq_len, t_valid, block, has_state):
    if has_state:
        z_ref, s_in_ref, lb_ref, nw_ref, o_ref, s_out_ref, q_s, k_s, v_s, g_s, y_s = refs
    else:
        z_ref, lb_ref, nw_ref, o_ref, s_out_ref, q_s, k_s, v_s, g_s, y_s, state_s = refs
    rows = z_ref.shape[0]
    i = pl.program_id(0)
    t_row = (i * rows + _iota((rows, 1), 0)) % seq_len
    live = t_row < t_valid
    hq, hf, hi, hg = z_ref[:, 0:256], z_ref[:, 256:512], z_ref[:, 512:768], z_ref[:, 768:1024]
    lb, lb_log, l1m = lb_ref[0:1], lb_ref[1:2], lb_ref[2:3]
    ls = _log_sigmoid(hf)
    x2 = l1m + ls
    lae = jnp.maximum(lb_log, x2) + jnp.log1p(jnp.exp(-jnp.abs(lb_log - x2)))
    logf = jnp.where(lb > 0.0, lae, ls)
    q_s[...] = hq * _sigmoid(hq)
    k_s[...] = jnp.where(live, (1.0 - lb) * _sigmoid(-hf), 0.0)
    v_s[...] = hi
    g_s[...] = jnp.where(live, logf, 0.0)
    ones_bd = _block_diag_mask(W_BRANCH, HEAD_DIM).astype(BF)
    bd_mask = _block_diag_mask(W_BRANCH, HEAD_DIM)

    if not has_state:
        @pl.when((i * rows) % seq_len == 0)
        def _():
            state_s[...] = jnp.zeros_like(state_s)

    def body(c, carry):
        sl = pl.ds(pl.multiple_of(c * block, block), block)
        if has_state:
            sr = pl.ds(pl.multiple_of(c * W_BRANCH, W_BRANCH), W_BRANCH)
            st = _to_block_diag(s_in_ref[sr, :]).T
        else:
            st = state_s[...]
        o, st_new = _hgrn_block(q_s[sl, :], k_s[sl, :], v_s[sl, :], g_s[sl, :], st, ones_bd, bd_mask)
        y_s[sl, :] = o
        if has_state:
            s_out_ref[sr, :] = _from_block_diag(st_new.T)
        else:
            state_s[...] = st_new
        return carry

    lax.fori_loop(0, rows // block, body, 0)
    if not has_state:
        s_out_ref[...] = _from_block_diag(state_s[...].T)
    o = y_s[...]
    ms = _head_sum(o * o, ones_bd) * (1.0 / HEAD_DIM)
    o_ref[...] = o * lax.rsqrt(ms + RMS_EPS) * nw_ref[...] * (hg * _sigmoid(hg))


def _hgrn(z, state_in, lb_rows, norm_w, b, seq_len, t_valid):
    n = z.shape[0]
    has_state = state_in is not None
    rows = min(ROW_TILE, n)
    block = min(HGRN_BLOCK, seq_len)
    seq_per_tile = max(1, rows // seq_len)
    tiles_per_seq = max(1, seq_len // rows)
    kernel = functools.partial(_hgrn_kernel, seq_len=seq_len, t_valid=t_valid, block=block, has_state=has_state)
    zspec = pl.BlockSpec((rows, W_GROUP), lambda i: (i, 3))
    consts = [_const_spec((8, W_BRANCH)), _const_spec((1, W_BRANCH))]
    scratch = [pltpu.VMEM((rows, W_BRANCH), F32)] * 5
    if has_state:
        srows = seq_per_tile * W_BRANCH
        in_specs = [zspec, pl.BlockSpec((srows, HEAD_DIM), lambda i: (i, 0))] + consts
        s_spec = pl.BlockSpec((srows, HEAD_DIM), lambda i: (i, 0))
        args = (z, state_in, lb_rows, norm_w)
    else:
        in_specs = [zspec] + consts
        s_spec = pl.BlockSpec((W_BRANCH, HEAD_DIM), lambda i: (i // tiles_per_seq, 0))
        scratch = scratch + [pltpu.VMEM((W_BRANCH, W_BRANCH), F32)]
        args = (z, lb_rows, norm_w)
    return pl.pallas_call(
        kernel, grid=(n // rows,), in_specs=in_specs,
        out_specs=[pl.BlockSpec((rows, W_BRANCH), lambda i: (i, 0)), s_spec],
        out_shape=[jax.ShapeDtypeStruct((n, W_BRANCH), F32), jax.ShapeDtypeStruct((b * W_BRANCH, HEAD_DIM), F32)],
        scratch_shapes=scratch, compiler_params=_params(), name="hgrn")(*args)


def _merge_kernel(x_ref, oa_ref, ob_ref, oc_ref, od_ref, wg_ref, wb_ref, wo_ref, ln_ref, h_ref, *, alpha):
    x = x_ref[...]
    xb = x.astype(BF)
    m = jnp.zeros(x.shape, F32)
    for n, o_ref in enumerate((oa_ref, ob_ref, oc_ref, od_ref)):
        gate = _sigmoid(_dot(xb, wg_ref[:, n * D_MODEL:(n + 1) * D_MODEL]))
        m = m + gate * _dot(o_ref[...].astype(BF), wb_ref[n])
    mix = _dot(m.astype(BF), wo_ref[...])
    h_ref[...] = _layer_norm(alpha * x + mix, ln_ref[0:1], ln_ref[1:2])


def _merge(x, oa, ob, oc, od, wg, wb, wo, ln, alpha):
    n = x.shape[0]
    tm = min(ROW_TILE, n)
    row = lambda w: pl.BlockSpec((tm, w), lambda i: (i, 0))
    return pl.pallas_call(
        functools.partial(_merge_kernel, alpha=alpha), grid=(n // tm,),
        in_specs=[row(D_MODEL), row(256), row(256), row(256), row(256), _const_spec(wg.shape), _const_spec(wb.shape),
                  _const_spec(wo.shape), _const_spec((8, D_MODEL))],
        out_specs=row(D_MODEL), out_shape=jax.ShapeDtypeStruct((n, D_MODEL), F32),
        compiler_params=_params(), name="merge")(x, oa, ob, oc, od, wg, wb, wo, ln)


def _ffn_kernel(*refs, seq_len, has_state, n_split, alpha):
    if has_state:
        h_ref, p1_ref, p2_ref, wup_ref, wdn_ref, cv_ref, ln_ref, y_ref, a_ref = refs
    else:
        h_ref, wup_ref, wdn_ref, cv_ref, ln_ref, y_ref, a_ref, last_s = refs
    rows = h_ref.shape[0]
    i = pl.program_id(0)
    h = h_ref[...]
    hb = h.astype(BF)
    ridx = _iota((rows, 1), 0)
    t_row = (i * rows + ridx) % seq_len
    wf = D_FF // n_split
    f = jnp.zeros((rows, D_MODEL), F32)
    if not has_state:
        @pl.when((i * rows) % seq_len == 0)
        def _():
            last_s[...] = jnp.zeros_like(last_s)
    for j in range(n_split):
        lo, hi = j * wf, (j + 1) * wf
        a = _dot(hb, wup_ref[:, lo:hi])
        gt = _dot(hb, wup_ref[:, D_FF + lo:D_FF + hi])
        r1 = pltpu.roll(a, 1, 0)
        r2 = pltpu.roll(a, 2, 0)
        if has_state:
            prev1 = jnp.where(t_row == 0, p1_ref[:, lo:hi], r1)
            prev2 = jnp.where(t_row < 2, p2_ref[:, lo:hi], r2)
            a_ref[:, lo:hi] = a
        else:
            c6 = last_s[6:7, lo:hi]
            c7 = last_s[7:8, lo:hi]
            prev1 = jnp.where(ridx == 0, c7, r1)
            prev2 = jnp.where(ridx == 0, c6, jnp.where(ridx == 1, c7, r2))
            last_s[:, lo:hi] = a[rows - 8:rows]
            a_ref[:, lo:hi] = a[rows - 8:rows]
        conv = cv_ref[3:4, lo:hi] + prev2 * cv_ref[0:1, lo:hi] + prev1 * cv_ref[1:2, lo:hi] + a * cv_ref[2:3, lo:hi]
        hid = _gelu(conv) * gt
        f = f + _dot(hid.astype(BF), wdn_ref[lo:hi, :])
    y_ref[...] = _layer_norm(alpha * h + f, ln_ref[0:1], ln_ref[1:2])


def _ffn(h, p1, p2, wup, wdn, cv, ln, seq_len, alpha):
    n = h.shape[0]
    has_state = p1 is not None
    tm = min(ROW_TILE, n)
    row = lambda w: pl.BlockSpec((tm, w), lambda i: (i, 0))
    kernel = functools.partial(_ffn_kernel, seq_len=seq_len, has_state=has_state, n_split=2, alpha=alpha)
    consts = [_const_spec(wup.shape), _const_spec(wdn.shape), _const_spec((8, D_FF)), _const_spec((8, D_MODEL))]
    if has_state:
        in_specs = [row(D_MODEL), row(D_FF), row(D_FF)] + consts
        a_spec, a_rows, scratch = row(D_FF), n, []
        args = (h, p1, p2, wup, wdn, cv, ln)
    else:
        in_specs = [row(D_MODEL)] + consts
        a_spec, a_rows = pl.BlockSpec((8, D_FF), lambda i: (i, 0)), (n // tm) * 8
        scratch = [pltpu.VMEM((8, D_FF), F32)]
        args = (h, wup, wdn, cv, ln)
    return pl.pallas_call(
        kernel, grid=(n // tm,), in_specs=in_specs, out_specs=[row(D_MODEL), a_spec],
        out_shape=[jax.ShapeDtypeStruct((n, D_MODEL), F32), jax.ShapeDtypeStruct((a_rows, D_FF), F32)],
        scratch_shapes=scratch, compiler_params=_params(), name="ffn")(*args)


def _pad_lanes(x, width):
    return jnp.pad(x, [(0, 0)] * (x.ndim - 1) + [(0, width - x.shape[-1])])


def _regroup_w_in(w_in):
    o = 0
    cols = {}
    for name, w in (('fox_q', 256), ('fox_k', 256), ('fox_v', 256), ('fox_f', 4), ('dsa_q', 256), ('dsa_k', 256),
                    ('dsa_v', 256), ('idx_q', 128), ('idx_k', 32), ('idx_w', 4), ('rwkv', 1024), ('hgrn', 1024),
                    ('gate', 4096)):
        cols[name] = w_in[..., o:o + w]
        o += w
    g0 = _pad_lanes(jnp.concatenate([cols['fox_k'], cols['fox_v'], cols['fox_q'], cols['fox_f']], -1), W_GROUP)
    g1 = _pad_lanes(jnp.concatenate([cols['dsa_k'], cols['dsa_v'], cols['dsa_q'], cols['idx_q'], cols['idx_k'],
                                     cols['idx_w']], -1), W_GROUP)
    wz = jnp.concatenate([g0, g1, cols['rwkv'], cols['hgrn']], -1).astype(BF)
    return wz, cols['gate'].astype(BF)


def _rows8(*vecs, width):
    rows = [v.reshape(1, width) for v in vecs]
    rows.append(jnp.zeros((8 - len(rows), width), F32))
    return jnp.concatenate(rows, axis=0)


def _expand_first_rows(state, tt, offsets):
    b, _, w = state.shape
    out = jnp.zeros((b, tt, w), state.dtype)
    for s, ts in offsets:
        out = out.at[:, ts].set(state[:, s])
    return out.reshape(b * tt, w)


def _layer(x, cfg, lw):
    b, tt, tv, past = cfg['b'], cfg['tt'], cfg['tv'], cfg['past']
    decode = past > 0
    z = _inproj(x, lw['wz'])
    topk = max(1, min(DSA_TOPK, (past + tv) // 4))
    new = {}
    if decode:
        o_a, logf = _fox_decode(z, lw['fox_bf'], cfg['fox_kv'], cfg['fox_lf'], cfg['page_table'], cfg['layer'], b, tt)
    else:
        q_a, kv_a, logf, ccol, crow = _fox_prep(z, lw['fox_bf'], b, tt)
        o_a = _fox_attn(q_a, kv_a, ccol, crow, b, tt)
    new['fox_kv'] = z[:, 0:512]
    new['fox_logf'] = logf[:, 0:N_HEAD]
    kv_b, kvb_b, q_b, qi_b, misc_b, ki4_b = _dsa_prep(z, cfg['tab_k'], cfg['tab_i'], F32 if decode else BF)
    if decode:
        keep = _dsa_select(qi_b, misc_b, cfg['dsa_ki'], cfg['page_table'], cfg['layer'], b, tt, topk, tv)
        o_b = _dsa_decode(q_b, keep, kv_b, cfg['dsa_kv'], cfg['page_table'], cfg['layer'], b, tt)
    else:
        o_b = _dsa_attn(q_b, qi_b, misc_b, kvb_b, ki4_b, b, tt, topk)
    new['dsa_kv'] = kv_b
    new['dsa_kidx'] = misc_b[:, 0:D_IDX]
    o_c, new['rwkv'] = _rwkv(z, cfg.get('shift_rows'), cfg.get('rwkv_state'), lw['rwkv_mu'], lw['rwkv_vec'], lw['rwkv_wa'],
                             lw['rwkv_g2'], b, tt, tv)
    new['shift'] = z[:, 2 * W_GROUP:3 * W_GROUP]
    o_d, new['hgrn'] = _hgrn(z, cfg.get('hgrn_state'), lw['hgrn_lb'], lw['hgrn_nw'], b, tt, tv)
    h = _merge(x, o_a, o_b, o_c, o_d, lw['wg'], lw['wb'], lw['wo'], lw['ln1'], cfg['alpha'])
    y, new['conv'] = _ffn(h, cfg.get('conv_p1'), cfg.get('conv_p2'), lw['wup'], lw['wdn'], lw['conv'], lw['ln2'], tt, cfg['alpha'])
    return y, new


def kernel(x_prompt, x_sample, cache_fox_kv, cache_fox_logf, cache_dsa_kv, cache_dsa_kidx, state_rwkv, state_rwkv_shift, state_hgrn, state_ffn_conv, page_table, w_in, fox_bf, rwkv_mu, rwkv_w0, rwkv_w2, rwkv_a0, rwkv_a2, rwkv_g2, rwkv_kk, rwkv_ka, rwkv_rk, rwkv_ln_w, rwkv_ln_b, hgrn_lb, hgrn_norm_w, w_branch, w_o, ln1_g, ln1_b, ln2_g, ln2_b, ffn_w_in, ffn_conv_w, ffn_conv_b, ffn_w_out):
    depth = w_in.shape[0]
    bp, tp, d = x_prompt.shape
    bs, ts, _ = x_sample.shape
    n_pool, page = cache_fox_kv.shape[1], cache_fox_kv.shape[2]
    n_pages = page_table.shape[1]
    past = n_pages * page
    tsp = -(-ts // T_ALIGN) * T_ALIGN
    assert d == D_MODEL and tp % min(ROW_TILE, tp) == 0 and (bs * tsp) % min(ROW_TILE, bs * tsp) == 0
    assert tsp <= 128 and ts >= 2

    wz_all, wg_all = _regroup_w_in(w_in)
    wb_all, wo_all = w_branch.astype(BF), w_o.astype(BF)
    wup_all, wdn_all = ffn_w_in.astype(BF), ffn_w_out.astype(BF)
    zero_l = jnp.zeros((depth, LORA_W, W_BRANCH), F32)
    wa_all = jnp.concatenate([jnp.concatenate([rwkv_w2, zero_l], 2), jnp.concatenate([zero_l, rwkv_a2], 2)], 1).astype(BF)
    g2_all = rwkv_g2.astype(BF)
    lb_all = _hgrn_lb(hgrn_lb)
    page_table = page_table.astype(I32)

    fox_kv_pages = jnp.transpose(cache_fox_kv, (0, 1, 3, 4, 5, 2)).reshape(depth, n_pool, 2 * W_BRANCH, page)
    dsa_kv_pages = jnp.transpose(cache_dsa_kv, (0, 1, 3, 4, 5, 2)).reshape(depth, n_pool, 2 * W_BRANCH, page)
    dsa_ki_pages = jnp.swapaxes(cache_dsa_kidx, 2, 3)
    fox_lf_pages = _pad_rows_nd(jnp.swapaxes(cache_fox_logf, 2, 3), 8)

    pos_p = jnp.arange(tp)
    pos_s = past + (jnp.arange(bs * tsp) % tsp)[:min(ROW_TILE, bs * tsp)]
    alpha = (2 * depth) ** 0.25
    cfg_p = dict(b=bp, tt=tp, tv=tp, past=0, alpha=alpha,
                 tab_k=_rope_tables(pos_p, 256, HEAD_DIM, ROPE_HALF_QK), tab_i=_rope_tables(pos_p, 128, D_IDX, ROPE_HALF_IDX))
    cfg_s = dict(b=bs, tt=tsp, tv=ts, past=past, alpha=alpha, page_table=page_table, fox_kv=fox_kv_pages, fox_lf=fox_lf_pages,
                 dsa_kv=dsa_kv_pages, dsa_ki=dsa_ki_pages,
                 tab_k=_rope_tables(pos_s, 256, HEAD_DIM, ROPE_HALF_QK), tab_i=_rope_tables(pos_s, 128, D_IDX, ROPE_HALF_IDX))

    xp = x_prompt.reshape(bp * tp, d)
    xs = jnp.pad(x_sample, ((0, 0), (0, tsp - ts), (0, 0))).reshape(bs * tsp, d)
    new_p, new_s = [], []
    for l in range(depth):
        lw = dict(wz=wz_all[l], wg=wg_all[l], wb=wb_all[l], wo=wo_all[l], wup=wup_all[l], wdn=wdn_all[l],
                  fox_bf=_pad_lanes(fox_bf[l][None], 128), rwkv_mu=rwkv_mu[l][None],
                  rwkv_vec=_rows8(rwkv_w0[l], rwkv_a0[l], rwkv_kk[l], rwkv_ka[l], rwkv_rk[l], rwkv_ln_w[l], rwkv_ln_b[l],
                                  width=W_BRANCH),
                  rwkv_wa=wa_all[l], rwkv_g2=g2_all[l], hgrn_lb=lb_all[l], hgrn_nw=hgrn_norm_w[l][None],
                  ln1=_rows8(ln1_g[l], ln1_b[l], width=D_MODEL), ln2=_rows8(ln2_g[l], ln2_b[l], width=D_MODEL),
                  conv=_rows8(ffn_conv_w[l, 0], ffn_conv_w[l, 1], ffn_conv_w[l, 2], ffn_conv_b[l], width=D_FF))
        xp, st_p = _layer(xp, cfg_p, lw)
        cfg_l = dict(cfg_s, layer=l,
                     shift_rows=_expand_first_rows(state_rwkv_shift[l][:, None], tsp, ((0, 0),)),
                     rwkv_state=state_rwkv[l].reshape(bs * W_BRANCH, HEAD_DIM),
                     hgrn_state=state_hgrn[l].reshape(bs * W_BRANCH, HEAD_DIM),
                     conv_p1=_expand_first_rows(state_ffn_conv[l], tsp, ((1, 0),)),
                     conv_p2=_expand_first_rows(state_ffn_conv[l], tsp, ((0, 0), (1, 1))))
        xs, st_s = _layer(xs, cfg_l, lw)
        new_p.append(st_p)
        new_s.append(st_s)

    def assemble(new, b, tt, tv, decode):
        def rows(name, shape):
            a = jnp.stack([n[name] for n in new]).reshape(depth, b, tt, -1)[:, :, :tv]
            return a.reshape((depth, b, tv) + shape)

        fox_kv = rows('fox_kv', (2, N_HEAD, HEAD_DIM))
        fox_logf = rows('fox_logf', (N_HEAD,))
        dsa_kv = rows('dsa_kv', (2, N_HEAD, HEAD_DIM))
        dsa_kidx = rows('dsa_kidx', (D_IDX,))
        rwkv = jnp.stack([n['rwkv'] for n in new]).reshape(depth, b, N_HEAD, HEAD_DIM, HEAD_DIM)
        hgrn = jnp.stack([n['hgrn'] for n in new]).reshape(depth, b, N_HEAD, HEAD_DIM, HEAD_DIM)
        shift = jnp.stack([n['shift'] for n in new]).reshape(depth, b, tt, W_GROUP)[:, :, tv - 1]
        conv = jnp.stack([n['conv'] for n in new])
        if decode:
            conv = conv.reshape(depth, b, tt, D_FF)[:, :, tv - 2:tv]
        else:
            conv = conv.reshape(depth, b, -1, 8, D_FF)[:, :, -1, 6:8]
        return fox_kv, fox_logf, dsa_kv, dsa_kidx, rwkv, shift, hgrn, conv

    y_p = xp.reshape(bp, tp, d)
    y_s = xs.reshape(bs, tsp, d)[:, :ts]
    return (y_p, y_s) + assemble(new_p, bp, tp, tp, False) + assemble(new_s, bs, tsp, ts, True)


def _pad_rows_nd(x, rows):
    pad = [(0, 0)] * x.ndim
    pad[-2] = (0, rows - x.shape[-2])
    return jnp.pad(x, pad)
```

```python
import functools
import math

import jax
import jax.numpy as jnp
import numpy as np
from jax import lax
from jax.experimental import pallas as pl
from jax.experimental.pallas import tpu as pltpu

F32 = jnp.float32
BF = jnp.bfloat16
I32 = jnp.int32
I16 = jnp.int16

D_MODEL = 1024
N_HEAD = 4
HEAD_DIM = 64
W_BRANCH = N_HEAD * HEAD_DIM
D_IDX = 32
W_IDX = N_HEAD * D_IDX
LORA_W = 64
LORA_A = 64
LORA_G = 128
D_FF = 2816
N_GROUP = 4
W_GROUP = 1024
DSA_TOPK = 256
ROPE_THETA = 500000.0
ROPE_HALF_QK = 8
ROPE_HALF_IDX = 4
RWKV_GN_EPS = 64e-5
RMS_EPS = 1e-6
LB_TINY = 1e-30
LN_EPS = 1e-5
NEG = -1e30
ROW_TILE = 256
KEY_CHUNK = 512
FLASH_KEYS = 512
SELECT_SEQS = 8
RWKV_CHUNK = 64
HGRN_BLOCK = 16
T_ALIGN = 8
INT_MIN = -2 ** 31


def _dot(a, b):
    return jnp.dot(a, b, preferred_element_type=F32)


def _dot_nt(a, b):
    return lax.dot_general(a, b, (((1,), (1,)), ((), ())), preferred_element_type=F32)


def _dot_tn(a, b):
    return lax.dot_general(a, b, (((0,), (0,)), ((), ())), preferred_element_type=F32)


def _split3(x):
    hi = x.astype(BF)
    r = x - hi.astype(F32)
    mid = r.astype(BF)
    lo = (r - mid.astype(F32)).astype(BF)
    return hi, mid, lo


def _dot_sel_l(sel_bf, x):
    hi, mid, lo = _split3(x)
    return _dot(sel_bf, hi) + _dot(sel_bf, mid) + _dot(sel_bf, lo)


def _dot_sel_r(x, sel_bf):
    hi, mid, lo = _split3(x)
    return _dot(hi, sel_bf) + _dot(mid, sel_bf) + _dot(lo, sel_bf)


def _dot_sel_tn(x, sel_bf):
    hi, mid, lo = _split3(x)
    return _dot_tn(hi, sel_bf) + _dot_tn(mid, sel_bf) + _dot_tn(lo, sel_bf)


def _iota(shape, dim):
    return lax.broadcasted_iota(I32, shape, dim)


def _tril(n, strict=False):
    r, c = _iota((n, n), 0), _iota((n, n), 1)
    return (r > c) if strict else (r >= c)


def _head_lane(width, head_width):
    return _iota((1, width), 1) // head_width


def _block_diag_mask(n, block):
    return (_iota((n, n), 0) // block) == (_iota((n, n), 1) // block)


def _stack_heads(x, head_width=HEAD_DIM):
    hl = _head_lane(x.shape[1], head_width)
    return jnp.concatenate([jnp.where(hl == h, x, jnp.zeros_like(x)) for h in range(N_HEAD)], axis=0)


def _unstack_heads(xs):
    c = xs.shape[0] // N_HEAD
    out = xs[0:c]
    for h in range(1, N_HEAD):
        out = out + xs[h * c:(h + 1) * c]
    return out


def _head_sum(x, ones_bd):
    return _dot_sel_r(x, ones_bd)


def _sigmoid(x):
    return 1.0 / (1.0 + jnp.exp(-x))


def _log_sigmoid(x):
    return jnp.minimum(x, 0.0) - jnp.log1p(jnp.exp(-jnp.abs(x)))


def _softplus(x):
    return jnp.maximum(x, 0.0) + jnp.log1p(jnp.exp(-jnp.abs(x)))


def _gelu(x):
    return 0.5 * x * (1.0 + lax.erf(x * (2.0 ** -0.5)))


def _layer_norm(x, g, b):
    mu = jnp.mean(x, axis=-1, keepdims=True)
    xc = x - mu
    var = jnp.mean(xc * xc, axis=-1, keepdims=True)
    return xc * lax.rsqrt(var + LN_EPS) * g + b


def _params(n_axes=1):
    return pltpu.CompilerParams(dimension_semantics=("arbitrary",) * n_axes)


def _const_spec(shape):
    nd = len(shape)
    return pl.BlockSpec(shape, lambda *_: (0,) * nd)


def _inproj_kernel(x_ref, w_ref, z_ref):
    z_ref[...] = _dot(x_ref[...].astype(BF), w_ref[...])


def _inproj(x, wz):
    n = x.shape[0]
    tm = min(ROW_TILE, n)
    nz = wz.shape[1]
    return pl.pallas_call(
        _inproj_kernel, grid=(n // tm,),
        in_specs=[pl.BlockSpec((tm, D_MODEL), lambda i: (i, 0)), _const_spec((D_MODEL, nz))],
        out_specs=pl.BlockSpec((tm, nz), lambda i: (i, 0)),
        out_shape=jax.ShapeDtypeStruct((n, nz), F32), compiler_params=_params(), name="inproj")(x, wz)


def _fox_prep_kernel(z_ref, bf_ref, qt_ref, k_ref, vt_ref, logf_ref, ccol_ref, crow_ref):
    t = z_ref.shape[0]
    ck = vt_ref.shape[2]
    tq = crow_ref.shape[2]
    k_ref[...] = z_ref[:, 0:256].astype(BF)
    qt_ref[...] = (z_ref[:, 512:768] * (HEAD_DIM ** -0.5)).T.astype(BF)
    logf = _log_sigmoid(z_ref[:, 768:896] + bf_ref[...])
    logf_ref[...] = logf
    tri = _tril(ck).astype(BF)
    carry = jnp.zeros((1, 128), F32)
    for c in range(t // ck):
        rows = slice(c * ck, (c + 1) * ck)
        vt_ref[c] = z_ref[rows, 256:512].T.astype(BF)
        cum = _dot_sel_l(tri, logf[rows]) + carry
        carry = cum[ck - 1:ck]
        ccol_ref[rows, :] = cum
        cum_t = cum.T[0:8]
        for j in range(ck // tq):
            crow_ref[c * (ck // tq) + j] = cum_t[:, j * tq:(j + 1) * tq]


def _fox_prep(z, bf_row, b, t):
    ck = min(KEY_CHUNK, t)
    tq = min(ROW_TILE, t)
    nc, nq = t // ck, t // tq
    rows = lambda w: pl.BlockSpec((t, w), lambda i: (i, 0))
    return pl.pallas_call(
        _fox_prep_kernel, grid=(b,),
        in_specs=[rows(W_GROUP), _const_spec((1, 128))],
        out_specs=[pl.BlockSpec((W_BRANCH, t), lambda i: (0, i)), rows(W_BRANCH),
                   pl.BlockSpec((nc, W_BRANCH, ck), lambda i: (i, 0, 0)), rows(128), rows(128),
                   pl.BlockSpec((None, nq, 8, tq), lambda i: (i, 0, 0, 0))],
        out_shape=[jax.ShapeDtypeStruct((W_BRANCH, b * t), BF), jax.ShapeDtypeStruct((b * t, W_BRANCH), BF),
                   jax.ShapeDtypeStruct((b * nc, W_BRANCH, ck), BF),
                   jax.ShapeDtypeStruct((b * t, 128), F32), jax.ShapeDtypeStruct((b * t, 128), F32),
                   jax.ShapeDtypeStruct((b, nq, 8, tq), F32)],
        compiler_params=_params(), name="fox_prep")(z, bf_row)


def _flash_init(m_s, l_s, acc_s):
    m_s[...] = jnp.full(m_s.shape, NEG, F32)
    l_s[...] = jnp.zeros(l_s.shape, F32)
    acc_s[...] = jnp.zeros(acc_s.shape, F32)


def _flash_update(h, st, vt_h, m_s, l_s, acc_s):
    m_old = m_s[h]
    m_new = jnp.maximum(m_old, jnp.max(st, axis=0, keepdims=True))
    a = jnp.exp(m_old - m_new)
    p = jnp.exp(st - m_new)
    l_s[h] = a * l_s[h] + jnp.sum(p, axis=0, keepdims=True)
    acc_s[h] = a * acc_s[h] + _dot(vt_h, p.astype(BF))
    m_s[h] = m_new


def _flash_result(l_s, acc_s):
    return jnp.concatenate([acc_s[h] / l_s[h] for h in range(N_HEAD)], axis=0).T


def _flash_scratch(tq):
    return [pltpu.VMEM((N_HEAD, 1, tq), F32), pltpu.VMEM((N_HEAD, 1, tq), F32), pltpu.VMEM((N_HEAD, HEAD_DIM, tq), F32)]


def _head_rows(x, head_rows):
    rh = _iota((x.shape[0], 1), 0) // head_rows
    return [jnp.where(rh == h, x, jnp.zeros_like(x)) for h in range(N_HEAD)]


def _causal_chunks(i, tq, ck):
    return (i * tq) // ck + 1


def _fox_attn_kernel(qt_ref, k_ref, vt_ref, ccol_ref, crow_ref, o_ref, m_s, l_s, acc_s):
    tq = qt_ref.shape[1]
    ck = vt_ref.shape[2]
    i = pl.program_id(1)
    qpos = i * tq + _iota((1, tq), 1)
    qth = _head_rows(qt_ref[...], HEAD_DIM)
    cq = [crow_ref[h:h + 1, :] for h in range(N_HEAD)]
    _flash_init(m_s, l_s, acc_s)

    def chunk(c, carry):
        for j in range(ck // FLASH_KEYS):
            rows = pl.ds(pl.multiple_of(c * ck + j * FLASH_KEYS, FLASH_KEYS), FLASH_KEYS)
            cols = slice(j * FLASH_KEYS, (j + 1) * FLASH_KEYS)
            k = k_ref[rows, :]
            ccol = ccol_ref[rows, :]
            visible = (c * ck + j * FLASH_KEYS + _iota((FLASH_KEYS, 1), 0)) <= qpos
            for h in range(N_HEAD):
                st = _dot(k, qth[h]) + cq[h] - ccol[:, h:h + 1]
                _flash_update(h, jnp.where(visible, st, NEG), vt_ref[c, h * HEAD_DIM:(h + 1) * HEAD_DIM, cols], m_s, l_s, acc_s)
        return carry

    lax.fori_loop(0, _causal_chunks(i, tq, ck), chunk, 0)
    o_ref[...] = _flash_result(l_s, acc_s)


def _fox_attn(qt, k, vt, ccol, crow, b, t):
    tq = crow.shape[3]
    nq = t // tq
    ck = vt.shape[2]
    nc = t // ck
    return pl.pallas_call(
        _fox_attn_kernel, grid=(b, nq),
        in_specs=[pl.BlockSpec((W_BRANCH, tq), lambda bi, i: (0, bi * nq + i)),
                  pl.BlockSpec((t, W_BRANCH), lambda bi, i: (bi, 0)),
                  pl.BlockSpec((nc, W_BRANCH, ck), lambda bi, i: (bi, 0, 0)),
                  pl.BlockSpec((t, 128), lambda bi, i: (bi, 0)),
                  pl.BlockSpec((None, None, 8, tq), lambda bi, i: (bi, i, 0, 0))],
        out_specs=pl.BlockSpec((tq, W_BRANCH), lambda bi, i: (bi * nq + i, 0)),
        out_shape=jax.ShapeDtypeStruct((b * t, W_BRANCH), F32), scratch_shapes=_flash_scratch(tq),
        compiler_params=_params(2), name="fox_attn")(qt, k, vt, ccol, crow)


def _page_specs(n_pages, block, layer):
    nd = len(block)

    def mk(p):
        return pl.BlockSpec((None, None) + block, lambda b, pt: (layer, pt[b, p]) + (0,) * nd)

    return [mk(p) for p in range(n_pages)]


def _softmax_pv(s, vt_past, v_new, past):
    m = jnp.max(s, axis=1, keepdims=True)
    p = jnp.exp(s - m)
    l = jnp.sum(p, axis=1, keepdims=True)
    pb = p.astype(BF)
    o = _dot_nt(pb[:, :past], vt_past) + _dot(pb[:, past:], v_new)
    o = _stack_mask(o / l)
    return _unstack_heads(o)


def _cat_pages(refs, lo, hi):
    return jnp.concatenate([r[lo:hi, :] for r in refs], axis=1)


def _stack_mask(o):
    c = o.shape[0] // N_HEAD
    hl = _head_lane(o.shape[1], HEAD_DIM)
    row_h = _iota((o.shape[0], 1), 0) // c
    return jnp.where(row_h == hl, o, 0.0)


def _pad_rows(x, rows):
    return jnp.concatenate([x, jnp.zeros((rows - x.shape[0], x.shape[1]), x.dtype)], axis=0)


def _fox_decode_kernel(pt_ref, z_ref, bf_ref, *refs, n_pages):
    kv_refs = refs[:n_pages]
    lf_refs = refs[n_pages:2 * n_pages]
    o_ref, logf_ref = refs[2 * n_pages:]
    tt = z_ref.shape[0]
    page = kv_refs[0].shape[1]
    past = n_pages * page
    k_new = z_ref[:, 0:256]
    v_new = z_ref[:, 256:512]
    q = z_ref[:, 512:768] * (HEAD_DIM ** -0.5)
    logf = _log_sigmoid(z_ref[:, 768:896] + bf_ref[...])
    logf_ref[...] = logf

    lf = jnp.concatenate([r[...] for r in lf_refs], axis=0)
    n = lf.shape[0]
    tri_u = (_iota((page, page), 0) <= _iota((page, page), 1)).astype(BF)
    in_page = _dot_sel_r(lf, tri_u)
    tot = jnp.broadcast_to(in_page[:, page - 1:page], (n, page))
    r, c = _iota((n, n), 0), _iota((n, n), 1)
    later = jnp.where(((r % 8) == (c % 8)) & ((c // 8) >= (r // 8)), -1.0, 0.0).astype(BF)
    ck_rel = in_page + _dot_sel_l(later, tot)

    cn_col = _dot_sel_l(_tril(tt).astype(BF), logf)
    tri_pad = (_iota((tt, 128), 0) <= _iota((tt, 128), 1)).astype(BF)
    cn_row = _dot_sel_tn(logf, tri_pad)

    qbd = _stack_heads(q).astype(BF)
    s_past = _dot(qbd, _cat_pages(kv_refs, 0, W_BRANCH).astype(BF))
    vt_past = _cat_pages(kv_refs, W_BRANCH, 2 * W_BRANCH).astype(BF)
    s_new = _dot_nt(qbd, _pad_rows(k_new, 128).astype(BF))

    cq = jnp.concatenate([cn_col[:, h:h + 1] for h in range(N_HEAD)], axis=0)
    bias_past = jnp.concatenate(
        [jnp.concatenate([jnp.broadcast_to(ck_rel[p * 8 + h:p * 8 + h + 1], (tt, page)) for h in range(N_HEAD)], axis=0)
         for p in range(n_pages)], axis=1)
    bias_new = jnp.concatenate([jnp.broadcast_to(cn_row[h:h + 1], (tt, 128)) for h in range(N_HEAD)], axis=0)
    tq = _iota((N_HEAD * tt, 128), 0) % tt
    ok_new = _iota((N_HEAD * tt, 128), 1) <= tq
    s = jnp.concatenate([s_past + cq - bias_past, jnp.where(ok_new, s_new + cq - bias_new, NEG)], axis=1)
    o_ref[...] = _softmax_pv(s, vt_past, _pad_rows(v_new, 128).astype(BF), past)


def _fox_decode(z, bf_row, cache_kv, cache_lf, page_table, layer, b, tt):
    n_pages = page_table.shape[1]
    page = cache_kv.shape[3]
    kernel = functools.partial(_fox_decode_kernel, n_pages=n_pages)
    grid_spec = pltpu.PrefetchScalarGridSpec(
        num_scalar_prefetch=1, grid=(b,),
        in_specs=[pl.BlockSpec((tt, W_GROUP), lambda i, pt: (i, 0)), pl.BlockSpec((1, 128), lambda i, pt: (0, 0))]
        + _page_specs(n_pages, (2 * W_BRANCH, page), layer) + _page_specs(n_pages, (8, page), layer),
        out_specs=[pl.BlockSpec((tt, W_BRANCH), lambda i, pt: (i, 0)), pl.BlockSpec((tt, 128), lambda i, pt: (i, 0))])
    return pl.pallas_call(
        kernel, grid_spec=grid_spec,
        out_shape=[jax.ShapeDtypeStruct((b * tt, W_BRANCH), F32), jax.ShapeDtypeStruct((b * tt, 128), F32)],
        compiler_params=_params(), name="fox_decode")(page_table, z, bf_row, *([cache_kv] * n_pages), *([cache_lf] * n_pages))


def _rope_tables(pos, n_lanes, head_width, half):
    inv = jnp.power(ROPE_THETA, -jnp.arange(half, dtype=F32) / half)
    ang = pos.astype(F32)[:, None] * inv[None]
    d = np.arange(n_lanes) % head_width
    first = jnp.asarray(d < half)[None]
    second = jnp.asarray((d >= half) & (d < 2 * half))[None]
    cos = jnp.cos(ang)[:, d % half]
    sin = jnp.sin(ang)[:, d % half]
    c = jnp.where(first | second, cos, 1.0)
    s_up = jnp.where(first, -sin, 0.0)
    s_dn = jnp.where(second, sin, 0.0)
    return jnp.concatenate([c, s_up, s_dn], axis=1).astype(F32)


def _rope(x, tab, half):
    w = x.shape[1]
    return x * tab[:, 0:w] + pltpu.roll(x, w - half, 1) * tab[:, w:2 * w] + pltpu.roll(x, half, 1) * tab[:, 2 * w:3 * w]


def _dsa_prep_kernel(z_ref, tk_ref, ti_ref, kv_ref, misc_ref, *out_refs, prefill):
    tk = tk_ref[...]
    ti = ti_ref[...]
    k = _rope(z_ref[:, 0:256], tk, ROPE_HALF_QK)
    v = z_ref[:, 256:512]
    q = _rope(z_ref[:, 512:768], tk, ROPE_HALF_QK) * (HEAD_DIM ** -0.5)
    qi = _rope(z_ref[:, 768:896], ti, ROPE_HALF_IDX)
    tail = z_ref[:, 896:1024]
    ki = _rope(tail, ti, ROPE_HALF_IDX)
    kv_ref[...] = jnp.concatenate([k, v], axis=1)
    lane = _iota((1, 128), 1)
    misc = jnp.where(lane < D_IDX, ki, tail * (N_HEAD ** -0.5))
    misc_ref[...] = misc
    if prefill:
        qt_ref, k_ref, vt_ref, qit_ref, wt_ref, ki4_ref = out_refs
        qt_ref[...] = q.T.astype(BF)
        k_ref[...] = k.astype(BF)
        vt_ref[...] = v.T.astype(BF)
        qit_ref[...] = qi.T.astype(BF)
        wt_ref[...] = misc.T[D_IDX:D_IDX + 8]
        kim = jnp.where(lane < D_IDX, ki, 0.0)
        ki4 = kim + pltpu.roll(kim, 32, 1) + pltpu.roll(kim, 64, 1) + pltpu.roll(kim, 96, 1)
        ki4_ref[...] = ki4.astype(BF)
    else:
        q_ref, qi_ref = out_refs
        q_ref[...] = q
        qi_ref[...] = qi


def _dsa_prep(z, tab_k, tab_i, prefill):
    n = z.shape[0]
    tm = min(ROW_TILE, n)
    nt = tab_k.shape[0] // tm
    row = lambda w: pl.BlockSpec((tm, w), lambda i: (i, 0))
    col = lambda w: pl.BlockSpec((w, tm), lambda i: (0, i))
    out_specs = [row(512), row(128)]
    out_shape = [jax.ShapeDtypeStruct((n, 512), F32), jax.ShapeDtypeStruct((n, 128), F32)]
    if prefill:
        ck = 2 * tm
        out_specs += [col(256), row(256), pl.BlockSpec((None, W_BRANCH, tm), lambda i: (i // 2, 0, i % 2)),
                      col(128), col(8), row(128)]
        out_shape += [jax.ShapeDtypeStruct((256, n), BF), jax.ShapeDtypeStruct((n, 256), BF),
                      jax.ShapeDtypeStruct((n // ck, W_BRANCH, ck), BF), jax.ShapeDtypeStruct((128, n), BF),
                      jax.ShapeDtypeStruct((8, n), F32), jax.ShapeDtypeStruct((n, 128), BF)]
    else:
        out_specs += [row(256), row(128)]
        out_shape += [jax.ShapeDtypeStruct((n, 256), F32), jax.ShapeDtypeStruct((n, 128), F32)]
    return pl.pallas_call(
        functools.partial(_dsa_prep_kernel, prefill=prefill), grid=(n // tm,),
        in_specs=[pl.BlockSpec((tm, W_GROUP), lambda i: (i, 1)),
                  pl.BlockSpec((tm, 768), lambda i: (i % nt, 0)), pl.BlockSpec((tm, 384), lambda i: (i % nt, 0))],
        out_specs=out_specs, out_shape=out_shape, compiler_params=_params(), name="dsa_prep")(z, tab_k, tab_i)


def _sortable(x):
    b = pltpu.bitcast(x, I32)
    return b ^ ((b >> 31) & I32(0x7FFFFFFF))


def _topk_mask(score, idx, topk, n_idx_bits):
    key = _sortable(score)
    rows = score.shape[0]

    def count(pred):
        return jnp.sum(pred.astype(I32), axis=1, keepdims=True)

    def value_step(it, ans):
        cand = ans + jnp.left_shift(I32(1), I32(31) - it)
        return jnp.where(count(key >= cand) >= topk, cand, ans)

    thr = lax.fori_loop(0, 32, value_step, jnp.full((rows, 1), INT_MIN, I32))
    above = key > thr
    tie = key == thr
    need = topk - count(above)

    def index_step(it, lo):
        cand = lo + jnp.left_shift(I32(1), I32(n_idx_bits - 1) - it)
        return jnp.where(count(tie & (idx < cand)) < need, cand, lo)

    last = lax.fori_loop(0, n_idx_bits, index_step, jnp.zeros((rows, 1), I32))
    return above | (tie & (idx <= last))


def _dsa_attn_kernel(qt_ref, qit_ref, wt_ref, k_ref, vt_ref, ki4_ref, o_ref, hi_s, lo_s, tie_s, keep_s, m_s, l_s, acc_s, *,
                     topk, n_idx_bits):
    tq = qt_ref.shape[1]
    ck = hi_s.shape[1]
    i = pl.program_id(1)
    nc = _causal_chunks(i, tq, ck)
    qpos = i * tq + _iota((1, tq), 1)
    qpos16 = qpos.astype(I16)
    qith = _head_rows(qit_ref[...], D_IDX)
    w_row = [wt_ref[h:h + 1, :] for h in range(N_HEAD)]
    i16_min, i16_max = I16(-32768), I16(32767)

    def key_pos(c):
        return c * ck + _iota((ck, tq), 0)

    def score_chunk(c, carry):
        ki4 = ki4_ref[pl.ds(pl.multiple_of(c * ck, ck), ck), :]
        sc = jnp.zeros((ck, tq), F32)
        for h in range(N_HEAD):
            sc = sc + jnp.maximum(_dot(ki4, qith[h]) * (D_IDX ** -0.5), 0.0) * w_row[h]
        sc = jnp.where(key_pos(c) <= qpos, sc, NEG)
        key = _sortable(jnp.where(sc == 0.0, 0.0, sc))
        hi_s[c] = (key >> 16).astype(I16)
        lo_s[c] = ((key & I32(0xFFFF)) - I32(32768)).astype(I16)
        return carry

    lax.fori_loop(0, nc, score_chunk, 0)

    def count(pred):
        def body(c, acc):
            ones = jnp.where(pred(c), I16(1), I16(0))
            for j in range(ck // 16):
                acc = acc + ones[j * 16:(j + 1) * 16]
            return acc
        acc = lax.fori_loop(0, nc, body, jnp.zeros((16, tq), I16))
        return jnp.sum(acc.astype(I32), axis=0, keepdims=True)

    def search16(ref, want):
        def step(it, ans):
            cand = ans + jnp.left_shift(I32(1), I32(15) - it)
            c16 = cand.astype(I16)
            return jnp.where(count(lambda c: ref[c] >= c16) >= want, cand, ans)
        return lax.fori_loop(0, 16, step, jnp.full((1, tq), -32768, I32)).astype(I16)

    h16 = search16(hi_s, topk)
    n_above = count(lambda c: hi_s[c] > h16)

    def mask_lo(c, carry):
        lo_s[c] = jnp.where(hi_s[c] == h16, lo_s[c], i16_min)
        return carry

    lax.fori_loop(0, nc, mask_lo, 0)
    l16 = search16(lo_s, topk - n_above)
    need = topk - n_above - count(lambda c: lo_s[c] > l16)

    def tie_chunk(c, carry):
        kpos16 = key_pos(c).astype(I16)
        tie = (hi_s[c] == h16) & (lo_s[c] == l16) & (kpos16 <= qpos16)
        tie_s[c] = jnp.where(tie, kpos16, i16_max)
        return carry

    lax.fori_loop(0, nc, tie_chunk, 0)

    def index_step(it, lo):
        cand = lo + jnp.left_shift(I32(1), I32(n_idx_bits - 1) - it)
        c16 = cand.astype(I16)
        return jnp.where(count(lambda c: tie_s[c] < c16) < need, cand, lo)

    last16 = lax.fori_loop(0, n_idx_bits, index_step, jnp.zeros((1, tq), I32)).astype(I16)

    def keep_chunk(c, carry):
        kpos16 = key_pos(c).astype(I16)
        sel = jnp.where(hi_s[c] > h16, I16(1), I16(0)) + jnp.where(lo_s[c] > l16, I16(1), I16(0)) \
            + jnp.where(tie_s[c] <= last16, I16(1), I16(0))
        sel = jnp.where(kpos16 <= qpos16, sel, I16(0))
        keep_s[c] = jnp.where(sel.astype(I32) > 0, 0.0, NEG)
        return carry

    lax.fori_loop(0, nc, keep_chunk, 0)

    qth = _head_rows(qt_ref[...], HEAD_DIM)
    _flash_init(m_s, l_s, acc_s)

    def attend_chunk(c, carry):
        for j in range(ck // FLASH_KEYS):
            rows = pl.ds(pl.multiple_of(c * ck + j * FLASH_KEYS, FLASH_KEYS), FLASH_KEYS)
            cols = slice(j * FLASH_KEYS, (j + 1) * FLASH_KEYS)
            k = k_ref[rows, :]
            keep = keep_s[c, cols, :]
            for h in range(N_HEAD):
                _flash_update(h, _dot(k, qth[h]) + keep, vt_ref[c, h * HEAD_DIM:(h + 1) * HEAD_DIM, cols], m_s, l_s, acc_s)
        return carry

    lax.fori_loop(0, nc, attend_chunk, 0)
    o_ref[...] = _flash_result(l_s, acc_s)


def _dsa_attn(qt, k, vt, qit, wt, ki4, b, t, topk):
    tq = min(ROW_TILE, t)
    ck = vt.shape[2]
    nq, nc = t // tq, t // ck
    assert topk <= ck and t < 2 ** 15
    qcol = lambda w: pl.BlockSpec((w, tq), lambda bi, i: (0, bi * nq + i))
    seq = lambda w: pl.BlockSpec((t, w), lambda bi, i: (bi, 0))
    kernel = functools.partial(_dsa_attn_kernel, topk=topk, n_idx_bits=max(1, (t - 1).bit_length()))
    scratch = [pltpu.VMEM((nc, ck, tq), I16)] * 3 + [pltpu.VMEM((nc, ck, tq), F32)] + _flash_scratch(tq)
    return pl.pallas_call(
        kernel, grid=(b, nq),
        in_specs=[qcol(256), qcol(128), qcol(8), seq(256), pl.BlockSpec((nc, W_BRANCH, ck), lambda bi, i: (bi, 0, 0)),
                  seq(128)],
        out_specs=pl.BlockSpec((tq, W_BRANCH), lambda bi, i: (bi * nq + i, 0)),
        out_shape=jax.ShapeDtypeStruct((b * t, W_BRANCH), F32), scratch_shapes=scratch,
        compiler_params=_params(2), name="dsa_attn")(qt, qit, wt, k, vt, ki4)


def _dsa_select_kernel(pt_ref, qi_ref, misc_ref, *refs, n_pages, n_seq, tt, topk, t_valid):
    ki_refs = refs[:n_seq * n_pages]
    keep_ref = refs[n_seq * n_pages]
    page = ki_refs[0].shape[1]
    past = n_pages * page
    scores = []
    for j in range(n_seq):
        rows = slice(j * tt, (j + 1) * tt)
        qi = qi_ref[rows, :]
        qi_h = jnp.concatenate([qi[:, h * D_IDX:(h + 1) * D_IDX] for h in range(N_HEAD)], axis=0).astype(BF)
        w_col = jnp.concatenate([misc_ref[rows, D_IDX + h:D_IDX + h + 1] for h in range(N_HEAD)], axis=0)
        kit_past = _cat_pages(ki_refs[j * n_pages:(j + 1) * n_pages], 0, D_IDX).astype(BF)
        ki_new = _pad_rows(misc_ref[rows, 0:D_IDX], 128).astype(BF)
        s = jnp.concatenate([_dot(qi_h, kit_past), _dot_nt(qi_h, ki_new)], axis=1)
        scores.append(_unstack_heads(jnp.maximum(s * (D_IDX ** -0.5), 0.0) * w_col))
    score = jnp.concatenate(scores, axis=0)
    n_keys = past + 128
    idx = _iota((1, n_keys), 1)
    new_t = idx - past
    t_q = _iota((n_seq * tt, 1), 0) % tt
    visible = (new_t <= t_q) & (new_t < t_valid)
    score = jnp.where(visible, jnp.where(score == 0.0, 0.0, score), NEG)
    sel = _topk_mask(score, idx, topk, n_keys.bit_length()) & visible
    keep_ref[...] = jnp.where(sel, 0.0, NEG)


def _dsa_select(qi, misc, cache_ki, page_table, layer, b, tt, topk, t_valid):
    n_pages = page_table.shape[1]
    page = cache_ki.shape[3]
    n_seq = min(SELECT_SEQS, b)
    n_keys = n_pages * page + 128
    kernel = functools.partial(_dsa_select_kernel, n_pages=n_pages, n_seq=n_seq, tt=tt, topk=topk, t_valid=t_valid)

    def page_spec(j, p):
        return pl.BlockSpec((None, None, D_IDX, page), lambda g, pt: (layer, pt[g * n_seq + j, p], 0, 0))

    rows = lambda w: pl.BlockSpec((n_seq * tt, w), lambda g, pt: (g, 0))
    grid_spec = pltpu.PrefetchScalarGridSpec(
        num_scalar_prefetch=1, grid=(b // n_seq,),
        in_specs=[rows(128), rows(128)] + [page_spec(j, p) for j in range(n_seq) for p in range(n_pages)],
        out_specs=rows(n_keys))
    return pl.pallas_call(
        kernel, grid_spec=grid_spec, out_shape=jax.ShapeDtypeStruct((b * tt, n_keys), F32),
        compiler_params=_params(), name="dsa_select")(page_table, qi, misc, *([cache_ki] * (n_seq * n_pages)))


def _dsa_decode_kernel(pt_ref, q_ref, keep_ref, kvn_ref, *refs, n_pages):
    kv_refs = refs[:n_pages]
    o_ref = refs[n_pages]
    past = n_pages * kv_refs[0].shape[1]
    qbd = _stack_heads(q_ref[...]).astype(BF)
    k_new = _pad_rows(kvn_ref[:, 0:256], 128).astype(BF)
    v_new = _pad_rows(kvn_ref[:, 256:512], 128).astype(BF)
    s = jnp.concatenate([_dot(qbd, _cat_pages(kv_refs, 0, W_BRANCH).astype(BF)), _dot_nt(qbd, k_new)], axis=1)
    keep4 = jnp.concatenate([keep_ref[...]] * N_HEAD, axis=0)
    vt_past = _cat_pages(kv_refs, W_BRANCH, 2 * W_BRANCH).astype(BF)
    o_ref[...] = _softmax_pv(jnp.where(keep4 == 0.0, s, NEG), vt_past, v_new, past)


def _dsa_decode(q, keep, kv_new, cache_kv, page_table, layer, b, tt):
    n_pages = page_table.shape[1]
    page = cache_kv.shape[3]
    kernel = functools.partial(_dsa_decode_kernel, n_pages=n_pages)
    row = lambda w: pl.BlockSpec((tt, w), lambda i, pt: (i, 0))
    grid_spec = pltpu.PrefetchScalarGridSpec(
        num_scalar_prefetch=1, grid=(b,),
        in_specs=[row(256), row(keep.shape[1]), row(512)] + _page_specs(n_pages, (2 * W_BRANCH, page), layer),
        out_specs=row(256))
    return pl.pallas_call(
        kernel, grid_spec=grid_spec, out_shape=jax.ShapeDtypeStruct((b * tt, W_BRANCH), F32),
        compiler_params=_params(), name="dsa_decode")(page_table, q, keep, kv_new, *([cache_kv] * n_pages))


def _rwkv_chunk(r, k, v, lw, kap, beta, s_big, n_double):
    c = r.shape[0]
    cc = N_HEAD * c
    g = _dot_sel_l(_tril(c).astype(BF), lw)
    g_end = g[c - 1:c]
    e_neg = jnp.exp(-g)
    e_end = jnp.exp(g_end - g)
    a_f = _stack_heads(kap * jnp.exp(g - lw))
    a_s = a_f.astype(BF)
    r_s = _stack_heads(r * jnp.exp(g))
    bb_s = _stack_heads(beta * e_neg).astype(BF)
    bk_s = _stack_heads(k * e_neg).astype(BF)
    v_s = _stack_heads(v).astype(BF)
    kh_s = _stack_heads(k * e_end).astype(BF)
    bh_s = _stack_heads(beta * e_end).astype(BF)
    strict = _tril(cc, strict=True)
    incl = _tril(cc)
    r_sb = r_s.astype(BF)
    l_b = jnp.where(strict, _dot_nt(a_s, bb_s), 0.0)
    l_k = jnp.where(strict, _dot_nt(a_s, bk_s), 0.0).astype(BF)
    w_b = jnp.where(incl, _dot_nt(r_sb, bb_s), 0.0).astype(BF)
    w_k = jnp.where(incl, _dot_nt(r_sb, bk_s), 0.0).astype(BF)
    y = -l_b
    n = y
    for _ in range(n_double):
        yb = y.astype(BF)
        y = _dot(yb, yb)
        n = n + y + _dot(n.astype(BF), y.astype(BF))
    nb = n.astype(BF)
    a_t = a_f + _dot(nb, a_s)
    lkv = _dot(l_k, v_s)
    u0 = lkv + _dot(nb, lkv.astype(BF))
    a_tb = a_t.astype(BF)
    u0b = u0.astype(BF)
    r_hat = r_s - _dot(w_b, a_tb)
    y0 = _dot(w_k, v_s) - _dot(w_b, u0b)
    h_mat = _dot_tn(a_tb, bh_s)
    s_add = _dot_tn(v_s, kh_s) - _dot_tn(u0b, bh_s)
    sb = s_big.astype(BF)
    ys = _dot_nt(r_hat.astype(BF), sb) + y0
    s_new = s_big * jnp.exp(g_end) - _dot(sb, h_mat.astype(BF)) + s_add
    return _unstack_heads(ys), s_new


def _to_block_diag(x):
    return jnp.where(_block_diag_mask(W_BRANCH, HEAD_DIM), jnp.concatenate([x] * N_HEAD, axis=1), 0.0)


def _from_block_diag(x):
    y = x + pltpu.roll(x, 64, 1) + pltpu.roll(x, 128, 1) + pltpu.roll(x, 192, 1)
    return y[:, 0:HEAD_DIM]


def _rwkv_kernel(*refs, seq_len, t_valid, chunk, n_double, has_state):
    if has_state:
        (z_ref, prev_ref, s_in_ref, mu_ref, vec_ref, wa_ref, g2_ref, o_ref, s_out_ref,
         r_s, k_s, v_s, lw_s, kap_s, beta_s, y_s) = refs
    else:
        (z_ref, mu_ref, vec_ref, wa_ref, g2_ref, o_ref, s_out_ref,
         r_s, k_s, v_s, lw_s, kap_s, beta_s, y_s, state_s, last_s) = refs
    rows = z_ref.shape[0]
    i = pl.program_id(0)
    t_row = (i * rows + _iota((rows, 1), 0)) % seq_len
    pr = z_ref[...]
    shifted = pltpu.roll(pr, 1, 0)
    if has_state:
        prev = jnp.where(t_row == 0, prev_ref[...], shifted)
    else:
        first = (i * rows) % seq_len == 0

        @pl.when(first)
        def _():
            last_s[...] = jnp.zeros_like(last_s)
            state_s[...] = jnp.zeros_like(state_s)

        prev = jnp.where(_iota((rows, 1), 0) == 0, last_s[7:8, :], shifted)
        last_s[...] = pr[rows - 8:rows]
    xs = pr + (prev - pr) * mu_ref[...]
    r, k, v = xs[:, 0:256], xs[:, 256:512], xs[:, 512:768]
    lora = xs[:, 768:896]
    lora = jnp.where(_iota((1, 128), 1) < LORA_W, jnp.tanh(lora), lora)
    wa = _dot(lora.astype(BF), wa_ref[...])
    w0, a0, kk_p, ka_p = vec_ref[0:1], vec_ref[1:2], vec_ref[2:3], vec_ref[3:4]
    rk_p, ln_w, ln_b = vec_ref[4:5], vec_ref[5:6], vec_ref[6:7]
    w_log = -_softplus(-(w0 + wa[:, 0:256])) - 0.5
    lw = -jnp.exp(w_log)
    a = _sigmoid(a0 + wa[:, 256:512])
    gate = _dot(_sigmoid(xs[:, 896:1024]).astype(BF), g2_ref[...])
    ones_bd = _block_diag_mask(W_BRANCH, HEAD_DIM).astype(BF)
    kk = k * kk_p
    kap = kk / jnp.maximum(jnp.sqrt(_head_sum(kk * kk, ones_bd)), 1e-12)
    k2 = k * (1.0 + (a - 1.0) * ka_p)
    bonus = _head_sum(r * k2 * rk_p, ones_bd) * v
    live = t_row < t_valid
    r_s[...] = r
    k_s[...] = jnp.where(live, k2, 0.0)
    v_s[...] = jnp.where(live, v, 0.0)
    lw_s[...] = jnp.where(live, lw, 0.0)
    kap_s[...] = jnp.where(live, kap, 0.0)
    beta_s[...] = jnp.where(live, a * kap, 0.0)

    def body(c, carry):
        sl = pl.ds(pl.multiple_of(c * chunk, chunk), chunk)
        if has_state:
            st = pl.ds(pl.multiple_of(c * W_BRANCH, W_BRANCH), W_BRANCH)
            s_big = _to_block_diag(s_in_ref[st, :])
        else:
            s_big = state_s[...]
        y, s_new = _rwkv_chunk(r_s[sl, :], k_s[sl, :], v_s[sl, :], lw_s[sl, :], kap_s[sl, :], beta_s[sl, :], s_big, n_double)
        y_s[sl, :] = y
        if has_state:
            s_out_ref[st, :] = _from_block_diag(s_new)
        else:
            state_s[...] = s_new
        return carry

    lax.fori_loop(0, rows // chunk, body, 0, unroll=2)
    if not has_state:
        s_out_ref[...] = _from_block_diag(state_s[...])
    y = y_s[...]
    mu = _head_sum(y, ones_bd) * (1.0 / HEAD_DIM)
    yc = y - mu
    var = _head_sum(yc * yc, ones_bd) * (1.0 / HEAD_DIM)
    yn = yc * lax.rsqrt(var + RWKV_GN_EPS) * ln_w + ln_b
    o_ref[...] = (yn + bonus) * gate


def _rwkv(z, prev_rows, state_in, mu, vec, wa, g2, b, seq_len, t_valid):
    n = z.shape[0]
    has_state = state_in is not None
    rows = min(ROW_TILE, n)
    chunk = min(RWKV_CHUNK, seq_len)
    n_double = max(0, int(math.log2(chunk)) - 1)
    steps = n // rows
    seq_per_tile = max(1, rows // seq_len)
    tiles_per_seq = max(1, seq_len // rows)
    kernel = functools.partial(_rwkv_kernel, seq_len=seq_len, t_valid=t_valid, chunk=chunk, n_double=n_double,
                               has_state=has_state)
    consts = [_const_spec((1, W_GROUP)), _const_spec((8, W_BRANCH)), _const_spec((128, 512)), _const_spec((128, W_BRANCH))]
    zspec = pl.BlockSpec((rows, W_GROUP), lambda i: (i, 2))
    scratch = [pltpu.VMEM((rows, W_BRANCH), F32)] * 7
    if has_state:
        srows = seq_per_tile * W_BRANCH
        in_specs = [zspec, pl.BlockSpec((rows, W_GROUP), lambda i: (i, 0)), pl.BlockSpec((srows, HEAD_DIM), lambda i: (i, 0))] + consts
        s_spec = pl.BlockSpec((srows, HEAD_DIM), lambda i: (i, 0))
        args = (z, prev_rows, state_in, mu, vec, wa, g2)
    else:
        in_specs = [zspec] + consts
        s_spec = pl.BlockSpec((W_BRANCH, HEAD_DIM), lambda i: (i // tiles_per_seq, 0))
        scratch = scratch + [pltpu.VMEM((W_BRANCH, W_BRANCH), F32), pltpu.VMEM((8, W_GROUP), F32)]
        args = (z, mu, vec, wa, g2)
    return pl.pallas_call(
        kernel, grid=(steps,), in_specs=in_specs,
        out_specs=[pl.BlockSpec((rows, W_BRANCH), lambda i: (i, 0)), s_spec],
        out_shape=[jax.ShapeDtypeStruct((n, W_BRANCH), F32), jax.ShapeDtypeStruct((b * W_BRANCH, HEAD_DIM), F32)],
        scratch_shapes=scratch, compiler_params=_params(), name="rwkv")(*args)


def _hgrn_lb_kernel(x_ref, o_ref):
    x = x_ref[...]
    depth = x.shape[0]
    e = jnp.exp(x - jnp.max(x, axis=0, keepdims=True))
    soft = e / jnp.sum(e, axis=0, keepdims=True)
    cum = jnp.zeros((1, x.shape[1]), F32)
    for l in range(depth):
        cum = cum + soft[l:l + 1]
        lb = jnp.maximum(cum - soft[0:1], 0.0)
        o_ref[l, 0:1, :] = lb
        o_ref[l, 1:2, :] = jnp.log(jnp.maximum(lb, LB_TINY))
        o_ref[l, 2:3, :] = jnp.log1p(-lb)
        o_ref[l, 3:8, :] = jnp.zeros((5, x.shape[1]), F32)


def _hgrn_lb(hgrn_lb):
    depth, w = hgrn_lb.shape
    return pl.pallas_call(_hgrn_lb_kernel, out_shape=jax.ShapeDtypeStruct((depth, 8, w), F32), name="hgrn_lb")(hgrn_lb)


def _hgrn_block(q, k, v, g, st, ones_bd, bd_mask):
    c = q.shape[0]
    b = _dot_sel_l(_tril(c).astype(BF), g)
    b_end = b[c - 1:c]
    o = _dot_nt((q * jnp.exp(b)).astype(BF), st.astype(BF))
    t_idx = _iota((c, 1), 0)
    xs = []
    for s in range(c):
        e = jnp.exp(jnp.where(t_idx >= s, b - b[s:s + 1], NEG))
        xs.append((q * e * k[s:s + 1]).astype(BF))
    col = _dot(jnp.concatenate(xs, axis=0), ones_bd)
    for s in range(c):
        o = o + col[s * c:(s + 1) * c] * v[s:s + 1]
    kh = (k * jnp.exp(b_end - b)).astype(BF)
    st_new = st * jnp.exp(b_end) + jnp.where(bd_mask, _dot_tn(v.astype(BF), kh), 0.0)
    return o, st_new


def _hgrn_kernel(*refs, seq_len, t_valid, block, has_state):
    if has_state:
        z_ref, s_in_ref, lb_ref, nw_ref, o_ref, s_out_ref, q_s, k_s, v_s, g_s, y_s = refs
    else:
        z_ref, lb_ref, nw_ref, o_ref, s_out_ref, q_s, k_s, v_s, g_s, y_s, state_s = refs
    rows = z_ref.shape[0]
    i = pl.program_id(0)
    t_row = (i * rows + _iota((rows, 1), 0)) % seq_len
    live = t_row < t_valid
    hq, hf, hi, hg = z_ref[:, 0:256], z_ref[:, 256:512], z_ref[:, 512:768], z_ref[:, 768:1024]
    lb, lb_log, l1m = lb_ref[0:1], lb_ref[1:2], lb_ref[2:3]
    ls = _log_sigmoid(hf)
    x2 = l1m + ls
    lae = jnp.maximum(lb_log, x2) + jnp.log1p(jnp.exp(-jnp.abs(lb_log - x2)))
    logf = jnp.where(lb > 0.0, lae, ls)
    q_s[...] = hq * _sigmoid(hq)
    k_s[...] = jnp.where(live, (1.0 - lb) * _sigmoid(-hf), 0.0)
    v_s[...] = hi
    g_s[...] = jnp.where(live, logf, 0.0)
    ones_bd = _block_diag_mask(W_BRANCH, HEAD_DIM).astype(BF)
    bd_mask = _block_diag_mask(W_BRANCH, HEAD_DIM)

    if not has_state:
        @pl.when((i * rows) % seq_len == 0)
        def _():
            state_s[...] = jnp.zeros_like(state_s)

    def body(c, carry):
        sl = pl.ds(pl.multiple_of(c * block, block), block)
        if has_state:
            sr = pl.ds(pl.multiple_of(c * W_BRANCH, W_BRANCH), W_BRANCH)
            st = _to_block_diag(s_in_ref[sr, :]).T
        else:
            st = state_s[...]
        o, st_new = _hgrn_block(q_s[sl, :], k_s[sl, :], v_s[sl, :], g_s[sl, :], st, ones_bd, bd_mask)
        y_s[sl, :] = o
        if has_state:
            s_out_ref[sr, :] = _from_block_diag(st_new.T)
        else:
            state_s[...] = st_new
        return carry

    lax.fori_loop(0, rows // block, body, 0, unroll=2)
    if not has_state:
        s_out_ref[...] = _from_block_diag(state_s[...].T)
    o = y_s[...]
    ms = _head_sum(o * o, ones_bd) * (1.0 / HEAD_DIM)
    o_ref[...] = o * lax.rsqrt(ms + RMS_EPS) * nw_ref[...] * (hg * _sigmoid(hg))


def _hgrn(z, state_in, lb_rows, norm_w, b, seq_len, t_valid):
    n = z.shape[0]
    has_state = state_in is not None
    rows = min(ROW_TILE, n)
    block = min(HGRN_BLOCK, seq_len)
    seq_per_tile = max(1, rows // seq_len)
    tiles_per_seq = max(1, seq_len // rows)
    kernel = functools.partial(_hgrn_kernel, seq_len=seq_len, t_valid=t_valid, block=block, has_state=has_state)
    zspec = pl.BlockSpec((rows, W_GROUP), lambda i: (i, 3))
    consts = [_const_spec((8, W_BRANCH)), _const_spec((1, W_BRANCH))]
    scratch = [pltpu.VMEM((rows, W_BRANCH), F32)] * 5
    if has_state:
        srows = seq_per_tile * W_BRANCH
        in_specs = [zspec, pl.BlockSpec((srows, HEAD_DIM), lambda i: (i, 0))] + consts
        s_spec = pl.BlockSpec((srows, HEAD_DIM), lambda i: (i, 0))
        args = (z, state_in, lb_rows, norm_w)
    else:
        in_specs = [zspec] + consts
        s_spec = pl.BlockSpec((W_BRANCH, HEAD_DIM), lambda i: (i // tiles_per_seq, 0))
        scratch = scratch + [pltpu.VMEM((W_BRANCH, W_BRANCH), F32)]
        args = (z, lb_rows, norm_w)
    return pl.pallas_call(
        kernel, grid=(n // rows,), in_specs=in_specs,
        out_specs=[pl.BlockSpec((rows, W_BRANCH), lambda i: (i, 0)), s_spec],
        out_shape=[jax.ShapeDtypeStruct((n, W_BRANCH), F32), jax.ShapeDtypeStruct((b * W_BRANCH, HEAD_DIM), F32)],
        scratch_shapes=scratch, compiler_params=_params(), name="hgrn")(*args)


def _merge_kernel(x_ref, oa_ref, ob_ref, oc_ref, od_ref, wg_ref, wb_ref, wo_ref, ln_ref, h_ref, *, alpha):
    x = x_ref[...]
    xb = x.astype(BF)
    m = jnp.zeros(x.shape, F32)
    for n, o_ref in enumerate((oa_ref, ob_ref, oc_ref, od_ref)):
        gate = _sigmoid(_dot(xb, wg_ref[:, n * D_MODEL:(n + 1) * D_MODEL]))
        m = m + gate * _dot(o_ref[...].astype(BF), wb_ref[n])
    mix = _dot(m.astype(BF), wo_ref[...])
    h_ref[...] = _layer_norm(alpha * x + mix, ln_ref[0:1], ln_ref[1:2])


def _merge(x, oa, ob, oc, od, wg, wb, wo, ln, alpha):
    n = x.shape[0]
    tm = min(ROW_TILE, n)
    row = lambda w: pl.BlockSpec((tm, w), lambda i: (i, 0))
    return pl.pallas_call(
        functools.partial(_merge_kernel, alpha=alpha), grid=(n // tm,),
        in_specs=[row(D_MODEL), row(256), row(256), row(256), row(256), _const_spec(wg.shape), _const_spec(wb.shape),
                  _const_spec(wo.shape), _const_spec((8, D_MODEL))],
        out_specs=row(D_MODEL), out_shape=jax.ShapeDtypeStruct((n, D_MODEL), F32),
        compiler_params=_params(), name="merge")(x, oa, ob, oc, od, wg, wb, wo, ln)


def _ffn_kernel(*refs, seq_len, has_state, n_split, alpha):
    if has_state:
        h_ref, p1_ref, p2_ref, wup_ref, wdn_ref, cv_ref, ln_ref, y_ref, a_ref = refs
    else:
        h_ref, wup_ref, wdn_ref, cv_ref, ln_ref, y_ref, a_ref, last_s = refs
    rows = h_ref.shape[0]
    i = pl.program_id(0)
    h = h_ref[...]
    hb = h.astype(BF)
    ridx = _iota((rows, 1), 0)
    t_row = (i * rows + ridx) % seq_len
    wf = D_FF // n_split
    f = jnp.zeros((rows, D_MODEL), F32)
    if not has_state:
        @pl.when((i * rows) % seq_len == 0)
        def _():
            last_s[...] = jnp.zeros_like(last_s)
    for j in range(n_split):
        lo, hi = j * wf, (j + 1) * wf
        a = _dot(hb, wup_ref[:, lo:hi])
        gt = _dot(hb, wup_ref[:, D_FF + lo:D_FF + hi])
        r1 = pltpu.roll(a, 1, 0)
        r2 = pltpu.roll(a, 2, 0)
        if has_state:
            prev1 = jnp.where(t_row == 0, p1_ref[:, lo:hi], r1)
            prev2 = jnp.where(t_row < 2, p2_ref[:, lo:hi], r2)
            a_ref[:, lo:hi] = a
        else:
            c6 = last_s[6:7, lo:hi]
            c7 = last_s[7:8, lo:hi]
            prev1 = jnp.where(ridx == 0, c7, r1)
            prev2 = jnp.where(ridx == 0, c6, jnp.where(ridx == 1, c7, r2))
            last_s[:, lo:hi] = a[rows - 8:rows]
            a_ref[:, lo:hi] = a[rows - 8:rows]
        conv = cv_ref[3:4, lo:hi] + prev2 * cv_ref[0:1, lo:hi] + prev1 * cv_ref[1:2, lo:hi] + a * cv_ref[2:3, lo:hi]
        hid = _gelu(conv) * gt
        f = f + _dot(hid.astype(BF), wdn_ref[lo:hi, :])
    y_ref[...] = _layer_norm(alpha * h + f, ln_ref[0:1], ln_ref[1:2])


def _ffn(h, p1, p2, wup, wdn, cv, ln, seq_len, alpha):
    n = h.shape[0]
    has_state = p1 is not None
    tm = min(ROW_TILE, n)
    row = lambda w: pl.BlockSpec((tm, w), lambda i: (i, 0))
    kernel = functools.partial(_ffn_kernel, seq_len=seq_len, has_state=has_state, n_split=2, alpha=alpha)
    consts = [_const_spec(wup.shape), _const_spec(wdn.shape), _const_spec((8, D_FF)), _const_spec((8, D_MODEL))]
    if has_state:
        in_specs = [row(D_MODEL), row(D_FF), row(D_FF)] + consts
        a_spec, a_rows, scratch = row(D_FF), n, []
        args = (h, p1, p2, wup, wdn, cv, ln)
    else:
        in_specs = [row(D_MODEL)] + consts
        a_spec, a_rows = pl.BlockSpec((8, D_FF), lambda i: (i, 0)), (n // tm) * 8
        scratch = [pltpu.VMEM((8, D_FF), F32)]
        args = (h, wup, wdn, cv, ln)
    return pl.pallas_call(
        kernel, grid=(n // tm,), in_specs=in_specs, out_specs=[row(D_MODEL), a_spec],
        out_shape=[jax.ShapeDtypeStruct((n, D_MODEL), F32), jax.ShapeDtypeStruct((a_rows, D_FF), F32)],
        scratch_shapes=scratch, compiler_params=_params(), name="ffn")(*args)


def _pad_lanes(x, width):
    return jnp.pad(x, [(0, 0)] * (x.ndim - 1) + [(0, width - x.shape[-1])])


def _regroup_w_in(w_in):
    o = 0
    cols = {}
    for name, w in (('fox_q', 256), ('fox_k', 256), ('fox_v', 256), ('fox_f', 4), ('dsa_q', 256), ('dsa_k', 256),
                    ('dsa_v', 256), ('idx_q', 128), ('idx_k', 32), ('idx_w', 4), ('rwkv', 1024), ('hgrn', 1024),
                    ('gate', 4096)):
        cols[name] = w_in[..., o:o + w]
        o += w
    g0 = _pad_lanes(jnp.concatenate([cols['fox_k'], cols['fox_v'], cols['fox_q'], cols['fox_f']], -1), W_GROUP)
    g1 = _pad_lanes(jnp.concatenate([cols['dsa_k'], cols['dsa_v'], cols['dsa_q'], cols['idx_q'], cols['idx_k'],
                                     cols['idx_w']], -1), W_GROUP)
    wz = jnp.concatenate([g0, g1, cols['rwkv'], cols['hgrn']], -1).astype(BF)
    return wz, cols['gate'].astype(BF)


def _rows8(*vecs, width):
    rows = [v.reshape(1, width) for v in vecs]
    rows.append(jnp.zeros((8 - len(rows), width), F32))
    return jnp.concatenate(rows, axis=0)


def _expand_first_rows(state, tt, offsets):
    b, _, w = state.shape
    out = jnp.zeros((b, tt, w), state.dtype)
    for s, ts in offsets:
        out = out.at[:, ts].set(state[:, s])
    return out.reshape(b * tt, w)


def _layer(x, cfg, lw):
    b, tt, tv, past = cfg['b'], cfg['tt'], cfg['tv'], cfg['past']
    decode = past > 0
    z = _inproj(x, lw['wz'])
    topk = max(1, min(DSA_TOPK, (past + tv) // 4))
    new = {}
    if decode:
        o_a, logf = _fox_decode(z, lw['fox_bf'], cfg['fox_kv'], cfg['fox_lf'], cfg['page_table'], cfg['layer'], b, tt)
    else:
        qt_a, k_a, vt_a, logf, ccol, crow = _fox_prep(z, lw['fox_bf'], b, tt)
        o_a = _fox_attn(qt_a, k_a, vt_a, ccol, crow, b, tt)
    new['fox_kv'] = z[:, 0:512]
    new['fox_logf'] = logf[:, 0:N_HEAD]
    if decode:
        kv_b, misc_b, q_b, qi_b = _dsa_prep(z, cfg['tab_k'], cfg['tab_i'], False)
        keep = _dsa_select(qi_b, misc_b, cfg['dsa_ki'], cfg['page_table'], cfg['layer'], b, tt, topk, tv)
        o_b = _dsa_decode(q_b, keep, kv_b, cfg['dsa_kv'], cfg['page_table'], cfg['layer'], b, tt)
    else:
        kv_b, misc_b, qt_b, k_b, vt_b, qit_b, wt_b, ki4_b = _dsa_prep(z, cfg['tab_k'], cfg['tab_i'], True)
        o_b = _dsa_attn(qt_b, k_b, vt_b, qit_b, wt_b, ki4_b, b, tt, topk)
    new['dsa_kv'] = kv_b
    new['dsa_kidx'] = misc_b[:, 0:D_IDX]
    o_c, new['rwkv'] = _rwkv(z, cfg.get('shift_rows'), cfg.get('rwkv_state'), lw['rwkv_mu'], lw['rwkv_vec'], lw['rwkv_wa'],
                             lw['rwkv_g2'], b, tt, tv)
    new['shift'] = z[:, 2 * W_GROUP:3 * W_GROUP]
    o_d, new['hgrn'] = _hgrn(z, cfg.get('hgrn_state'), lw['hgrn_lb'], lw['hgrn_nw'], b, tt, tv)
    h = _merge(x, o_a, o_b, o_c, o_d, lw['wg'], lw['wb'], lw['wo'], lw['ln1'], cfg['alpha'])
    y, new['conv'] = _ffn(h, cfg.get('conv_p1'), cfg.get('conv_p2'), lw['wup'], lw['wdn'], lw['conv'], lw['ln2'], tt, cfg['alpha'])
    return y, new


def kernel(x_prompt, x_sample, cache_fox_kv, cache_fox_logf, cache_dsa_kv, cache_dsa_kidx, state_rwkv, state_rwkv_shift, state_hgrn, state_ffn_conv, page_table, w_in, fox_bf, rwkv_mu, rwkv_w0, rwkv_w2, rwkv_a0, rwkv_a2, rwkv_g2, rwkv_kk, rwkv_ka, rwkv_rk, rwkv_ln_w, rwkv_ln_b, hgrn_lb, hgrn_norm_w, w_branch, w_o, ln1_g, ln1_b, ln2_g, ln2_b, ffn_w_in, ffn_conv_w, ffn_conv_b, ffn_w_out):
    depth = w_in.shape[0]
    bp, tp, d = x_prompt.shape
    bs, ts, _ = x_sample.shape
    n_pool, page = cache_fox_kv.shape[1], cache_fox_kv.shape[2]
    n_pages = page_table.shape[1]
    past = n_pages * page
    tsp = -(-ts // T_ALIGN) * T_ALIGN
    assert d == D_MODEL and tp % min(ROW_TILE, tp) == 0 and (bs * tsp) % min(ROW_TILE, bs * tsp) == 0
    assert tsp <= 128 and ts >= 2

    wz_all, wg_all = _regroup_w_in(w_in)
    wb_all, wo_all = w_branch.astype(BF), w_o.astype(BF)
    wup_all, wdn_all = ffn_w_in.astype(BF), ffn_w_out.astype(BF)
    zero_l = jnp.zeros((depth, LORA_W, W_BRANCH), F32)
    wa_all = jnp.concatenate([jnp.concatenate([rwkv_w2, zero_l], 2), jnp.concatenate([zero_l, rwkv_a2], 2)], 1).astype(BF)
    g2_all = rwkv_g2.astype(BF)
    lb_all = _hgrn_lb(hgrn_lb)
    page_table = page_table.astype(I32)

    fox_kv_pages = jnp.transpose(cache_fox_kv, (0, 1, 3, 4, 5, 2)).reshape(depth, n_pool, 2 * W_BRANCH, page)
    dsa_kv_pages = jnp.transpose(cache_dsa_kv, (0, 1, 3, 4, 5, 2)).reshape(depth, n_pool, 2 * W_BRANCH, page)
    dsa_ki_pages = jnp.swapaxes(cache_dsa_kidx, 2, 3)
    fox_lf_pages = _pad_rows_nd(jnp.swapaxes(cache_fox_logf, 2, 3), 8)

    pos_p = jnp.arange(tp)
    pos_s = past + (jnp.arange(bs * tsp) % tsp)[:min(ROW_TILE, bs * tsp)]
    alpha = (2 * depth) ** 0.25
    cfg_p = dict(b=bp, tt=tp, tv=tp, past=0, alpha=alpha,
                 tab_k=_rope_tables(pos_p, 256, HEAD_DIM, ROPE_HALF_QK), tab_i=_rope_tables(pos_p, 128, D_IDX, ROPE_HALF_IDX))
    cfg_s = dict(b=bs, tt=tsp, tv=ts, past=past, alpha=alpha, page_table=page_table, fox_kv=fox_kv_pages, fox_lf=fox_lf_pages,
                 dsa_kv=dsa_kv_pages, dsa_ki=dsa_ki_pages,
                 tab_k=_rope_tables(pos_s, 256, HEAD_DIM, ROPE_HALF_QK), tab_i=_rope_tables(pos_s, 128, D_IDX, ROPE_HALF_IDX))

    xp = x_prompt.reshape(bp * tp, d)
    xs = jnp.pad(x_sample, ((0, 0), (0, tsp - ts), (0, 0))).reshape(bs * tsp, d)
    new_p, new_s = [], []
    for l in range(depth):
        lw = dict(wz=wz_all[l], wg=wg_all[l], wb=wb_all[l], wo=wo_all[l], wup=wup_all[l], wdn=wdn_all[l],
                  fox_bf=_pad_lanes(fox_bf[l][None], 128), rwkv_mu=rwkv_mu[l][None],
                  rwkv_vec=_rows8(rwkv_w0[l], rwkv_a0[l], rwkv_kk[l], rwkv_ka[l], rwkv_rk[l], rwkv_ln_w[l], rwkv_ln_b[l],
                                  width=W_BRANCH),
                  rwkv_wa=wa_all[l], rwkv_g2=g2_all[l], hgrn_lb=lb_all[l], hgrn_nw=hgrn_norm_w[l][None],
                  ln1=_rows8(ln1_g[l], ln1_b[l], width=D_MODEL), ln2=_rows8(ln2_g[l], ln2_b[l], width=D_MODEL),
                  conv=_rows8(ffn_conv_w[l, 0], ffn_conv_w[l, 1], ffn_conv_w[l, 2], ffn_conv_b[l], width=D_FF))
        xp, st_p = _layer(xp, cfg_p, lw)
        cfg_l = dict(cfg_s, layer=l,
                     shift_rows=_expand_first_rows(state_rwkv_shift[l][:, None], tsp, ((0, 0),)),
                     rwkv_state=state_rwkv[l].reshape(bs * W_BRANCH, HEAD_DIM),
                     hgrn_state=state_hgrn[l].reshape(bs * W_BRANCH, HEAD_DIM),
                     conv_p1=_expand_first_rows(state_ffn_conv[l], tsp, ((1, 0),)),
                     conv_p2=_expand_first_rows(state_ffn_conv[l], tsp, ((0, 0), (1, 1))))
        xs, st_s = _layer(xs, cfg_l, lw)
        new_p.append(st_p)
        new_s.append(st_s)

    def assemble(new, b, tt, tv, decode):
        def rows(name, shape):
            a = jnp.stack([n[name] for n in new]).reshape(depth, b, tt, -1)[:, :, :tv]
            return a.reshape((depth, b, tv) + shape)

        fox_kv = rows('fox_kv', (2, N_HEAD, HEAD_DIM))
        fox_logf = rows('fox_logf', (N_HEAD,))
        dsa_kv = rows('dsa_kv', (2, N_HEAD, HEAD_DIM))
        dsa_kidx = rows('dsa_kidx', (D_IDX,))
        rwkv = jnp.stack([n['rwkv'] for n in new]).reshape(depth, b, N_HEAD, HEAD_DIM, HEAD_DIM)
        hgrn = jnp.stack([n['hgrn'] for n in new]).reshape(depth, b, N_HEAD, HEAD_DIM, HEAD_DIM)
        shift = jnp.stack([n['shift'] for n in new]).reshape(depth, b, tt, W_GROUP)[:, :, tv - 1]
        conv = jnp.stack([n['conv'] for n in new])
        if decode:
            conv = conv.reshape(depth, b, tt, D_FF)[:, :, tv - 2:tv]
        else:
            conv = conv.reshape(depth, b, -1, 8, D_FF)[:, :, -1, 6:8]
        return fox_kv, fox_logf, dsa_kv, dsa_kidx, rwkv, shift, hgrn, conv

    y_p = xp.reshape(bp, tp, d)
    y_s = xs.reshape(bs, tsp, d)[:, :ts]
    return (y_p, y_s) + assemble(new_p, bp, tp, tp, False) + assemble(new_s, bs, tsp, ts, True)


def _pad_rows_nd(x, rows):
    pad = [(0, 0)] * x.ndim
    pad[-2] = (0, rows - x.shape[-2])
    return jnp.pad(x, pad)
```

```python
import functools
import math

import jax
import jax.numpy as jnp
import numpy as np
from jax import lax
from jax.experimental import pallas as pl
from jax.experimental.pallas import tpu as pltpu

F32 = jnp.float32
BF = jnp.bfloat16
I32 = jnp.int32
I16 = jnp.int16

D_MODEL = 1024
N_HEAD = 4
HEAD_DIM = 64
W_BRANCH = N_HEAD * HEAD_DIM
D_IDX = 32
W_IDX = N_HEAD * D_IDX
LORA_W = 64
LORA_A = 64
LORA_G = 128
D_FF = 2816
N_GROUP = 4
W_GROUP = 1024
DSA_TOPK = 256
ROPE_THETA = 500000.0
ROPE_HALF_QK = 8
ROPE_HALF_IDX = 4
RWKV_GN_EPS = 64e-5
RMS_EPS = 1e-6
LB_TINY = 1e-30
LN_EPS = 1e-5
NEG = -1e30
ROW_TILE = 256
KEY_CHUNK = 512
SELECT_SEQS = 8
RWKV_CHUNK = 64
RWKV_LOCKSTEP = 4
HGRN_BLOCK = 16
HGRN_LOCKSTEP = 4
T_ALIGN = 8
INT_MIN = -2 ** 31


def _dot(a, b):
    return jnp.dot(a, b, preferred_element_type=F32)


def _dot_nt(a, b):
    return lax.dot_general(a, b, (((1,), (1,)), ((), ())), preferred_element_type=F32)


def _dot_tn(a, b):
    return lax.dot_general(a, b, (((0,), (0,)), ((), ())), preferred_element_type=F32)


def _split3(x):
    hi = x.astype(BF)
    r = x - hi.astype(F32)
    mid = r.astype(BF)
    lo = (r - mid.astype(F32)).astype(BF)
    return hi, mid, lo


def _dot_sel_l(sel_bf, x):
    hi, mid, lo = _split3(x)
    return _dot(sel_bf, hi) + _dot(sel_bf, mid) + _dot(sel_bf, lo)


def _dot_sel_r(x, sel_bf):
    hi, mid, lo = _split3(x)
    return _dot(hi, sel_bf) + _dot(mid, sel_bf) + _dot(lo, sel_bf)


def _dot_sel_tn(x, sel_bf):
    hi, mid, lo = _split3(x)
    return _dot_tn(hi, sel_bf) + _dot_tn(mid, sel_bf) + _dot_tn(lo, sel_bf)


def _iota(shape, dim):
    return lax.broadcasted_iota(I32, shape, dim)


def _tril(n, strict=False):
    r, c = _iota((n, n), 0), _iota((n, n), 1)
    return (r > c) if strict else (r >= c)


def _head_lane(width, head_width):
    return _iota((1, width), 1) // head_width


def _block_diag_mask(n, block):
    return (_iota((n, n), 0) // block) == (_iota((n, n), 1) // block)


def _stack_heads(x, head_width=HEAD_DIM):
    hl = _head_lane(x.shape[1], head_width)
    return jnp.concatenate([jnp.where(hl == h, x, jnp.zeros_like(x)) for h in range(N_HEAD)], axis=0)


def _unstack_heads(xs):
    c = xs.shape[0] // N_HEAD
    out = xs[0:c]
    for h in range(1, N_HEAD):
        out = out + xs[h * c:(h + 1) * c]
    return out


def _head_sum(x, ones_bd):
    return _dot_sel_r(x, ones_bd)


def _sigmoid(x):
    return 1.0 / (1.0 + jnp.exp(-x))


def _log_sigmoid(x):
    return jnp.minimum(x, 0.0) - jnp.log1p(jnp.exp(-jnp.abs(x)))


def _softplus(x):
    return jnp.maximum(x, 0.0) + jnp.log1p(jnp.exp(-jnp.abs(x)))


def _gelu(x):
    return 0.5 * x * (1.0 + lax.erf(x * (2.0 ** -0.5)))


def _layer_norm(x, g, b):
    mu = jnp.mean(x, axis=-1, keepdims=True)
    xc = x - mu
    var = jnp.mean(xc * xc, axis=-1, keepdims=True)
    return xc * lax.rsqrt(var + LN_EPS) * g + b


def _params(n_axes=1):
    return pltpu.CompilerParams(dimension_semantics=("arbitrary",) * n_axes)


def _const_spec(shape):
    nd = len(shape)
    return pl.BlockSpec(shape, lambda *_: (0,) * nd)


def _inproj_kernel(x_ref, w_ref, z_ref):
    z_ref[...] = _dot(x_ref[...].astype(BF), w_ref[...])


def _inproj(x, wz):
    n = x.shape[0]
    tm = min(ROW_TILE, n)
    nz = wz.shape[1]
    return pl.pallas_call(
        _inproj_kernel, grid=(n // tm,),
        in_specs=[pl.BlockSpec((tm, D_MODEL), lambda i: (i, 0)), _const_spec((D_MODEL, nz))],
        out_specs=pl.BlockSpec((tm, nz), lambda i: (i, 0)),
        out_shape=jax.ShapeDtypeStruct((n, nz), F32), compiler_params=_params(), name="inproj")(x, wz)


def _fox_prep_kernel(z_ref, bf_ref, qt_ref, k_ref, vt_ref, logf_ref, ccol_ref, crow_ref):
    t = z_ref.shape[0]
    ck = vt_ref.shape[2]
    tq = crow_ref.shape[2]
    k_ref[...] = z_ref[:, 0:256].astype(BF)
    qt_ref[...] = (z_ref[:, 512:768] * (HEAD_DIM ** -0.5)).T.astype(BF)
    logf = _log_sigmoid(z_ref[:, 768:896] + bf_ref[...])
    logf_ref[...] = logf
    tri = _tril(ck).astype(BF)
    carry = jnp.zeros((1, 128), F32)
    for c in range(t // ck):
        rows = slice(c * ck, (c + 1) * ck)
        vt_ref[c] = z_ref[rows, 256:512].T.astype(BF)
        cum = _dot_sel_l(tri, logf[rows]) + carry
        carry = cum[ck - 1:ck]
        ccol_ref[rows, :] = cum
        cum_t = cum.T[0:8]
        for j in range(ck // tq):
            crow_ref[c * (ck // tq) + j] = cum_t[:, j * tq:(j + 1) * tq]


def _fox_prep(z, bf_row, b, t):
    ck = min(KEY_CHUNK, t)
    tq = min(ROW_TILE, t)
    nc, nq = t // ck, t // tq
    rows = lambda w: pl.BlockSpec((t, w), lambda i: (i, 0))
    return pl.pallas_call(
        _fox_prep_kernel, grid=(b,),
        in_specs=[rows(W_GROUP), _const_spec((1, 128))],
        out_specs=[pl.BlockSpec((W_BRANCH, t), lambda i: (0, i)), rows(W_BRANCH),
                   pl.BlockSpec((nc, W_BRANCH, ck), lambda i: (i, 0, 0)), rows(128), rows(128),
                   pl.BlockSpec((None, nq, 8, tq), lambda i: (i, 0, 0, 0))],
        out_shape=[jax.ShapeDtypeStruct((W_BRANCH, b * t), BF), jax.ShapeDtypeStruct((b * t, W_BRANCH), BF),
                   jax.ShapeDtypeStruct((b * nc, W_BRANCH, ck), BF),
                   jax.ShapeDtypeStruct((b * t, 128), F32), jax.ShapeDtypeStruct((b * t, 128), F32),
                   jax.ShapeDtypeStruct((b, nq, 8, tq), F32)],
        compiler_params=_params(), name="fox_prep")(z, bf_row)


def _flash_init(m_s, l_s, acc_s):
    m_s[...] = jnp.full(m_s.shape, NEG, F32)
    l_s[...] = jnp.zeros(l_s.shape, F32)
    acc_s[...] = jnp.zeros(acc_s.shape, F32)


def _flash_update(sts, vts, m_s, l_s, acc_s):
    scaled = []
    for h, st in enumerate(sts):
        m_old = m_s[h]
        m_new = jnp.maximum(m_old, jnp.max(st, axis=0, keepdims=True))
        a = jnp.exp(m_old - m_new)
        p = jnp.exp(st - m_new)
        l_s[h] = a * l_s[h] + jnp.sum(p, axis=0, keepdims=True)
        m_s[h] = m_new
        scaled.append((a, p.astype(BF)))
    for h, (a, p) in enumerate(scaled):
        acc_s[h] = a * acc_s[h] + _dot(vts[h], p)


def _flash_result(l_s, acc_s):
    return jnp.concatenate([acc_s[h] / l_s[h] for h in range(N_HEAD)], axis=0).T


def _flash_scratch(tq):
    return [pltpu.VMEM((N_HEAD, 1, tq), F32), pltpu.VMEM((N_HEAD, 1, tq), F32), pltpu.VMEM((N_HEAD, HEAD_DIM, tq), F32)]


def _head_rows(x, head_rows):
    rh = _iota((x.shape[0], 1), 0) // head_rows
    return [jnp.where(rh == h, x, jnp.zeros_like(x)) for h in range(N_HEAD)]


def _causal_chunks(i, tq, ck):
    return (i * tq) // ck + 1


def _fox_attn_kernel(qt_ref, k_ref, vt_ref, ccol_ref, crow_ref, o_ref, m_s, l_s, acc_s):
    tq = qt_ref.shape[1]
    ck = vt_ref.shape[2]
    i = pl.program_id(1)
    qpos = i * tq + _iota((1, tq), 1)
    qth = _head_rows(qt_ref[...], HEAD_DIM)
    cq = [crow_ref[h:h + 1, :] for h in range(N_HEAD)]
    _flash_init(m_s, l_s, acc_s)

    def chunk(c, carry):
        rows = pl.ds(pl.multiple_of(c * ck, ck), ck)
        k = k_ref[rows, :]
        ccol = ccol_ref[rows, :]
        visible = (c * ck + _iota((ck, 1), 0)) <= qpos
        sts = [jnp.where(visible, _dot(k, qth[h]) + cq[h] - ccol[:, h:h + 1], NEG) for h in range(N_HEAD)]
        vts = [vt_ref[c, h * HEAD_DIM:(h + 1) * HEAD_DIM, :] for h in range(N_HEAD)]
        _flash_update(sts, vts, m_s, l_s, acc_s)
        return carry

    lax.fori_loop(0, _causal_chunks(i, tq, ck), chunk, 0)
    o_ref[...] = _flash_result(l_s, acc_s)


def _fox_attn(qt, k, vt, ccol, crow, b, t):
    tq = crow.shape[3]
    nq = t // tq
    ck = vt.shape[2]
    nc = t // ck
    return pl.pallas_call(
        _fox_attn_kernel, grid=(b, nq),
        in_specs=[pl.BlockSpec((W_BRANCH, tq), lambda bi, i: (0, bi * nq + i)),
                  pl.BlockSpec((t, W_BRANCH), lambda bi, i: (bi, 0)),
                  pl.BlockSpec((nc, W_BRANCH, ck), lambda bi, i: (bi, 0, 0)),
                  pl.BlockSpec((t, 128), lambda bi, i: (bi, 0)),
                  pl.BlockSpec((None, None, 8, tq), lambda bi, i: (bi, i, 0, 0))],
        out_specs=pl.BlockSpec((tq, W_BRANCH), lambda bi, i: (bi * nq + i, 0)),
        out_shape=jax.ShapeDtypeStruct((b * t, W_BRANCH), F32), scratch_shapes=_flash_scratch(tq),
        compiler_params=_params(2), name="fox_attn")(qt, k, vt, ccol, crow)


def _page_specs(n_pages, block, layer):
    nd = len(block)

    def mk(p):
        return pl.BlockSpec((None, None) + block, lambda b, pt: (layer, pt[b, p]) + (0,) * nd)

    return [mk(p) for p in range(n_pages)]


def _softmax_pv(s, vt_past, v_new, past):
    m = jnp.max(s, axis=1, keepdims=True)
    p = jnp.exp(s - m)
    l = jnp.sum(p, axis=1, keepdims=True)
    pb = p.astype(BF)
    o = _dot_nt(pb[:, :past], vt_past) + _dot(pb[:, past:], v_new)
    o = _stack_mask(o / l)
    return _unstack_heads(o)


def _cat_pages(refs, lo, hi):
    return jnp.concatenate([r[lo:hi, :] for r in refs], axis=1)


def _stack_mask(o):
    c = o.shape[0] // N_HEAD
    hl = _head_lane(o.shape[1], HEAD_DIM)
    row_h = _iota((o.shape[0], 1), 0) // c
    return jnp.where(row_h == hl, o, 0.0)


def _pad_rows(x, rows):
    return jnp.concatenate([x, jnp.zeros((rows - x.shape[0], x.shape[1]), x.dtype)], axis=0)


def _fox_decode_kernel(pt_ref, z_ref, bf_ref, *refs, n_pages):
    kv_refs = refs[:n_pages]
    lf_refs = refs[n_pages:2 * n_pages]
    o_ref, logf_ref = refs[2 * n_pages:]
    tt = z_ref.shape[0]
    page = kv_refs[0].shape[1]
    past = n_pages * page
    k_new = z_ref[:, 0:256]
    v_new = z_ref[:, 256:512]
    q = z_ref[:, 512:768] * (HEAD_DIM ** -0.5)
    logf = _log_sigmoid(z_ref[:, 768:896] + bf_ref[...])
    logf_ref[...] = logf

    lf = jnp.concatenate([r[...] for r in lf_refs], axis=0)
    n = lf.shape[0]
    tri_u = (_iota((page, page), 0) <= _iota((page, page), 1)).astype(BF)
    in_page = _dot_sel_r(lf, tri_u)
    tot = jnp.broadcast_to(in_page[:, page - 1:page], (n, page))
    r, c = _iota((n, n), 0), _iota((n, n), 1)
    later = jnp.where(((r % 8) == (c % 8)) & ((c // 8) >= (r // 8)), -1.0, 0.0).astype(BF)
    ck_rel = in_page + _dot_sel_l(later, tot)

    cn_col = _dot_sel_l(_tril(tt).astype(BF), logf)
    tri_pad = (_iota((tt, 128), 0) <= _iota((tt, 128), 1)).astype(BF)
    cn_row = _dot_sel_tn(logf, tri_pad)

    qbd = _stack_heads(q).astype(BF)
    s_past = _dot(qbd, _cat_pages(kv_refs, 0, W_BRANCH).astype(BF))
    vt_past = _cat_pages(kv_refs, W_BRANCH, 2 * W_BRANCH).astype(BF)
    s_new = _dot_nt(qbd, _pad_rows(k_new, 128).astype(BF))

    cq = jnp.concatenate([cn_col[:, h:h + 1] for h in range(N_HEAD)], axis=0)
    bias_past = jnp.concatenate(
        [jnp.concatenate([jnp.broadcast_to(ck_rel[p * 8 + h:p * 8 + h + 1], (tt, page)) for h in range(N_HEAD)], axis=0)
         for p in range(n_pages)], axis=1)
    bias_new = jnp.concatenate([jnp.broadcast_to(cn_row[h:h + 1], (tt, 128)) for h in range(N_HEAD)], axis=0)
    tq = _iota((N_HEAD * tt, 128), 0) % tt
    ok_new = _iota((N_HEAD * tt, 128), 1) <= tq
    s = jnp.concatenate([s_past + cq - bias_past, jnp.where(ok_new, s_new + cq - bias_new, NEG)], axis=1)
    o_ref[...] = _softmax_pv(s, vt_past, _pad_rows(v_new, 128).astype(BF), past)


def _fox_decode(z, bf_row, cache_kv, cache_lf, page_table, layer, b, tt):
    n_pages = page_table.shape[1]
    page = cache_kv.shape[3]
    kernel = functools.partial(_fox_decode_kernel, n_pages=n_pages)
    grid_spec = pltpu.PrefetchScalarGridSpec(
        num_scalar_prefetch=1, grid=(b,),
        in_specs=[pl.BlockSpec((tt, W_GROUP), lambda i, pt: (i, 0)), pl.BlockSpec((1, 128), lambda i, pt: (0, 0))]
        + _page_specs(n_pages, (2 * W_BRANCH, page), layer) + _page_specs(n_pages, (8, page), layer),
        out_specs=[pl.BlockSpec((tt, W_BRANCH), lambda i, pt: (i, 0)), pl.BlockSpec((tt, 128), lambda i, pt: (i, 0))])
    return pl.pallas_call(
        kernel, grid_spec=grid_spec,
        out_shape=[jax.ShapeDtypeStruct((b * tt, W_BRANCH), F32), jax.ShapeDtypeStruct((b * tt, 128), F32)],
        compiler_params=_params(), name="fox_decode")(page_table, z, bf_row, *([cache_kv] * n_pages), *([cache_lf] * n_pages))


def _rope_tables(pos, n_lanes, head_width, half):
    inv = jnp.power(ROPE_THETA, -jnp.arange(half, dtype=F32) / half)
    ang = pos.astype(F32)[:, None] * inv[None]
    d = np.arange(n_lanes) % head_width
    first = jnp.asarray(d < half)[None]
    second = jnp.asarray((d >= half) & (d < 2 * half))[None]
    cos = jnp.cos(ang)[:, d % half]
    sin = jnp.sin(ang)[:, d % half]
    c = jnp.where(first | second, cos, 1.0)
    s_up = jnp.where(first, -sin, 0.0)
    s_dn = jnp.where(second, sin, 0.0)
    return jnp.concatenate([c, s_up, s_dn], axis=1).astype(F32)


def _rope(x, tab, half):
    w = x.shape[1]
    return x * tab[:, 0:w] + pltpu.roll(x, w - half, 1) * tab[:, w:2 * w] + pltpu.roll(x, half, 1) * tab[:, 2 * w:3 * w]


def _dsa_prep_kernel(z_ref, tk_ref, ti_ref, kv_ref, misc_ref, *out_refs, prefill):
    tk = tk_ref[...]
    ti = ti_ref[...]
    k = _rope(z_ref[:, 0:256], tk, ROPE_HALF_QK)
    v = z_ref[:, 256:512]
    q = _rope(z_ref[:, 512:768], tk, ROPE_HALF_QK) * (HEAD_DIM ** -0.5)
    qi = _rope(z_ref[:, 768:896], ti, ROPE_HALF_IDX)
    tail = z_ref[:, 896:1024]
    ki = _rope(tail, ti, ROPE_HALF_IDX)
    kv_ref[...] = jnp.concatenate([k, v], axis=1)
    lane = _iota((1, 128), 1)
    misc = jnp.where(lane < D_IDX, ki, tail * (N_HEAD ** -0.5))
    misc_ref[...] = misc
    if prefill:
        qt_ref, k_ref, vt_ref, qit_ref, wt_ref, ki4_ref = out_refs
        qt_ref[...] = q.T.astype(BF)
        k_ref[...] = k.astype(BF)
        vt_ref[...] = v.T.astype(BF)
        qit_ref[...] = qi.T.astype(BF)
        wt_ref[...] = misc.T[D_IDX:D_IDX + 8]
        kim = jnp.where(lane < D_IDX, ki, 0.0)
        ki4 = kim + pltpu.roll(kim, 32, 1) + pltpu.roll(kim, 64, 1) + pltpu.roll(kim, 96, 1)
        ki4_ref[...] = ki4.astype(BF)
    else:
        q_ref, qi_ref = out_refs
        q_ref[...] = q
        qi_ref[...] = qi


def _dsa_prep(z, tab_k, tab_i, prefill):
    n = z.shape[0]
    tm = min(ROW_TILE, n)
    nt = tab_k.shape[0] // tm
    row = lambda w: pl.BlockSpec((tm, w), lambda i: (i, 0))
    col = lambda w: pl.BlockSpec((w, tm), lambda i: (0, i))
    out_specs = [row(512), row(128)]
    out_shape = [jax.ShapeDtypeStruct((n, 512), F32), jax.ShapeDtypeStruct((n, 128), F32)]
    if prefill:
        ck = 2 * tm
        out_specs += [col(256), row(256), pl.BlockSpec((None, W_BRANCH, tm), lambda i: (i // 2, 0, i % 2)),
                      col(128), col(8), row(128)]
        out_shape += [jax.ShapeDtypeStruct((256, n), BF), jax.ShapeDtypeStruct((n, 256), BF),
                      jax.ShapeDtypeStruct((n // ck, W_BRANCH, ck), BF), jax.ShapeDtypeStruct((128, n), BF),
                      jax.ShapeDtypeStruct((8, n), F32), jax.ShapeDtypeStruct((n, 128), BF)]
    else:
        out_specs += [row(256), row(128)]
        out_shape += [jax.ShapeDtypeStruct((n, 256), F32), jax.ShapeDtypeStruct((n, 128), F32)]
    return pl.pallas_call(
        functools.partial(_dsa_prep_kernel, prefill=prefill), grid=(n // tm,),
        in_specs=[pl.BlockSpec((tm, W_GROUP), lambda i: (i, 1)),
                  pl.BlockSpec((tm, 768), lambda i: (i % nt, 0)), pl.BlockSpec((tm, 384), lambda i: (i % nt, 0))],
        out_specs=out_specs, out_shape=out_shape, compiler_params=_params(), name="dsa_prep")(z, tab_k, tab_i)


def _sortable(x):
    b = pltpu.bitcast(x, I32)
    return b ^ ((b >> 31) & I32(0x7FFFFFFF))


def _topk_mask(score, idx, topk, n_idx_bits):
    key = _sortable(score)
    rows = score.shape[0]

    def count(pred):
        return jnp.sum(pred.astype(I32), axis=1, keepdims=True)

    def value_step(it, ans):
        cand = ans + jnp.left_shift(I32(1), I32(31) - it)
        return jnp.where(count(key >= cand) >= topk, cand, ans)

    thr = lax.fori_loop(0, 32, value_step, jnp.full((rows, 1), INT_MIN, I32))
    above = key > thr
    tie = key == thr
    need = topk - count(above)

    def index_step(it, lo):
        cand = lo + jnp.left_shift(I32(1), I32(n_idx_bits - 1) - it)
        return jnp.where(count(tie & (idx < cand)) < need, cand, lo)

    last = lax.fori_loop(0, n_idx_bits, index_step, jnp.zeros((rows, 1), I32))
    return above | (tie & (idx <= last))


def _dsa_attn_kernel(qt_ref, qit_ref, wt_ref, k_ref, vt_ref, ki4_ref, o_ref, hi_s, lo_s, tie_s, keep_s, m_s, l_s, acc_s, *,
                     topk, n_idx_bits):
    tq = qt_ref.shape[1]
    ck = hi_s.shape[1]
    i = pl.program_id(1)
    nc = _causal_chunks(i, tq, ck)
    qpos = i * tq + _iota((1, tq), 1)
    qpos16 = qpos.astype(I16)
    qith = _head_rows(qit_ref[...], D_IDX)
    w_row = [wt_ref[h:h + 1, :] for h in range(N_HEAD)]
    i16_min, i16_max = I16(-32768), I16(32767)

    def key_pos(c):
        return c * ck + _iota((ck, tq), 0)

    def score_chunk(c, carry):
        ki4 = ki4_ref[pl.ds(pl.multiple_of(c * ck, ck), ck), :]
        sc = jnp.zeros((ck, tq), F32)
        for h in range(N_HEAD):
            sc = sc + jnp.maximum(_dot(ki4, qith[h]) * (D_IDX ** -0.5), 0.0) * w_row[h]
        sc = jnp.where(key_pos(c) <= qpos, sc, NEG)
        key = _sortable(jnp.where(sc == 0.0, 0.0, sc))
        hi_s[c] = (key >> 16).astype(I16)
        lo_s[c] = ((key & I32(0xFFFF)) - I32(32768)).astype(I16)
        return carry

    lax.fori_loop(0, nc, score_chunk, 0)

    def count(pred):
        def body(c, acc):
            ones = jnp.where(pred(c), I16(1), I16(0))
            for j in range(ck // 16):
                acc = acc + ones[j * 16:(j + 1) * 16]
            return acc
        acc = lax.fori_loop(0, nc, body, jnp.zeros((16, tq), I16))
        return jnp.sum(acc.astype(I32), axis=0, keepdims=True)

    def search16(ref, want):
        def step(it, ans):
            cand = ans + jnp.left_shift(I32(1), I32(15) - it)
            c16 = cand.astype(I16)
            return jnp.where(count(lambda c: ref[c] >= c16) >= want, cand, ans)
        return lax.fori_loop(0, 16, step, jnp.full((1, tq), -32768, I32)).astype(I16)

    h16 = search16(hi_s, topk)
    n_above = count(lambda c: hi_s[c] > h16)

    def mask_lo(c, carry):
        lo_s[c] = jnp.where(hi_s[c] == h16, lo_s[c], i16_min)
        return carry

    lax.fori_loop(0, nc, mask_lo, 0)
    l16 = search16(lo_s, topk - n_above)
    need = topk - n_above - count(lambda c: lo_s[c] > l16)

    def tie_chunk(c, carry):
        kpos16 = key_pos(c).astype(I16)
        tie = (hi_s[c] == h16) & (lo_s[c] == l16) & (kpos16 <= qpos16)
        tie_s[c] = jnp.where(tie, kpos16, i16_max)
        return carry

    lax.fori_loop(0, nc, tie_chunk, 0)

    def index_step(it, lo):
        cand = lo + jnp.left_shift(I32(1), I32(n_idx_bits - 1) - it)
        c16 = cand.astype(I16)
        return jnp.where(count(lambda c: tie_s[c] < c16) < need, cand, lo)

    last16 = lax.fori_loop(0, n_idx_bits, index_step, jnp.zeros((1, tq), I32)).astype(I16)

    def keep_chunk(c, carry):
        kpos16 = key_pos(c).astype(I16)
        sel = jnp.where(hi_s[c] > h16, I16(1), I16(0)) + jnp.where(lo_s[c] > l16, I16(1), I16(0)) \
            + jnp.where(tie_s[c] <= last16, I16(1), I16(0))
        sel = jnp.where(kpos16 <= qpos16, sel, I16(0))
        keep_s[c] = jnp.where(sel.astype(I32) > 0, 0.0, NEG)
        return carry

    lax.fori_loop(0, nc, keep_chunk, 0)

    qth = _head_rows(qt_ref[...], HEAD_DIM)
    _flash_init(m_s, l_s, acc_s)

    def attend_chunk(c, carry):
        k = k_ref[pl.ds(pl.multiple_of(c * ck, ck), ck), :]
        keep = keep_s[c]
        sts = [_dot(k, qth[h]) + keep for h in range(N_HEAD)]
        vts = [vt_ref[c, h * HEAD_DIM:(h + 1) * HEAD_DIM, :] for h in range(N_HEAD)]
        _flash_update(sts, vts, m_s, l_s, acc_s)
        return carry

    lax.fori_loop(0, nc, attend_chunk, 0)
    o_ref[...] = _flash_result(l_s, acc_s)


def _dsa_attn(qt, k, vt, qit, wt, ki4, b, t, topk):
    tq = min(ROW_TILE, t)
    ck = vt.shape[2]
    nq, nc = t // tq, t // ck
    assert topk <= ck and t < 2 ** 15
    qcol = lambda w: pl.BlockSpec((w, tq), lambda bi, i: (0, bi * nq + i))
    seq = lambda w: pl.BlockSpec((t, w), lambda bi, i: (bi, 0))
    kernel = functools.partial(_dsa_attn_kernel, topk=topk, n_idx_bits=max(1, (t - 1).bit_length()))
    scratch = [pltpu.VMEM((nc, ck, tq), I16)] * 3 + [pltpu.VMEM((nc, ck, tq), F32)] + _flash_scratch(tq)
    return pl.pallas_call(
        kernel, grid=(b, nq),
        in_specs=[qcol(256), qcol(128), qcol(8), seq(256), pl.BlockSpec((nc, W_BRANCH, ck), lambda bi, i: (bi, 0, 0)),
                  seq(128)],
        out_specs=pl.BlockSpec((tq, W_BRANCH), lambda bi, i: (bi * nq + i, 0)),
        out_shape=jax.ShapeDtypeStruct((b * t, W_BRANCH), F32), scratch_shapes=scratch,
        compiler_params=_params(2), name="dsa_attn")(qt, qit, wt, k, vt, ki4)


def _dsa_select_kernel(pt_ref, qi_ref, misc_ref, *refs, n_pages, n_seq, tt, topk, t_valid):
    ki_refs = refs[:n_seq * n_pages]
    keep_ref = refs[n_seq * n_pages]
    page = ki_refs[0].shape[1]
    past = n_pages * page
    scores = []
    for j in range(n_seq):
        rows = slice(j * tt, (j + 1) * tt)
        qi = qi_ref[rows, :]
        qi_h = jnp.concatenate([qi[:, h * D_IDX:(h + 1) * D_IDX] for h in range(N_HEAD)], axis=0).astype(BF)
        w_col = jnp.concatenate([misc_ref[rows, D_IDX + h:D_IDX + h + 1] for h in range(N_HEAD)], axis=0)
        kit_past = _cat_pages(ki_refs[j * n_pages:(j + 1) * n_pages], 0, D_IDX).astype(BF)
        ki_new = _pad_rows(misc_ref[rows, 0:D_IDX], 128).astype(BF)
        s = jnp.concatenate([_dot(qi_h, kit_past), _dot_nt(qi_h, ki_new)], axis=1)
        scores.append(_unstack_heads(jnp.maximum(s * (D_IDX ** -0.5), 0.0) * w_col))
    score = jnp.concatenate(scores, axis=0)
    n_keys = past + 128
    idx = _iota((1, n_keys), 1)
    new_t = idx - past
    t_q = _iota((n_seq * tt, 1), 0) % tt
    visible = (new_t <= t_q) & (new_t < t_valid)
    score = jnp.where(visible, jnp.where(score == 0.0, 0.0, score), NEG)
    sel = _topk_mask(score, idx, topk, n_keys.bit_length()) & visible
    keep_ref[...] = jnp.where(sel, 0.0, NEG)


def _dsa_select(qi, misc, cache_ki, page_table, layer, b, tt, topk, t_valid):
    n_pages = page_table.shape[1]
    page = cache_ki.shape[3]
    n_seq = min(SELECT_SEQS, b)
    n_keys = n_pages * page + 128
    kernel = functools.partial(_dsa_select_kernel, n_pages=n_pages, n_seq=n_seq, tt=tt, topk=topk, t_valid=t_valid)

    def page_spec(j, p):
        return pl.BlockSpec((None, None, D_IDX, page), lambda g, pt: (layer, pt[g * n_seq + j, p], 0, 0))

    rows = lambda w: pl.BlockSpec((n_seq * tt, w), lambda g, pt: (g, 0))
    grid_spec = pltpu.PrefetchScalarGridSpec(
        num_scalar_prefetch=1, grid=(b // n_seq,),
        in_specs=[rows(128), rows(128)] + [page_spec(j, p) for j in range(n_seq) for p in range(n_pages)],
        out_specs=rows(n_keys))
    return pl.pallas_call(
        kernel, grid_spec=grid_spec, out_shape=jax.ShapeDtypeStruct((b * tt, n_keys), F32),
        compiler_params=_params(), name="dsa_select")(page_table, qi, misc, *([cache_ki] * (n_seq * n_pages)))


def _dsa_decode_kernel(pt_ref, q_ref, keep_ref, kvn_ref, *refs, n_pages):
    kv_refs = refs[:n_pages]
    o_ref = refs[n_pages]
    past = n_pages * kv_refs[0].shape[1]
    qbd = _stack_heads(q_ref[...]).astype(BF)
    k_new = _pad_rows(kvn_ref[:, 0:256], 128).astype(BF)
    v_new = _pad_rows(kvn_ref[:, 256:512], 128).astype(BF)
    s = jnp.concatenate([_dot(qbd, _cat_pages(kv_refs, 0, W_BRANCH).astype(BF)), _dot_nt(qbd, k_new)], axis=1)
    keep4 = jnp.concatenate([keep_ref[...]] * N_HEAD, axis=0)
    vt_past = _cat_pages(kv_refs, W_BRANCH, 2 * W_BRANCH).astype(BF)
    o_ref[...] = _softmax_pv(jnp.where(keep4 == 0.0, s, NEG), vt_past, v_new, past)


def _dsa_decode(q, keep, kv_new, cache_kv, page_table, layer, b, tt):
    n_pages = page_table.shape[1]
    page = cache_kv.shape[3]
    kernel = functools.partial(_dsa_decode_kernel, n_pages=n_pages)
    row = lambda w: pl.BlockSpec((tt, w), lambda i, pt: (i, 0))
    grid_spec = pltpu.PrefetchScalarGridSpec(
        num_scalar_prefetch=1, grid=(b,),
        in_specs=[row(256), row(keep.shape[1]), row(512)] + _page_specs(n_pages, (2 * W_BRANCH, page), layer),
        out_specs=row(256))
    return pl.pallas_call(
        kernel, grid_spec=grid_spec, out_shape=jax.ShapeDtypeStruct((b * tt, W_BRANCH), F32),
        compiler_params=_params(), name="dsa_decode")(page_table, q, keep, kv_new, *([cache_kv] * n_pages))


def _rwkv_chunks(chunks, n_double):
    c = chunks[0][0].shape[0]
    cc = N_HEAD * c
    tri = _tril(c).astype(BF)
    strict = _tril(cc, strict=True)
    incl = _tril(cc)
    each = lambda fn, *lists: [fn(*xs) for xs in zip(*lists)]
    r, k, v, lw, kap, beta = (list(x) for x in zip(*chunks))
    g = each(lambda x: _dot_sel_l(tri, x), lw)
    g_end = [x[c - 1:c] for x in g]
    e_neg = each(lambda x: jnp.exp(-x), g)
    e_end = each(lambda x, y: jnp.exp(y - x), g, g_end)
    a_f = each(lambda kp, x, l: _stack_heads(kp * jnp.exp(x - l)), kap, g, lw)
    a_s = each(lambda x: x.astype(BF), a_f)
    r_s = each(lambda x, y: _stack_heads(x * jnp.exp(y)), r, g)
    r_sb = each(lambda x: x.astype(BF), r_s)
    bb_s = each(lambda x, e: _stack_heads(x * e).astype(BF), beta, e_neg)
    bk_s = each(lambda x, e: _stack_heads(x * e).astype(BF), k, e_neg)
    v_s = each(lambda x: _stack_heads(x).astype(BF), v)
    kh_s = each(lambda x, e: _stack_heads(x * e).astype(BF), k, e_end)
    bh_s = each(lambda x, e: _stack_heads(x * e).astype(BF), beta, e_end)
    l_b = each(lambda x, y: jnp.where(strict, _dot_nt(x, y), 0.0), a_s, bb_s)
    l_k = each(lambda x, y: jnp.where(strict, _dot_nt(x, y), 0.0).astype(BF), a_s, bk_s)
    w_b = each(lambda x, y: jnp.where(incl, _dot_nt(x, y), 0.0).astype(BF), r_sb, bb_s)
    w_k = each(lambda x, y: jnp.where(incl, _dot_nt(x, y), 0.0).astype(BF), r_sb, bk_s)
    y = each(lambda x: -x, l_b)
    n = y
    for _ in range(n_double):
        y = each(lambda x: _dot(x.astype(BF), x.astype(BF)), y)
        n = each(lambda p, q: p + q + _dot(p.astype(BF), q.astype(BF)), n, y)
    nb = each(lambda x: x.astype(BF), n)
    a_t = each(lambda f, p, q: (f + _dot(p, q)).astype(BF), a_f, nb, a_s)
    lkv = each(_dot, l_k, v_s)
    u0 = each(lambda x, p: (x + _dot(p, x.astype(BF))).astype(BF), lkv, nb)
    r_hat = each(lambda x, w, a: (x - _dot(w, a)).astype(BF), r_s, w_b, a_t)
    y0 = each(lambda wk, vs, wb, u: _dot(wk, vs) - _dot(wb, u), w_k, v_s, w_b, u0)
    h_mat = each(lambda a, b: _dot_tn(a, b).astype(BF), a_t, bh_s)
    s_add = each(lambda vs, kh, u, bh: _dot_tn(vs, kh) - _dot_tn(u, bh), v_s, kh_s, u0, bh_s)
    decay = each(jnp.exp, g_end)
    return list(zip(r_hat, y0, decay, h_mat, s_add))


def _rwkv_apply(par, s_big):
    r_hat, y0, decay, h_mat, s_add = par
    sb = s_big.astype(BF)
    ys = _dot_nt(r_hat, sb) + y0
    s_new = s_big * decay - _dot(sb, h_mat) + s_add
    return _unstack_heads(ys), s_new


def _to_block_diag(x):
    return jnp.where(_block_diag_mask(W_BRANCH, HEAD_DIM), jnp.concatenate([x] * N_HEAD, axis=1), 0.0)


def _from_block_diag(x):
    y = x + pltpu.roll(x, 64, 1) + pltpu.roll(x, 128, 1) + pltpu.roll(x, 192, 1)
    return y[:, 0:HEAD_DIM]


def _rwkv_kernel(*refs, seq_len, t_valid, chunk, n_double, lockstep, has_state):
    if has_state:
        (z_ref, prev_ref, s_in_ref, mu_ref, vec_ref, wa_ref, g2_ref, o_ref, s_out_ref,
         r_s, k_s, v_s, lw_s, kap_s, beta_s, y_s) = refs
    else:
        (z_ref, mu_ref, vec_ref, wa_ref, g2_ref, o_ref, s_out_ref,
         r_s, k_s, v_s, lw_s, kap_s, beta_s, y_s, state_s, last_s) = refs
    rows = z_ref.shape[0]
    i = pl.program_id(0)
    t_row = (i * rows + _iota((rows, 1), 0)) % seq_len
    pr = z_ref[...]
    shifted = pltpu.roll(pr, 1, 0)
    if has_state:
        prev = jnp.where(t_row == 0, prev_ref[...], shifted)
    else:
        first = (i * rows) % seq_len == 0

        @pl.when(first)
        def _():
            last_s[...] = jnp.zeros_like(last_s)
            state_s[...] = jnp.zeros_like(state_s)

        prev = jnp.where(_iota((rows, 1), 0) == 0, last_s[7:8, :], shifted)
        last_s[...] = pr[rows - 8:rows]
    xs = pr + (prev - pr) * mu_ref[...]
    r, k, v = xs[:, 0:256], xs[:, 256:512], xs[:, 512:768]
    lora = xs[:, 768:896]
    lora = jnp.where(_iota((1, 128), 1) < LORA_W, jnp.tanh(lora), lora)
    wa = _dot(lora.astype(BF), wa_ref[...])
    w0, a0, kk_p, ka_p = vec_ref[0:1], vec_ref[1:2], vec_ref[2:3], vec_ref[3:4]
    rk_p, ln_w, ln_b = vec_ref[4:5], vec_ref[5:6], vec_ref[6:7]
    w_log = -_softplus(-(w0 + wa[:, 0:256])) - 0.5
    lw = -jnp.exp(w_log)
    a = _sigmoid(a0 + wa[:, 256:512])
    gate = _dot(_sigmoid(xs[:, 896:1024]).astype(BF), g2_ref[...])
    ones_bd = _block_diag_mask(W_BRANCH, HEAD_DIM).astype(BF)
    kk = k * kk_p
    kap = kk / jnp.maximum(jnp.sqrt(_head_sum(kk * kk, ones_bd)), 1e-12)
    k2 = k * (1.0 + (a - 1.0) * ka_p)
    bonus = _head_sum(r * k2 * rk_p, ones_bd) * v
    live = t_row < t_valid
    r_s[...] = r
    k_s[...] = jnp.where(live, k2, 0.0)
    v_s[...] = jnp.where(live, v, 0.0)
    lw_s[...] = jnp.where(live, lw, 0.0)
    kap_s[...] = jnp.where(live, kap, 0.0)
    beta_s[...] = jnp.where(live, a * kap, 0.0)

    def group(gi, carry):
        def rows_of(j, size):
            return pl.ds(pl.multiple_of((gi * lockstep + j) * size, size), size)

        sls = [rows_of(j, chunk) for j in range(lockstep)]
        pars = _rwkv_chunks([(r_s[sl, :], k_s[sl, :], v_s[sl, :], lw_s[sl, :], kap_s[sl, :], beta_s[sl, :]) for sl in sls],
                            n_double)
        if has_state:
            for j, (sl, par) in enumerate(zip(sls, pars)):
                st = rows_of(j, W_BRANCH)
                y, s_new = _rwkv_apply(par, _to_block_diag(s_in_ref[st, :]))
                y_s[sl, :] = y
                s_out_ref[st, :] = _from_block_diag(s_new)
        else:
            s_big = state_s[...]
            for sl, par in zip(sls, pars):
                y, s_big = _rwkv_apply(par, s_big)
                y_s[sl, :] = y
            state_s[...] = s_big
        return carry

    lax.fori_loop(0, rows // (chunk * lockstep), group, 0)
    if not has_state:
        s_out_ref[...] = _from_block_diag(state_s[...])
    y = y_s[...]
    mu = _head_sum(y, ones_bd) * (1.0 / HEAD_DIM)
    yc = y - mu
    var = _head_sum(yc * yc, ones_bd) * (1.0 / HEAD_DIM)
    yn = yc * lax.rsqrt(var + RWKV_GN_EPS) * ln_w + ln_b
    o_ref[...] = (yn + bonus) * gate


def _rwkv(z, prev_rows, state_in, mu, vec, wa, g2, b, seq_len, t_valid):
    n = z.shape[0]
    has_state = state_in is not None
    rows = min(ROW_TILE, n)
    chunk = min(RWKV_CHUNK, seq_len)
    n_double = max(0, int(math.log2(chunk)) - 1)
    steps = n // rows
    seq_per_tile = max(1, rows // seq_len)
    tiles_per_seq = max(1, seq_len // rows)
    lockstep = min(RWKV_LOCKSTEP, rows // chunk)
    kernel = functools.partial(_rwkv_kernel, seq_len=seq_len, t_valid=t_valid, chunk=chunk, n_double=n_double,
                               lockstep=lockstep, has_state=has_state)
    consts = [_const_spec((1, W_GROUP)), _const_spec((8, W_BRANCH)), _const_spec((128, 512)), _const_spec((128, W_BRANCH))]
    zspec = pl.BlockSpec((rows, W_GROUP), lambda i: (i, 2))
    scratch = [pltpu.VMEM((rows, W_BRANCH), F32)] * 7
    if has_state:
        srows = seq_per_tile * W_BRANCH
        in_specs = [zspec, pl.BlockSpec((rows, W_GROUP), lambda i: (i, 0)), pl.BlockSpec((srows, HEAD_DIM), lambda i: (i, 0))] + consts
        s_spec = pl.BlockSpec((srows, HEAD_DIM), lambda i: (i, 0))
        args = (z, prev_rows, state_in, mu, vec, wa, g2)
    else:
        in_specs = [zspec] + consts
        s_spec = pl.BlockSpec((W_BRANCH, HEAD_DIM), lambda i: (i // tiles_per_seq, 0))
        scratch = scratch + [pltpu.VMEM((W_BRANCH, W_BRANCH), F32), pltpu.VMEM((8, W_GROUP), F32)]
        args = (z, mu, vec, wa, g2)
    return pl.pallas_call(
        kernel, grid=(steps,), in_specs=in_specs,
        out_specs=[pl.BlockSpec((rows, W_BRANCH), lambda i: (i, 0)), s_spec],
        out_shape=[jax.ShapeDtypeStruct((n, W_BRANCH), F32), jax.ShapeDtypeStruct((b * W_BRANCH, HEAD_DIM), F32)],
        scratch_shapes=scratch, compiler_params=_params(), name="rwkv")(*args)


def _hgrn_lb_kernel(x_ref, o_ref):
    x = x_ref[...]
    depth = x.shape[0]
    e = jnp.exp(x - jnp.max(x, axis=0, keepdims=True))
    soft = e / jnp.sum(e, axis=0, keepdims=True)
    cum = jnp.zeros((1, x.shape[1]), F32)
    for l in range(depth):
        cum = cum + soft[l:l + 1]
        lb = jnp.maximum(cum - soft[0:1], 0.0)
        o_ref[l, 0:1, :] = lb
        o_ref[l, 1:2, :] = jnp.log(jnp.maximum(lb, LB_TINY))
        o_ref[l, 2:3, :] = jnp.log1p(-lb)
        o_ref[l, 3:8, :] = jnp.zeros((5, x.shape[1]), F32)


def _hgrn_lb(hgrn_lb):
    depth, w = hgrn_lb.shape
    return pl.pallas_call(_hgrn_lb_kernel, out_shape=jax.ShapeDtypeStruct((depth, 8, w), F32), name="hgrn_lb")(hgrn_lb)


def _hgrn_blocks(blocks, ones_bd, bd_mask):
    c = blocks[0][0].shape[0]
    tri = _tril(c).astype(BF)
    t_idx = _iota((c, 1), 0)
    each = lambda fn, *lists: [fn(*xs) for xs in zip(*lists)]
    q, k, v, g = (list(x) for x in zip(*blocks))
    b = each(lambda x: _dot_sel_l(tri, x), g)
    b_end = [x[c - 1:c] for x in b]

    def pair_terms(qq, kk, bb):
        xs = []
        for s in range(c):
            e = jnp.exp(jnp.where(t_idx >= s, bb - bb[s:s + 1], NEG))
            xs.append((qq * e * kk[s:s + 1]).astype(BF))
        return jnp.concatenate(xs, axis=0)

    col = each(lambda qq, kk, bb: _dot(pair_terms(qq, kk, bb), ones_bd), q, k, b)

    def inside(cl, vv):
        o = cl[0:c] * vv[0:1]
        for s in range(1, c):
            o = o + cl[s * c:(s + 1) * c] * vv[s:s + 1]
        return o

    o_in = each(inside, col, v)
    qd = each(lambda qq, bb: (qq * jnp.exp(bb)).astype(BF), q, b)
    upd = each(lambda vv, kk, bb, be: jnp.where(bd_mask, _dot_tn(vv.astype(BF), (kk * jnp.exp(be - bb)).astype(BF)), 0.0),
               v, k, b, b_end)
    decay = each(jnp.exp, b_end)
    return list(zip(qd, o_in, decay, upd))


def _hgrn_apply(par, st):
    qd, o_in, decay, upd = par
    return _dot_nt(qd, st.astype(BF)) + o_in, st * decay + upd


def _hgrn_kernel(*refs, seq_len, t_valid, block, has_state):
    if has_state:
        z_ref, s_in_ref, lb_ref, nw_ref, o_ref, s_out_ref, q_s, k_s, v_s, g_s, y_s = refs
    else:
        z_ref, lb_ref, nw_ref, o_ref, s_out_ref, q_s, k_s, v_s, g_s, y_s, state_s = refs
    rows = z_ref.shape[0]
    i = pl.program_id(0)
    t_row = (i * rows + _iota((rows, 1), 0)) % seq_len
    live = t_row < t_valid
    hq, hf, hi, hg = z_ref[:, 0:256], z_ref[:, 256:512], z_ref[:, 512:768], z_ref[:, 768:1024]
    lb, lb_log, l1m = lb_ref[0:1], lb_ref[1:2], lb_ref[2:3]
    ls = _log_sigmoid(hf)
    x2 = l1m + ls
    lae = jnp.maximum(lb_log, x2) + jnp.log1p(jnp.exp(-jnp.abs(lb_log - x2)))
    logf = jnp.where(lb > 0.0, lae, ls)
    q_s[...] = hq * _sigmoid(hq)
    k_s[...] = jnp.where(live, (1.0 - lb) * _sigmoid(-hf), 0.0)
    v_s[...] = hi
    g_s[...] = jnp.where(live, logf, 0.0)
    ones_bd = _block_diag_mask(W_BRANCH, HEAD_DIM).astype(BF)
    bd_mask = _block_diag_mask(W_BRANCH, HEAD_DIM)

    if not has_state:
        @pl.when((i * rows) % seq_len == 0)
        def _():
            state_s[...] = jnp.zeros_like(state_s)

    lockstep = min(HGRN_LOCKSTEP, rows // block)

    def group(gi, carry):
        def rows_of(j, size):
            return pl.ds(pl.multiple_of((gi * lockstep + j) * size, size), size)

        sls = [rows_of(j, block) for j in range(lockstep)]
        pars = _hgrn_blocks([(q_s[sl, :], k_s[sl, :], v_s[sl, :], g_s[sl, :]) for sl in sls], ones_bd, bd_mask)
        if has_state:
            for j, (sl, par) in enumerate(zip(sls, pars)):
                sr = rows_of(j, W_BRANCH)
                o, st_new = _hgrn_apply(par, _to_block_diag(s_in_ref[sr, :]).T)
                y_s[sl, :] = o
                s_out_ref[sr, :] = _from_block_diag(st_new.T)
        else:
            st = state_s[...]
            for sl, par in zip(sls, pars):
                o, st = _hgrn_apply(par, st)
                y_s[sl, :] = o
            state_s[...] = st
        return carry

    lax.fori_loop(0, rows // (block * lockstep), group, 0)
    if not has_state:
        s_out_ref[...] = _from_block_diag(state_s[...].T)
    o = y_s[...]
    ms = _head_sum(o * o, ones_bd) * (1.0 / HEAD_DIM)
    o_ref[...] = o * lax.rsqrt(ms + RMS_EPS) * nw_ref[...] * (hg * _sigmoid(hg))


def _hgrn(z, state_in, lb_rows, norm_w, b, seq_len, t_valid):
    n = z.shape[0]
    has_state = state_in is not None
    rows = min(ROW_TILE, n)
    block = min(HGRN_BLOCK, seq_len)
    seq_per_tile = max(1, rows // seq_len)
    tiles_per_seq = max(1, seq_len // rows)
    kernel = functools.partial(_hgrn_kernel, seq_len=seq_len, t_valid=t_valid, block=block, has_state=has_state)
    zspec = pl.BlockSpec((rows, W_GROUP), lambda i: (i, 3))
    consts = [_const_spec((8, W_BRANCH)), _const_spec((1, W_BRANCH))]
    scratch = [pltpu.VMEM((rows, W_BRANCH), F32)] * 5
    if has_state:
        srows = seq_per_tile * W_BRANCH
        in_specs = [zspec, pl.BlockSpec((srows, HEAD_DIM), lambda i: (i, 0))] + consts
        s_spec = pl.BlockSpec((srows, HEAD_DIM), lambda i: (i, 0))
        args = (z, state_in, lb_rows, norm_w)
    else:
        in_specs = [zspec] + consts
        s_spec = pl.BlockSpec((W_BRANCH, HEAD_DIM), lambda i: (i // tiles_per_seq, 0))
        scratch = scratch + [pltpu.VMEM((W_BRANCH, W_BRANCH), F32)]
        args = (z, lb_rows, norm_w)
    return pl.pallas_call(
        kernel, grid=(n // rows,), in_specs=in_specs,
        out_specs=[pl.BlockSpec((rows, W_BRANCH), lambda i: (i, 0)), s_spec],
        out_shape=[jax.ShapeDtypeStruct((n, W_BRANCH), F32), jax.ShapeDtypeStruct((b * W_BRANCH, HEAD_DIM), F32)],
        scratch_shapes=scratch, compiler_params=_params(), name="hgrn")(*args)


def _merge_kernel(x_ref, oa_ref, ob_ref, oc_ref, od_ref, wg_ref, wb_ref, wo_ref, ln_ref, h_ref, *, alpha):
    x = x_ref[...]
    xb = x.astype(BF)
    m = jnp.zeros(x.shape, F32)
    for n, o_ref in enumerate((oa_ref, ob_ref, oc_ref, od_ref)):
        gate = _sigmoid(_dot(xb, wg_ref[:, n * D_MODEL:(n + 1) * D_MODEL]))
        m = m + gate * _dot(o_ref[...].astype(BF), wb_ref[n])
    mix = _dot(m.astype(BF), wo_ref[...])
    h_ref[...] = _layer_norm(alpha * x + mix, ln_ref[0:1], ln_ref[1:2])


def _merge(x, oa, ob, oc, od, wg, wb, wo, ln, alpha):
    n = x.shape[0]
    tm = min(ROW_TILE, n)
    row = lambda w: pl.BlockSpec((tm, w), lambda i: (i, 0))
    return pl.pallas_call(
        functools.partial(_merge_kernel, alpha=alpha), grid=(n // tm,),
        in_specs=[row(D_MODEL), row(256), row(256), row(256), row(256), _const_spec(wg.shape), _const_spec(wb.shape),
                  _const_spec(wo.shape), _const_spec((8, D_MODEL))],
        out_specs=row(D_MODEL), out_shape=jax.ShapeDtypeStruct((n, D_MODEL), F32),
        compiler_params=_params(), name="merge")(x, oa, ob, oc, od, wg, wb, wo, ln)


def _ffn_kernel(*refs, seq_len, has_state, n_split, alpha):
    if has_state:
        h_ref, p1_ref, p2_ref, wup_ref, wdn_ref, cv_ref, ln_ref, y_ref, a_ref = refs
    else:
        h_ref, wup_ref, wdn_ref, cv_ref, ln_ref, y_ref, a_ref, last_s = refs
    rows = h_ref.shape[0]
    i = pl.program_id(0)
    h = h_ref[...]
    hb = h.astype(BF)
    ridx = _iota((rows, 1), 0)
    t_row = (i * rows + ridx) % seq_len
    wf = D_FF // n_split
    f = jnp.zeros((rows, D_MODEL), F32)
    if not has_state:
        @pl.when((i * rows) % seq_len == 0)
        def _():
            last_s[...] = jnp.zeros_like(last_s)
    for j in range(n_split):
        lo, hi = j * wf, (j + 1) * wf
        a = _dot(hb, wup_ref[:, lo:hi])
        gt = _dot(hb, wup_ref[:, D_FF + lo:D_FF + hi])
        r1 = pltpu.roll(a, 1, 0)
        r2 = pltpu.roll(a, 2, 0)
        if has_state:
            prev1 = jnp.where(t_row == 0, p1_ref[:, lo:hi], r1)
            prev2 = jnp.where(t_row < 2, p2_ref[:, lo:hi], r2)
            a_ref[:, lo:hi] = a
        else:
            c6 = last_s[6:7, lo:hi]
            c7 = last_s[7:8, lo:hi]
            prev1 = jnp.where(ridx == 0, c7, r1)
            prev2 = jnp.where(ridx == 0, c6, jnp.where(ridx == 1, c7, r2))
            last_s[:, lo:hi] = a[rows - 8:rows]
            a_ref[:, lo:hi] = a[rows - 8:rows]
        conv = cv_ref[3:4, lo:hi] + prev2 * cv_ref[0:1, lo:hi] + prev1 * cv_ref[1:2, lo:hi] + a * cv_ref[2:3, lo:hi]
        hid = _gelu(conv) * gt
        f = f + _dot(hid.astype(BF), wdn_ref[lo:hi, :])
    y_ref[...] = _layer_norm(alpha * h + f, ln_ref[0:1], ln_ref[1:2])


def _ffn(h, p1, p2, wup, wdn, cv, ln, seq_len, alpha):
    n = h.shape[0]
    has_state = p1 is not None
    tm = min(ROW_TILE, n)
    row = lambda w: pl.BlockSpec((tm, w), lambda i: (i, 0))
    kernel = functools.partial(_ffn_kernel, seq_len=seq_len, has_state=has_state, n_split=2, alpha=alpha)
    consts = [_const_spec(wup.shape), _const_spec(wdn.shape), _const_spec((8, D_FF)), _const_spec((8, D_MODEL))]
    if has_state:
        in_specs = [row(D_MODEL), row(D_FF), row(D_FF)] + consts
        a_spec, a_rows, scratch = row(D_FF), n, []
        args = (h, p1, p2, wup, wdn, cv, ln)
    else:
        in_specs = [row(D_MODEL)] + consts
        a_spec, a_rows = pl.BlockSpec((8, D_FF), lambda i: (i, 0)), (n // tm) * 8
        scratch = [pltpu.VMEM((8, D_FF), F32)]
        args = (h, wup, wdn, cv, ln)
    return pl.pallas_call(
        kernel, grid=(n // tm,), in_specs=in_specs, out_specs=[row(D_MODEL), a_spec],
        out_shape=[jax.ShapeDtypeStruct((n, D_MODEL), F32), jax.ShapeDtypeStruct((a_rows, D_FF), F32)],
        scratch_shapes=scratch, compiler_params=_params(), name="ffn")(*args)


def _pad_lanes(x, width):
    return jnp.pad(x, [(0, 0)] * (x.ndim - 1) + [(0, width - x.shape[-1])])


def _regroup_w_in(w_in):
    o = 0
    cols = {}
    for name, w in (('fox_q', 256), ('fox_k', 256), ('fox_v', 256), ('fox_f', 4), ('dsa_q', 256), ('dsa_k', 256),
                    ('dsa_v', 256), ('idx_q', 128), ('idx_k', 32), ('idx_w', 4), ('rwkv', 1024), ('hgrn', 1024),
                    ('gate', 4096)):
        cols[name] = w_in[..., o:o + w]
        o += w
    g0 = _pad_lanes(jnp.concatenate([cols['fox_k'], cols['fox_v'], cols['fox_q'], cols['fox_f']], -1), W_GROUP)
    g1 = _pad_lanes(jnp.concatenate([cols['dsa_k'], cols['dsa_v'], cols['dsa_q'], cols['idx_q'], cols['idx_k'],
                                     cols['idx_w']], -1), W_GROUP)
    wz = jnp.concatenate([g0, g1, cols['rwkv'], cols['hgrn']], -1).astype(BF)
    return wz, cols['gate'].astype(BF)


def _rows8(*vecs, width):
    rows = [v.reshape(1, width) for v in vecs]
    rows.append(jnp.zeros((8 - len(rows), width), F32))
    return jnp.concatenate(rows, axis=0)


def _expand_first_rows(state, tt, offsets):
    b, _, w = state.shape
    out = jnp.zeros((b, tt, w), state.dtype)
    for s, ts in offsets:
        out = out.at[:, ts].set(state[:, s])
    return out.reshape(b * tt, w)


def _layer(x, cfg, lw):
    b, tt, tv, past = cfg['b'], cfg['tt'], cfg['tv'], cfg['past']
    decode = past > 0
    z = _inproj(x, lw['wz'])
    topk = max(1, min(DSA_TOPK, (past + tv) // 4))
    new = {}
    if decode:
        o_a, logf = _fox_decode(z, lw['fox_bf'], cfg['fox_kv'], cfg['fox_lf'], cfg['page_table'], cfg['layer'], b, tt)
    else:
        qt_a, k_a, vt_a, logf, ccol, crow = _fox_prep(z, lw['fox_bf'], b, tt)
        o_a = _fox_attn(qt_a, k_a, vt_a, ccol, crow, b, tt)
    new['fox_kv'] = z[:, 0:512]
    new['fox_logf'] = logf[:, 0:N_HEAD]
    if decode:
        kv_b, misc_b, q_b, qi_b = _dsa_prep(z, cfg['tab_k'], cfg['tab_i'], False)
        keep = _dsa_select(qi_b, misc_b, cfg['dsa_ki'], cfg['page_table'], cfg['layer'], b, tt, topk, tv)
        o_b = _dsa_decode(q_b, keep, kv_b, cfg['dsa_kv'], cfg['page_table'], cfg['layer'], b, tt)
    else:
        kv_b, misc_b, qt_b, k_b, vt_b, qit_b, wt_b, ki4_b = _dsa_prep(z, cfg['tab_k'], cfg['tab_i'], True)
        o_b = _dsa_attn(qt_b, k_b, vt_b, qit_b, wt_b, ki4_b, b, tt, topk)
    new['dsa_kv'] = kv_b
    new['dsa_kidx'] = misc_b[:, 0:D_IDX]
    o_c, new['rwkv'] = _rwkv(z, cfg.get('shift_rows'), cfg.get('rwkv_state'), lw['rwkv_mu'], lw['rwkv_vec'], lw['rwkv_wa'],
                             lw['rwkv_g2'], b, tt, tv)
    new['shift'] = z[:, 2 * W_GROUP:3 * W_GROUP]
    o_d, new['hgrn'] = _hgrn(z, cfg.get('hgrn_state'), lw['hgrn_lb'], lw['hgrn_nw'], b, tt, tv)
    h = _merge(x, o_a, o_b, o_c, o_d, lw['wg'], lw['wb'], lw['wo'], lw['ln1'], cfg['alpha'])
    y, new['conv'] = _ffn(h, cfg.get('conv_p1'), cfg.get('conv_p2'), lw['wup'], lw['wdn'], lw['conv'], lw['ln2'], tt, cfg['alpha'])
    return y, new


def kernel(x_prompt, x_sample, cache_fox_kv, cache_fox_logf, cache_dsa_kv, cache_dsa_kidx, state_rwkv, state_rwkv_shift, state_hgrn, state_ffn_conv, page_table, w_in, fox_bf, rwkv_mu, rwkv_w0, rwkv_w2, rwkv_a0, rwkv_a2, rwkv_g2, rwkv_kk, rwkv_ka, rwkv_rk, rwkv_ln_w, rwkv_ln_b, hgrn_lb, hgrn_norm_w, w_branch, w_o, ln1_g, ln1_b, ln2_g, ln2_b, ffn_w_in, ffn_conv_w, ffn_conv_b, ffn_w_out):
    depth = w_in.shape[0]
    bp, tp, d = x_prompt.shape
    bs, ts, _ = x_sample.shape
    n_pool, page = cache_fox_kv.shape[1], cache_fox_kv.shape[2]
    n_pages = page_table.shape[1]
    past = n_pages * page
    tsp = -(-ts // T_ALIGN) * T_ALIGN
    assert d == D_MODEL and tp % min(ROW_TILE, tp) == 0 and (bs * tsp) % min(ROW_TILE, bs * tsp) == 0
    assert tsp <= 128 and ts >= 2

    wz_all, wg_all = _regroup_w_in(w_in)
    wb_all, wo_all = w_branch.astype(BF), w_o.astype(BF)
    wup_all, wdn_all = ffn_w_in.astype(BF), ffn_w_out.astype(BF)
    zero_l = jnp.zeros((depth, LORA_W, W_BRANCH), F32)
    wa_all = jnp.concatenate([jnp.concatenate([rwkv_w2, zero_l], 2), jnp.concatenate([zero_l, rwkv_a2], 2)], 1).astype(BF)
    g2_all = rwkv_g2.astype(BF)
    lb_all = _hgrn_lb(hgrn_lb)
    page_table = page_table.astype(I32)

    fox_kv_pages = jnp.transpose(cache_fox_kv, (0, 1, 3, 4, 5, 2)).reshape(depth, n_pool, 2 * W_BRANCH, page)
    dsa_kv_pages = jnp.transpose(cache_dsa_kv, (0, 1, 3, 4, 5, 2)).reshape(depth, n_pool, 2 * W_BRANCH, page)
    dsa_ki_pages = jnp.swapaxes(cache_dsa_kidx, 2, 3)
    fox_lf_pages = _pad_rows_nd(jnp.swapaxes(cache_fox_logf, 2, 3), 8)

    pos_p = jnp.arange(tp)
    pos_s = past + (jnp.arange(bs * tsp) % tsp)[:min(ROW_TILE, bs * tsp)]
    alpha = (2 * depth) ** 0.25
    cfg_p = dict(b=bp, tt=tp, tv=tp, past=0, alpha=alpha,
                 tab_k=_rope_tables(pos_p, 256, HEAD_DIM, ROPE_HALF_QK), tab_i=_rope_tables(pos_p, 128, D_IDX, ROPE_HALF_IDX))
    cfg_s = dict(b=bs, tt=tsp, tv=ts, past=past, alpha=alpha, page_table=page_table, fox_kv=fox_kv_pages, fox_lf=fox_lf_pages,
                 dsa_kv=dsa_kv_pages, dsa_ki=dsa_ki_pages,
                 tab_k=_rope_tables(pos_s, 256, HEAD_DIM, ROPE_HALF_QK), tab_i=_rope_tables(pos_s, 128, D_IDX, ROPE_HALF_IDX))

    xp = x_prompt.reshape(bp * tp, d)
    xs = jnp.pad(x_sample, ((0, 0), (0, tsp - ts), (0, 0))).reshape(bs * tsp, d)
    new_p, new_s = [], []
    for l in range(depth):
        lw = dict(wz=wz_all[l], wg=wg_all[l], wb=wb_all[l], wo=wo_all[l], wup=wup_all[l], wdn=wdn_all[l],
                  fox_bf=_pad_lanes(fox_bf[l][None], 128), rwkv_mu=rwkv_mu[l][None],
                  rwkv_vec=_rows8(rwkv_w0[l], rwkv_a0[l], rwkv_kk[l], rwkv_ka[l], rwkv_rk[l], rwkv_ln_w[l], rwkv_ln_b[l],
                                  width=W_BRANCH),
                  rwkv_wa=wa_all[l], rwkv_g2=g2_all[l], hgrn_lb=lb_all[l], hgrn_nw=hgrn_norm_w[l][None],
                  ln1=_rows8(ln1_g[l], ln1_b[l], width=D_MODEL), ln2=_rows8(ln2_g[l], ln2_b[l], width=D_MODEL),
                  conv=_rows8(ffn_conv_w[l, 0], ffn_conv_w[l, 1], ffn_conv_w[l, 2], ffn_conv_b[l], width=D_FF))
        xp, st_p = _layer(xp, cfg_p, lw)
        cfg_l = dict(cfg_s, layer=l,
                     shift_rows=_expand_first_rows(state_rwkv_shift[l][:, None], tsp, ((0, 0),)),
                     rwkv_state=state_rwkv[l].reshape(bs * W_BRANCH, HEAD_DIM),
                     hgrn_state=state_hgrn[l].reshape(bs * W_BRANCH, HEAD_DIM),
                     conv_p1=_expand_first_rows(state_ffn_conv[l], tsp, ((1, 0),)),
                     conv_p2=_expand_first_rows(state_ffn_conv[l], tsp, ((0, 0), (1, 1))))
        xs, st_s = _layer(xs, cfg_l, lw)
        new_p.append(st_p)
        new_s.append(st_s)

    def assemble(new, b, tt, tv, decode):
        def rows(name, shape):
            a = jnp.stack([n[name] for n in new]).reshape(depth, b, tt, -1)[:, :, :tv]
            return a.reshape((depth, b, tv) + shape)

        fox_kv = rows('fox_kv', (2, N_HEAD, HEAD_DIM))
        fox_logf = rows('fox_logf', (N_HEAD,))
        dsa_kv = rows('dsa_kv', (2, N_HEAD, HEAD_DIM))
        dsa_kidx = rows('dsa_kidx', (D_IDX,))
        rwkv = jnp.stack([n['rwkv'] for n in new]).reshape(depth, b, N_HEAD, HEAD_DIM, HEAD_DIM)
        hgrn = jnp.stack([n['hgrn'] for n in new]).reshape(depth, b, N_HEAD, HEAD_DIM, HEAD_DIM)
        shift = jnp.stack([n['shift'] for n in new]).reshape(depth, b, tt, W_GROUP)[:, :, tv - 1]
        conv = jnp.stack([n['conv'] for n in new])
        if decode:
            conv = conv.reshape(depth, b, tt, D_FF)[:, :, tv - 2:tv]
        else:
            conv = conv.reshape(depth, b, -1, 8, D_FF)[:, :, -1, 6:8]
        return fox_kv, fox_logf, dsa_kv, dsa_kidx, rwkv, shift, hgrn, conv

    y_p = xp.reshape(bp, tp, d)
    y_s = xs.reshape(bs, tsp, d)[:, :ts]
    return (y_p, y_s) + assemble(new_p, bp, tp, tp, False) + assemble(new_s, bs, tsp, ts, True)


def _pad_rows_nd(x, rows):
    pad = [(0, 0)] * x.ndim
    pad[-2] = (0, rows - x.shape[-2])
    return jnp.pad(x, pad)
```

```python
import functools
import math

import jax
import jax.numpy as jnp
import numpy as np
from jax import lax
from jax.experimental import pallas as pl
from jax.experimental.pallas import tpu as pltpu

F32 = jnp.float32
BF = jnp.bfloat16
I32 = jnp.int32
I16 = jnp.int16

D_MODEL = 1024
N_HEAD = 4
HEAD_DIM = 64
W_BRANCH = N_HEAD * HEAD_DIM
D_IDX = 32
W_IDX = N_HEAD * D_IDX
LORA_W = 64
LORA_A = 64
LORA_G = 128
D_FF = 2816
N_GROUP = 4
W_GROUP = 1024
DSA_TOPK = 256
ROPE_THETA = 500000.0
ROPE_HALF_QK = 8
ROPE_HALF_IDX = 4
RWKV_GN_EPS = 64e-5
RMS_EPS = 1e-6
LB_TINY = 1e-30
LN_EPS = 1e-5
NEG = -1e30
ROW_TILE = 256
KEY_CHUNK = 512
SELECT_SEQS = 16
DECODE_SEQS = 2
RWKV_CHUNK = 64
RWKV_LOCKSTEP = 4
HGRN_BLOCK = 16
HGRN_LOCKSTEP = 4
T_ALIGN = 8
INT_MIN = -2 ** 31


def _dot(a, b):
    return jnp.dot(a, b, preferred_element_type=F32)


def _dot_nt(a, b):
    return lax.dot_general(a, b, (((1,), (1,)), ((), ())), preferred_element_type=F32)


def _dot_tn(a, b):
    return lax.dot_general(a, b, (((0,), (0,)), ((), ())), preferred_element_type=F32)


def _split3(x):
    hi = x.astype(BF)
    r = x - hi.astype(F32)
    mid = r.astype(BF)
    lo = (r - mid.astype(F32)).astype(BF)
    return hi, mid, lo


def _dot_sel_l(sel_bf, x):
    hi, mid, lo = _split3(x)
    return _dot(sel_bf, hi) + _dot(sel_bf, mid) + _dot(sel_bf, lo)


def _dot_sel_r(x, sel_bf):
    hi, mid, lo = _split3(x)
    return _dot(hi, sel_bf) + _dot(mid, sel_bf) + _dot(lo, sel_bf)


def _dot_sel_tn(x, sel_bf):
    hi, mid, lo = _split3(x)
    return _dot_tn(hi, sel_bf) + _dot_tn(mid, sel_bf) + _dot_tn(lo, sel_bf)


def _iota(shape, dim):
    return lax.broadcasted_iota(I32, shape, dim)


def _tril(n, strict=False):
    r, c = _iota((n, n), 0), _iota((n, n), 1)
    return (r > c) if strict else (r >= c)


def _head_lane(width, head_width):
    return _iota((1, width), 1) // head_width


def _block_diag_mask(n, block):
    return (_iota((n, n), 0) // block) == (_iota((n, n), 1) // block)


def _stack_heads(x, head_width=HEAD_DIM):
    hl = _head_lane(x.shape[1], head_width)
    return jnp.concatenate([jnp.where(hl == h, x, jnp.zeros_like(x)) for h in range(N_HEAD)], axis=0)


def _unstack_heads(xs):
    c = xs.shape[0] // N_HEAD
    out = xs[0:c]
    for h in range(1, N_HEAD):
        out = out + xs[h * c:(h + 1) * c]
    return out


def _head_sum(x, ones_bd):
    return _dot_sel_r(x, ones_bd)


def _sigmoid(x):
    return 1.0 / (1.0 + jnp.exp(-x))


def _log_sigmoid(x):
    return jnp.minimum(x, 0.0) - jnp.log1p(jnp.exp(-jnp.abs(x)))


def _softplus(x):
    return jnp.maximum(x, 0.0) + jnp.log1p(jnp.exp(-jnp.abs(x)))


def _gelu(x):
    return 0.5 * x * (1.0 + lax.erf(x * (2.0 ** -0.5)))


def _layer_norm(x, g, b):
    mu = jnp.mean(x, axis=-1, keepdims=True)
    xc = x - mu
    var = jnp.mean(xc * xc, axis=-1, keepdims=True)
    return xc * lax.rsqrt(var + LN_EPS) * g + b


def _params(n_axes=1):
    return pltpu.CompilerParams(dimension_semantics=("arbitrary",) * n_axes)


def _const_spec(shape):
    nd = len(shape)
    return pl.BlockSpec(shape, lambda *_: (0,) * nd)


def _inproj_kernel(x_ref, w_ref, z_ref):
    z_ref[...] = _dot(x_ref[...].astype(BF), w_ref[...])


def _inproj(x, wz):
    n = x.shape[0]
    tm = min(ROW_TILE, n)
    nz = wz.shape[1]
    return pl.pallas_call(
        _inproj_kernel, grid=(n // tm,),
        in_specs=[pl.BlockSpec((tm, D_MODEL), lambda i: (i, 0)), _const_spec((D_MODEL, nz))],
        out_specs=pl.BlockSpec((tm, nz), lambda i: (i, 0)),
        out_shape=jax.ShapeDtypeStruct((n, nz), F32), compiler_params=_params(), name="inproj")(x, wz)


def _fox_prep_kernel(z_ref, bf_ref, qt_ref, k_ref, vt_ref, kv_ref, logf_ref, ccol_ref, crow_ref):
    t = z_ref.shape[0]
    ck = vt_ref.shape[2]
    tq = crow_ref.shape[2]
    kv_ref[...] = z_ref[:, 0:512]
    k_ref[...] = z_ref[:, 0:256].astype(BF)
    qt_ref[...] = (z_ref[:, 512:768] * (HEAD_DIM ** -0.5)).T.astype(BF)
    logf = _log_sigmoid(z_ref[:, 768:896] + bf_ref[...])
    logf_ref[...] = logf
    tri = _tril(ck).astype(BF)
    carry = jnp.zeros((1, 128), F32)
    for c in range(t // ck):
        rows = slice(c * ck, (c + 1) * ck)
        vt_ref[c] = z_ref[rows, 256:512].T.astype(BF)
        cum = _dot_sel_l(tri, logf[rows]) + carry
        carry = cum[ck - 1:ck]
        ccol_ref[rows, :] = cum
        cum_t = cum.T[0:8]
        for j in range(ck // tq):
            crow_ref[c * (ck // tq) + j] = cum_t[:, j * tq:(j + 1) * tq]


def _fox_prep(z, bf_row, b, t):
    ck = min(KEY_CHUNK, t)
    tq = min(ROW_TILE, t)
    nc, nq = t // ck, t // tq
    rows = lambda w: pl.BlockSpec((t, w), lambda i: (i, 0))
    return pl.pallas_call(
        _fox_prep_kernel, grid=(b,),
        in_specs=[rows(W_GROUP), _const_spec((1, 128))],
        out_specs=[pl.BlockSpec((W_BRANCH, t), lambda i: (0, i)), rows(W_BRANCH),
                   pl.BlockSpec((nc, W_BRANCH, ck), lambda i: (i, 0, 0)), rows(512), rows(128), rows(128),
                   pl.BlockSpec((None, nq, 8, tq), lambda i: (i, 0, 0, 0))],
        out_shape=[jax.ShapeDtypeStruct((W_BRANCH, b * t), BF), jax.ShapeDtypeStruct((b * t, W_BRANCH), BF),
                   jax.ShapeDtypeStruct((b * nc, W_BRANCH, ck), BF), jax.ShapeDtypeStruct((b * t, 512), F32),
                   jax.ShapeDtypeStruct((b * t, 128), F32), jax.ShapeDtypeStruct((b * t, 128), F32),
                   jax.ShapeDtypeStruct((b, nq, 8, tq), F32)],
        compiler_params=_params(), name="fox_prep")(z, bf_row)


def _flash_init(m_s, l_s, acc_s):
    m_s[...] = jnp.full(m_s.shape, NEG, F32)
    l_s[...] = jnp.zeros(l_s.shape, F32)
    acc_s[...] = jnp.zeros(acc_s.shape, F32)


def _flash_update(sts, vts, m_s, l_s, acc_s):
    scaled = []
    for h, st in enumerate(sts):
        m_old = m_s[h]
        m_new = jnp.maximum(m_old, jnp.max(st, axis=0, keepdims=True))
        a = jnp.exp(m_old - m_new)
        p = jnp.exp(st - m_new)
        l_s[h] = a * l_s[h] + jnp.sum(p, axis=0, keepdims=True)
        m_s[h] = m_new
        scaled.append((a, p.astype(BF)))
    for h, (a, p) in enumerate(scaled):
        acc_s[h] = a * acc_s[h] + _dot(vts[h], p)


def _flash_result(l_s, acc_s):
    return jnp.concatenate([acc_s[h] / l_s[h] for h in range(N_HEAD)], axis=0).T


def _flash_scratch(tq):
    return [pltpu.VMEM((N_HEAD, 1, tq), F32), pltpu.VMEM((N_HEAD, 1, tq), F32), pltpu.VMEM((N_HEAD, HEAD_DIM, tq), F32)]


def _head_rows(x, head_rows):
    rh = _iota((x.shape[0], 1), 0) // head_rows
    return [jnp.where(rh == h, x, jnp.zeros_like(x)) for h in range(N_HEAD)]


def _causal_chunks(i, tq, ck):
    return (i * tq) // ck + 1


def _fox_attn_kernel(qt_ref, k_ref, vt_ref, ccol_ref, crow_ref, o_ref, m_s, l_s, acc_s):
    tq = qt_ref.shape[1]
    ck = vt_ref.shape[2]
    i = pl.program_id(1)
    qpos = i * tq + _iota((1, tq), 1)
    qth = _head_rows(qt_ref[...], HEAD_DIM)
    cq = [crow_ref[h:h + 1, :] for h in range(N_HEAD)]
    _flash_init(m_s, l_s, acc_s)

    def chunk(c, carry):
        rows = pl.ds(pl.multiple_of(c * ck, ck), ck)
        k = k_ref[rows, :]
        ccol = ccol_ref[rows, :]
        visible = (c * ck + _iota((ck, 1), 0)) <= qpos
        sts = [jnp.where(visible, _dot(k, qth[h]) + cq[h] - ccol[:, h:h + 1], NEG) for h in range(N_HEAD)]
        vts = [vt_ref[c, h * HEAD_DIM:(h + 1) * HEAD_DIM, :] for h in range(N_HEAD)]
        _flash_update(sts, vts, m_s, l_s, acc_s)
        return carry

    lax.fori_loop(0, _causal_chunks(i, tq, ck), chunk, 0)
    o_ref[...] = _flash_result(l_s, acc_s)


def _fox_attn(qt, k, vt, ccol, crow, b, t):
    tq = crow.shape[3]
    nq = t // tq
    ck = vt.shape[2]
    nc = t // ck
    return pl.pallas_call(
        _fox_attn_kernel, grid=(b, nq),
        in_specs=[pl.BlockSpec((W_BRANCH, tq), lambda bi, i: (0, bi * nq + i)),
                  pl.BlockSpec((t, W_BRANCH), lambda bi, i: (bi, 0)),
                  pl.BlockSpec((nc, W_BRANCH, ck), lambda bi, i: (bi, 0, 0)),
                  pl.BlockSpec((t, 128), lambda bi, i: (bi, 0)),
                  pl.BlockSpec((None, None, 8, tq), lambda bi, i: (bi, i, 0, 0))],
        out_specs=pl.BlockSpec((tq, W_BRANCH), lambda bi, i: (bi * nq + i, 0)),
        out_shape=jax.ShapeDtypeStruct((b * t, W_BRANCH), F32), scratch_shapes=_flash_scratch(tq),
        compiler_params=_params(2), name="fox_attn")(qt, k, vt, ccol, crow)


def _page_specs(n_pages, block, layer, n_seq=1):
    nd = len(block)

    def mk(j, p):
        return pl.BlockSpec((None, None) + block, lambda g, pt: (layer, pt[g * n_seq + j, p]) + (0,) * nd)

    return [mk(j, p) for j in range(n_seq) for p in range(n_pages)]


def _softmax_rows(s):
    m = jnp.max(s, axis=1, keepdims=True)
    p = jnp.exp(s - m)
    return p.astype(BF), jnp.sum(p, axis=1, keepdims=True)


def _weighted_values(pb, l, vt_past, v_new, past):
    o = _dot_nt(pb[:, :past], vt_past) + _dot(pb[:, past:], v_new)
    return _unstack_heads(_stack_mask(o / l))


def _cat_pages(refs, lo, hi):
    return jnp.concatenate([r[lo:hi, :] for r in refs], axis=1)


def _stack_mask(o):
    c = o.shape[0] // N_HEAD
    hl = _head_lane(o.shape[1], HEAD_DIM)
    row_h = _iota((o.shape[0], 1), 0) // c
    return jnp.where(row_h == hl, o, 0.0)


def _pad_rows(x, rows):
    return jnp.concatenate([x, jnp.zeros((rows - x.shape[0], x.shape[1]), x.dtype)], axis=0)


def _fox_decode_kernel(pt_ref, z_ref, bf_ref, *refs, n_pages, n_seq):
    o_ref, logf_ref = refs[2 * n_seq * n_pages:]
    tt = z_ref.shape[0] // n_seq
    past = n_pages * refs[0].shape[1]
    parts = [_fox_decode_scores(z_ref, bf_ref, logf_ref, slice(j * tt, (j + 1) * tt),
                                refs[j * n_pages:(j + 1) * n_pages],
                                refs[(n_seq + j) * n_pages:(n_seq + j + 1) * n_pages]) for j in range(n_seq)]
    probs = [_softmax_rows(s) for s, _, _ in parts]
    for j, ((pb, l), (_, vt_past, v_new)) in enumerate(zip(probs, parts)):
        o_ref[j * tt:(j + 1) * tt, :] = _weighted_values(pb, l, vt_past, v_new, past)


def _fox_decode_scores(z_ref, bf_ref, logf_ref, rows, kv_refs, lf_refs):
    tt = rows.stop - rows.start
    n_pages = len(kv_refs)
    page = kv_refs[0].shape[1]
    k_new = z_ref[rows, 0:256]
    v_new = z_ref[rows, 256:512]
    q = z_ref[rows, 512:768] * (HEAD_DIM ** -0.5)
    logf = _log_sigmoid(z_ref[rows, 768:896] + bf_ref[...])
    logf_ref[rows, :] = logf

    lf = jnp.concatenate([r[...] for r in lf_refs], axis=0)
    n = lf.shape[0]
    tri_u = (_iota((page, page), 0) <= _iota((page, page), 1)).astype(BF)
    in_page = _dot_sel_r(lf, tri_u)
    tot = jnp.broadcast_to(in_page[:, page - 1:page], (n, page))
    r, c = _iota((n, n), 0), _iota((n, n), 1)
    later = jnp.where(((r % 8) == (c % 8)) & ((c // 8) >= (r // 8)), -1.0, 0.0).astype(BF)
    ck_rel = in_page + _dot_sel_l(later, tot)

    cn_col = _dot_sel_l(_tril(tt).astype(BF), logf)
    tri_pad = (_iota((tt, 128), 0) <= _iota((tt, 128), 1)).astype(BF)
    cn_row = _dot_sel_tn(logf, tri_pad)

    qbd = _stack_heads(q).astype(BF)
    s_past = _dot(qbd, _cat_pages(kv_refs, 0, W_BRANCH).astype(BF))
    vt_past = _cat_pages(kv_refs, W_BRANCH, 2 * W_BRANCH).astype(BF)
    s_new = _dot_nt(qbd, _pad_rows(k_new, 128).astype(BF))

    cq = jnp.concatenate([cn_col[:, h:h + 1] for h in range(N_HEAD)], axis=0)
    bias_past = jnp.concatenate(
        [jnp.concatenate([jnp.broadcast_to(ck_rel[p * 8 + h:p * 8 + h + 1], (tt, page)) for h in range(N_HEAD)], axis=0)
         for p in range(n_pages)], axis=1)
    bias_new = jnp.concatenate([jnp.broadcast_to(cn_row[h:h + 1], (tt, 128)) for h in range(N_HEAD)], axis=0)
    tq = _iota((N_HEAD * tt, 128), 0) % tt
    ok_new = _iota((N_HEAD * tt, 128), 1) <= tq
    s = jnp.concatenate([s_past + cq - bias_past, jnp.where(ok_new, s_new + cq - bias_new, NEG)], axis=1)
    return s, vt_past, _pad_rows(v_new, 128).astype(BF)


def _fox_decode(z, bf_row, cache_kv, cache_lf, page_table, layer, b, tt):
    n_pages = page_table.shape[1]
    page = cache_kv.shape[3]
    n_seq = min(DECODE_SEQS, b)
    kernel = functools.partial(_fox_decode_kernel, n_pages=n_pages, n_seq=n_seq)
    rows = lambda w: pl.BlockSpec((n_seq * tt, w), lambda i, pt: (i, 0))
    grid_spec = pltpu.PrefetchScalarGridSpec(
        num_scalar_prefetch=1, grid=(b // n_seq,),
        in_specs=[rows(W_GROUP), pl.BlockSpec((1, 128), lambda i, pt: (0, 0))]
        + _page_specs(n_pages, (2 * W_BRANCH, page), layer, n_seq) + _page_specs(n_pages, (8, page), layer, n_seq),
        out_specs=[rows(W_BRANCH), rows(128)])
    return pl.pallas_call(
        kernel, grid_spec=grid_spec,
        out_shape=[jax.ShapeDtypeStruct((b * tt, W_BRANCH), F32), jax.ShapeDtypeStruct((b * tt, 128), F32)],
        compiler_params=_params(), name="fox_decode")(page_table, z, bf_row, *([cache_kv] * (n_seq * n_pages)),
                                                      *([cache_lf] * (n_seq * n_pages)))


def _rope_tables(pos, n_lanes, head_width, half):
    inv = jnp.power(ROPE_THETA, -jnp.arange(half, dtype=F32) / half)
    ang = pos.astype(F32)[:, None] * inv[None]
    d = np.arange(n_lanes) % head_width
    first = jnp.asarray(d < half)[None]
    second = jnp.asarray((d >= half) & (d < 2 * half))[None]
    cos = jnp.cos(ang)[:, d % half]
    sin = jnp.sin(ang)[:, d % half]
    c = jnp.where(first | second, cos, 1.0)
    s_up = jnp.where(first, -sin, 0.0)
    s_dn = jnp.where(second, sin, 0.0)
    return jnp.concatenate([c, s_up, s_dn], axis=1).astype(F32)


def _rope(x, tab, half):
    w = x.shape[1]
    return x * tab[:, 0:w] + pltpu.roll(x, w - half, 1) * tab[:, w:2 * w] + pltpu.roll(x, half, 1) * tab[:, 2 * w:3 * w]


def _dsa_prep_kernel(z_ref, tk_ref, ti_ref, kv_ref, misc_ref, *out_refs, prefill):
    tk = tk_ref[...]
    ti = ti_ref[...]
    k = _rope(z_ref[:, 0:256], tk, ROPE_HALF_QK)
    v = z_ref[:, 256:512]
    q = _rope(z_ref[:, 512:768], tk, ROPE_HALF_QK) * (HEAD_DIM ** -0.5)
    qi = _rope(z_ref[:, 768:896], ti, ROPE_HALF_IDX)
    tail = z_ref[:, 896:1024]
    ki = _rope(tail, ti, ROPE_HALF_IDX)
    kv_ref[...] = jnp.concatenate([k, v], axis=1)
    lane = _iota((1, 128), 1)
    misc = jnp.where(lane < D_IDX, ki, tail * (N_HEAD ** -0.5))
    misc_ref[...] = misc
    if prefill:
        qt_ref, k_ref, vt_ref, qit_ref, wt_ref, ki4_ref = out_refs
        qt_ref[...] = q.T.astype(BF)
        k_ref[...] = k.astype(BF)
        vt_ref[...] = v.T.astype(BF)
        qit_ref[...] = qi.T.astype(BF)
        wt_ref[...] = misc.T[D_IDX:D_IDX + 8]
        kim = jnp.where(lane < D_IDX, ki, 0.0)
        ki4 = kim + pltpu.roll(kim, 32, 1) + pltpu.roll(kim, 64, 1) + pltpu.roll(kim, 96, 1)
        ki4_ref[...] = ki4.astype(BF)
    else:
        q_ref, qi_ref = out_refs
        q_ref[...] = q
        qi_ref[...] = qi


def _dsa_prep(z, tab_k, tab_i, prefill):
    n = z.shape[0]
    tm = min(ROW_TILE, n)
    nt = tab_k.shape[0] // tm
    row = lambda w: pl.BlockSpec((tm, w), lambda i: (i, 0))
    col = lambda w: pl.BlockSpec((w, tm), lambda i: (0, i))
    out_specs = [row(512), row(128)]
    out_shape = [jax.ShapeDtypeStruct((n, 512), F32), jax.ShapeDtypeStruct((n, 128), F32)]
    if prefill:
        ck = 2 * tm
        out_specs += [col(256), row(256), pl.BlockSpec((None, W_BRANCH, tm), lambda i: (i // 2, 0, i % 2)),
                      col(128), col(8), row(128)]
        out_shape += [jax.ShapeDtypeStruct((256, n), BF), jax.ShapeDtypeStruct((n, 256), BF),
                      jax.ShapeDtypeStruct((n // ck, W_BRANCH, ck), BF), jax.ShapeDtypeStruct((128, n), BF),
                      jax.ShapeDtypeStruct((8, n), F32), jax.ShapeDtypeStruct((n, 128), BF)]
    else:
        out_specs += [row(256), row(128)]
        out_shape += [jax.ShapeDtypeStruct((n, 256), F32), jax.ShapeDtypeStruct((n, 128), F32)]
    return pl.pallas_call(
        functools.partial(_dsa_prep_kernel, prefill=prefill), grid=(n // tm,),
        in_specs=[pl.BlockSpec((tm, W_GROUP), lambda i: (i, 1)),
                  pl.BlockSpec((tm, 768), lambda i: (i % nt, 0)), pl.BlockSpec((tm, 384), lambda i: (i % nt, 0))],
        out_specs=out_specs, out_shape=out_shape, compiler_params=_params(), name="dsa_prep")(z, tab_k, tab_i)


def _sortable(x):
    b = pltpu.bitcast(x, I32)
    return b ^ ((b >> 31) & I32(0x7FFFFFFF))


def _topk_mask(score, idx, topk, n_idx_bits):
    key = _sortable(score)
    rows = score.shape[0]

    def count(pred):
        return jnp.sum(pred.astype(I32), axis=1, keepdims=True)

    def value_step(it, ans):
        cand = ans + jnp.left_shift(I32(1), I32(31) - it)
        return jnp.where(count(key >= cand) >= topk, cand, ans)

    thr = lax.fori_loop(0, 32, value_step, jnp.full((rows, 1), INT_MIN, I32))
    above = key > thr
    tie = key == thr
    need = topk - count(above)

    def index_step(it, lo):
        cand = lo + jnp.left_shift(I32(1), I32(n_idx_bits - 1) - it)
        return jnp.where(count(tie & (idx < cand)) < need, cand, lo)

    last = lax.fori_loop(0, n_idx_bits, index_step, jnp.zeros((rows, 1), I32))
    return above | (tie & (idx <= last))


def _dsa_attn_kernel(qt_ref, qit_ref, wt_ref, k_ref, vt_ref, ki4_ref, o_ref, hi_s, lo_s, tie_s, keep_s, m_s, l_s, acc_s, *,
                     topk, n_idx_bits):
    tq = qt_ref.shape[1]
    ck = hi_s.shape[1]
    i = pl.program_id(1)
    nc = _causal_chunks(i, tq, ck)
    qpos = i * tq + _iota((1, tq), 1)
    qpos16 = qpos.astype(I16)
    qith = _head_rows(qit_ref[...], D_IDX)
    w_row = [wt_ref[h:h + 1, :] for h in range(N_HEAD)]
    i16_min, i16_max = I16(-32768), I16(32767)

    def key_pos(c):
        return c * ck + _iota((ck, tq), 0)

    def score_chunk(c, carry):
        ki4 = ki4_ref[pl.ds(pl.multiple_of(c * ck, ck), ck), :]
        sc = jnp.zeros((ck, tq), F32)
        for h in range(N_HEAD):
            sc = sc + jnp.maximum(_dot(ki4, qith[h]) * (D_IDX ** -0.5), 0.0) * w_row[h]
        sc = jnp.where(key_pos(c) <= qpos, sc, NEG)
        key = _sortable(jnp.where(sc == 0.0, 0.0, sc))
        hi_s[c] = (key >> 16).astype(I16)
        lo_s[c] = ((key & I32(0xFFFF)) - I32(32768)).astype(I16)
        return carry

    lax.fori_loop(0, nc, score_chunk, 0)

    def count(pred):
        def body(c, acc):
            ones = jnp.where(pred(c), I16(1), I16(0))
            for j in range(ck // 16):
                acc = acc + ones[j * 16:(j + 1) * 16]
            return acc
        acc = lax.fori_loop(0, nc, body, jnp.zeros((16, tq), I16))
        return jnp.sum(acc.astype(I32), axis=0, keepdims=True)

    def search16(ref, want):
        def step(it, ans):
            cand = ans + jnp.left_shift(I32(1), I32(15) - it)
            c16 = cand.astype(I16)
            return jnp.where(count(lambda c: ref[c] >= c16) >= want, cand, ans)
        return lax.fori_loop(0, 16, step, jnp.full((1, tq), -32768, I32)).astype(I16)

    h16 = search16(hi_s, topk)
    n_above = count(lambda c: hi_s[c] > h16)

    def mask_lo(c, carry):
        lo_s[c] = jnp.where(hi_s[c] == h16, lo_s[c], i16_min)
        return carry

    lax.fori_loop(0, nc, mask_lo, 0)
    l16 = search16(lo_s, topk - n_above)
    need = topk - n_above - count(lambda c: lo_s[c] > l16)

    def tie_chunk(c, carry):
        kpos16 = key_pos(c).astype(I16)
        tie = (hi_s[c] == h16) & (lo_s[c] == l16) & (kpos16 <= qpos16)
        tie_s[c] = jnp.where(tie, kpos16, i16_max)
        return carry

    lax.fori_loop(0, nc, tie_chunk, 0)

    def index_step(it, lo):
        cand = lo + jnp.left_shift(I32(1), I32(n_idx_bits - 1) - it)
        c16 = cand.astype(I16)
        return jnp.where(count(lambda c: tie_s[c] < c16) < need, cand, lo)

    last16 = lax.fori_loop(0, n_idx_bits, index_step, jnp.zeros((1, tq), I32)).astype(I16)

    def keep_chunk(c, carry):
        kpos16 = key_pos(c).astype(I16)
        sel = jnp.where(hi_s[c] > h16, I16(1), I16(0)) + jnp.where(lo_s[c] > l16, I16(1), I16(0)) \
            + jnp.where(tie_s[c] <= last16, I16(1), I16(0))
        sel = jnp.where(kpos16 <= qpos16, sel, I16(0))
        keep_s[c] = jnp.where(sel.astype(I32) > 0, 0.0, NEG)
        return carry

    lax.fori_loop(0, nc, keep_chunk, 0)

    qth = _head_rows(qt_ref[...], HEAD_DIM)
    _flash_init(m_s, l_s, acc_s)

    def attend_chunk(c, carry):
        k = k_ref[pl.ds(pl.multiple_of(c * ck, ck), ck), :]
        keep = keep_s[c]
        sts = [_dot(k, qth[h]) + keep for h in range(N_HEAD)]
        vts = [vt_ref[c, h * HEAD_DIM:(h + 1) * HEAD_DIM, :] for h in range(N_HEAD)]
        _flash_update(sts, vts, m_s, l_s, acc_s)
        return carry

    lax.fori_loop(0, nc, attend_chunk, 0)
    o_ref[...] = _flash_result(l_s, acc_s)


def _dsa_attn(qt, k, vt, qit, wt, ki4, b, t, topk):
    tq = min(ROW_TILE, t)
    ck = vt.shape[2]
    nq, nc = t // tq, t // ck
    assert topk <= ck and t < 2 ** 15
    qcol = lambda w: pl.BlockSpec((w, tq), lambda bi, i: (0, bi * nq + i))
    seq = lambda w: pl.BlockSpec((t, w), lambda bi, i: (bi, 0))
    kernel = functools.partial(_dsa_attn_kernel, topk=topk, n_idx_bits=max(1, (t - 1).bit_length()))
    scratch = [pltpu.VMEM((nc, ck, tq), I16)] * 3 + [pltpu.VMEM((nc, ck, tq), F32)] + _flash_scratch(tq)
    return pl.pallas_call(
        kernel, grid=(b, nq),
        in_specs=[qcol(256), qcol(128), qcol(8), seq(256), pl.BlockSpec((nc, W_BRANCH, ck), lambda bi, i: (bi, 0, 0)),
                  seq(128)],
        out_specs=pl.BlockSpec((tq, W_BRANCH), lambda bi, i: (bi * nq + i, 0)),
        out_shape=jax.ShapeDtypeStruct((b * t, W_BRANCH), F32), scratch_shapes=scratch,
        compiler_params=_params(2), name="dsa_attn")(qt, qit, wt, k, vt, ki4)


def _dsa_select_kernel(pt_ref, qi_ref, misc_ref, *refs, n_pages, n_seq, tt, topk, t_valid):
    ki_refs = refs[:n_seq * n_pages]
    keep_ref = refs[n_seq * n_pages]
    page = ki_refs[0].shape[1]
    past = n_pages * page
    scores = []
    for j in range(n_seq):
        rows = slice(j * tt, (j + 1) * tt)
        qi = qi_ref[rows, :]
        qi_h = jnp.concatenate([qi[:, h * D_IDX:(h + 1) * D_IDX] for h in range(N_HEAD)], axis=0).astype(BF)
        w_col = jnp.concatenate([misc_ref[rows, D_IDX + h:D_IDX + h + 1] for h in range(N_HEAD)], axis=0)
        kit_past = _cat_pages(ki_refs[j * n_pages:(j + 1) * n_pages], 0, D_IDX).astype(BF)
        ki_new = _pad_rows(misc_ref[rows, 0:D_IDX], 128).astype(BF)
        s = jnp.concatenate([_dot(qi_h, kit_past), _dot_nt(qi_h, ki_new)], axis=1)
        scores.append(_unstack_heads(jnp.maximum(s * (D_IDX ** -0.5), 0.0) * w_col))
    score = jnp.concatenate(scores, axis=0)
    n_keys = past + 128
    idx = _iota((1, n_keys), 1)
    new_t = idx - past
    t_q = _iota((n_seq * tt, 1), 0) % tt
    visible = (new_t <= t_q) & (new_t < t_valid)
    score = jnp.where(visible, jnp.where(score == 0.0, 0.0, score), NEG)
    sel = _topk_mask(score, idx, topk, n_keys.bit_length()) & visible
    keep_ref[...] = jnp.where(sel, 0.0, NEG)


def _dsa_select(qi, misc, cache_ki, page_table, layer, b, tt, topk, t_valid):
    n_pages = page_table.shape[1]
    page = cache_ki.shape[3]
    n_seq = min(SELECT_SEQS, b)
    n_keys = n_pages * page + 128
    kernel = functools.partial(_dsa_select_kernel, n_pages=n_pages, n_seq=n_seq, tt=tt, topk=topk, t_valid=t_valid)

    def page_spec(j, p):
        return pl.BlockSpec((None, None, D_IDX, page), lambda g, pt: (layer, pt[g * n_seq + j, p], 0, 0))

    rows = lambda w: pl.BlockSpec((n_seq * tt, w), lambda g, pt: (g, 0))
    grid_spec = pltpu.PrefetchScalarGridSpec(
        num_scalar_prefetch=1, grid=(b // n_seq,),
        in_specs=[rows(128), rows(128)] + [page_spec(j, p) for j in range(n_seq) for p in range(n_pages)],
        out_specs=rows(n_keys))
    return pl.pallas_call(
        kernel, grid_spec=grid_spec, out_shape=jax.ShapeDtypeStruct((b * tt, n_keys), F32),
        compiler_params=_params(), name="dsa_select")(page_table, qi, misc, *([cache_ki] * (n_seq * n_pages)))


def _dsa_decode_kernel(pt_ref, q_ref, keep_ref, kvn_ref, *refs, n_pages, n_seq):
    o_ref = refs[n_seq * n_pages]
    tt = q_ref.shape[0] // n_seq
    past = n_pages * refs[0].shape[1]

    def scores(j):
        rows = slice(j * tt, (j + 1) * tt)
        kv_refs = refs[j * n_pages:(j + 1) * n_pages]
        qbd = _stack_heads(q_ref[rows, :]).astype(BF)
        k_new = _pad_rows(kvn_ref[rows, 0:256], 128).astype(BF)
        v_new = _pad_rows(kvn_ref[rows, 256:512], 128).astype(BF)
        s = jnp.concatenate([_dot(qbd, _cat_pages(kv_refs, 0, W_BRANCH).astype(BF)), _dot_nt(qbd, k_new)], axis=1)
        keep4 = jnp.concatenate([keep_ref[rows, :]] * N_HEAD, axis=0)
        return jnp.where(keep4 == 0.0, s, NEG), _cat_pages(kv_refs, W_BRANCH, 2 * W_BRANCH).astype(BF), v_new

    parts = [scores(j) for j in range(n_seq)]
    probs = [_softmax_rows(s) for s, _, _ in parts]
    for j, ((pb, l), (_, vt_past, v_new)) in enumerate(zip(probs, parts)):
        o_ref[j * tt:(j + 1) * tt, :] = _weighted_values(pb, l, vt_past, v_new, past)


def _dsa_decode(q, keep, kv_new, cache_kv, page_table, layer, b, tt):
    n_pages = page_table.shape[1]
    page = cache_kv.shape[3]
    n_seq = min(DECODE_SEQS, b)
    kernel = functools.partial(_dsa_decode_kernel, n_pages=n_pages, n_seq=n_seq)
    row = lambda w: pl.BlockSpec((n_seq * tt, w), lambda i, pt: (i, 0))
    grid_spec = pltpu.PrefetchScalarGridSpec(
        num_scalar_prefetch=1, grid=(b // n_seq,),
        in_specs=[row(256), row(keep.shape[1]), row(512)] + _page_specs(n_pages, (2 * W_BRANCH, page), layer, n_seq),
        out_specs=row(256))
    return pl.pallas_call(
        kernel, grid_spec=grid_spec, out_shape=jax.ShapeDtypeStruct((b * tt, W_BRANCH), F32),
        compiler_params=_params(), name="dsa_decode")(page_table, q, keep, kv_new, *([cache_kv] * (n_seq * n_pages)))


def _rwkv_chunks(chunks, n_double):
    c = chunks[0][0].shape[0]
    cc = N_HEAD * c
    tri = _tril(c).astype(BF)
    strict = _tril(cc, strict=True)
    incl = _tril(cc)
    each = lambda fn, *lists: [fn(*xs) for xs in zip(*lists)]
    r, k, v, lw, kap, beta = (list(x) for x in zip(*chunks))
    g = each(lambda x: _dot_sel_l(tri, x), lw)
    g_end = [x[c - 1:c] for x in g]
    e_neg = each(lambda x: jnp.exp(-x), g)
    e_end = each(lambda x, y: jnp.exp(y - x), g, g_end)
    a_f = each(lambda kp, x, l: _stack_heads(kp * jnp.exp(x - l)), kap, g, lw)
    a_s = each(lambda x: x.astype(BF), a_f)
    r_s = each(lambda x, y: _stack_heads(x * jnp.exp(y)), r, g)
    r_sb = each(lambda x: x.astype(BF), r_s)
    bb_s = each(lambda x, e: _stack_heads(x * e).astype(BF), beta, e_neg)
    bk_s = each(lambda x, e: _stack_heads(x * e).astype(BF), k, e_neg)
    v_s = each(lambda x: _stack_heads(x).astype(BF), v)
    kh_s = each(lambda x, e: _stack_heads(x * e).astype(BF), k, e_end)
    bh_s = each(lambda x, e: _stack_heads(x * e).astype(BF), beta, e_end)
    l_b = each(lambda x, y: jnp.where(strict, _dot_nt(x, y), 0.0), a_s, bb_s)
    l_k = each(lambda x, y: jnp.where(strict, _dot_nt(x, y), 0.0).astype(BF), a_s, bk_s)
    w_b = each(lambda x, y: jnp.where(incl, _dot_nt(x, y), 0.0).astype(BF), r_sb, bb_s)
    w_k = each(lambda x, y: jnp.where(incl, _dot_nt(x, y), 0.0).astype(BF), r_sb, bk_s)
    y = each(lambda x: -x, l_b)
    n = y
    for _ in range(n_double):
        y = each(lambda x: _dot(x.astype(BF), x.astype(BF)), y)
        n = each(lambda p, q: p + q + _dot(p.astype(BF), q.astype(BF)), n, y)
    nb = each(lambda x: x.astype(BF), n)
    a_t = each(lambda f, p, q: (f + _dot(p, q)).astype(BF), a_f, nb, a_s)
    lkv = each(_dot, l_k, v_s)
    u0 = each(lambda x, p: (x + _dot(p, x.astype(BF))).astype(BF), lkv, nb)
    r_hat = each(lambda x, w, a: (x - _dot(w, a)).astype(BF), r_s, w_b, a_t)
    y0 = each(lambda wk, vs, wb, u: _dot(wk, vs) - _dot(wb, u), w_k, v_s, w_b, u0)
    h_mat = each(lambda a, b: _dot_tn(a, b).astype(BF), a_t, bh_s)
    s_add = each(lambda vs, kh, u, bh: _dot_tn(vs, kh) - _dot_tn(u, bh), v_s, kh_s, u0, bh_s)
    decay = each(jnp.exp, g_end)
    return list(zip(r_hat, y0, decay, h_mat, s_add))


def _rwkv_apply(par, s_big):
    r_hat, y0, decay, h_mat, s_add = par
    sb = s_big.astype(BF)
    ys = _dot_nt(r_hat, sb) + y0
    s_new = s_big * decay - _dot(sb, h_mat) + s_add
    return _unstack_heads(ys), s_new


def _to_block_diag(x):
    return jnp.where(_block_diag_mask(W_BRANCH, HEAD_DIM), jnp.concatenate([x] * N_HEAD, axis=1), 0.0)


def _from_block_diag(x):
    y = x + pltpu.roll(x, 64, 1) + pltpu.roll(x, 128, 1) + pltpu.roll(x, 192, 1)
    return y[:, 0:HEAD_DIM]


def _rwkv_kernel(*refs, seq_len, t_valid, chunk, n_double, lockstep, has_state):
    if has_state:
        (z_ref, prev_ref, s_in_ref, mu_ref, vec_ref, wa_ref, g2_ref, o_ref, s_out_ref,
         r_s, k_s, v_s, lw_s, kap_s, beta_s, y_s) = refs
    else:
        (z_ref, mu_ref, vec_ref, wa_ref, g2_ref, o_ref, s_out_ref,
         r_s, k_s, v_s, lw_s, kap_s, beta_s, y_s, state_s, last_s) = refs
    rows = z_ref.shape[0]
    i = pl.program_id(0)
    t_row = (i * rows + _iota((rows, 1), 0)) % seq_len
    pr = z_ref[...]
    shifted = pltpu.roll(pr, 1, 0)
    if has_state:
        prev = jnp.where(t_row == 0, prev_ref[...], shifted)
    else:
        first = (i * rows) % seq_len == 0

        @pl.when(first)
        def _():
            last_s[...] = jnp.zeros_like(last_s)
            state_s[...] = jnp.zeros_like(state_s)

        prev = jnp.where(_iota((rows, 1), 0) == 0, last_s[7:8, :], shifted)
        last_s[...] = pr[rows - 8:rows]
    xs = pr + (prev - pr) * mu_ref[...]
    r, k, v = xs[:, 0:256], xs[:, 256:512], xs[:, 512:768]
    lora = xs[:, 768:896]
    lora = jnp.where(_iota((1, 128), 1) < LORA_W, jnp.tanh(lora), lora)
    wa = _dot(lora.astype(BF), wa_ref[...])
    w0, a0, kk_p, ka_p = vec_ref[0:1], vec_ref[1:2], vec_ref[2:3], vec_ref[3:4]
    rk_p, ln_w, ln_b = vec_ref[4:5], vec_ref[5:6], vec_ref[6:7]
    w_log = -_softplus(-(w0 + wa[:, 0:256])) - 0.5
    lw = -jnp.exp(w_log)
    a = _sigmoid(a0 + wa[:, 256:512])
    gate = _dot(_sigmoid(xs[:, 896:1024]).astype(BF), g2_ref[...])
    ones_bd = _block_diag_mask(W_BRANCH, HEAD_DIM).astype(BF)
    kk = k * kk_p
    kap = kk / jnp.maximum(jnp.sqrt(_head_sum(kk * kk, ones_bd)), 1e-12)
    k2 = k * (1.0 + (a - 1.0) * ka_p)
    bonus = _head_sum(r * k2 * rk_p, ones_bd) * v
    live = t_row < t_valid
    r_s[...] = r
    k_s[...] = jnp.where(live, k2, 0.0)
    v_s[...] = jnp.where(live, v, 0.0)
    lw_s[...] = jnp.where(live, lw, 0.0)
    kap_s[...] = jnp.where(live, kap, 0.0)
    beta_s[...] = jnp.where(live, a * kap, 0.0)

    def group(gi, carry):
        def rows_of(j, size):
            return pl.ds(pl.multiple_of((gi * lockstep + j) * size, size), size)

        sls = [rows_of(j, chunk) for j in range(lockstep)]
        pars = _rwkv_chunks([(r_s[sl, :], k_s[sl, :], v_s[sl, :], lw_s[sl, :], kap_s[sl, :], beta_s[sl, :]) for sl in sls],
                            n_double)
        if has_state:
            for j, (sl, par) in enumerate(zip(sls, pars)):
                st = rows_of(j, W_BRANCH)
                y, s_new = _rwkv_apply(par, _to_block_diag(s_in_ref[st, :]))
                y_s[sl, :] = y
                s_out_ref[st, :] = _from_block_diag(s_new)
        else:
            s_big = state_s[...]
            for sl, par in zip(sls, pars):
                y, s_big = _rwkv_apply(par, s_big)
                y_s[sl, :] = y
            state_s[...] = s_big
        return carry

    lax.fori_loop(0, rows // (chunk * lockstep), group, 0)
    if not has_state:
        s_out_ref[...] = _from_block_diag(state_s[...])
    y = y_s[...]
    mu = _head_sum(y, ones_bd) * (1.0 / HEAD_DIM)
    yc = y - mu
    var = _head_sum(yc * yc, ones_bd) * (1.0 / HEAD_DIM)
    yn = yc * lax.rsqrt(var + RWKV_GN_EPS) * ln_w + ln_b
    o_ref[...] = (yn + bonus) * gate


def _rwkv(z, prev_rows, state_in, mu, vec, wa, g2, b, seq_len, t_valid):
    n = z.shape[0]
    has_state = state_in is not None
    rows = min(ROW_TILE, n)
    chunk = min(RWKV_CHUNK, seq_len)
    n_double = max(0, int(math.log2(chunk)) - 1)
    steps = n // rows
    seq_per_tile = max(1, rows // seq_len)
    tiles_per_seq = max(1, seq_len // rows)
    lockstep = min(RWKV_LOCKSTEP, rows // chunk)
    kernel = functools.partial(_rwkv_kernel, seq_len=seq_len, t_valid=t_valid, chunk=chunk, n_double=n_double,
                               lockstep=lockstep, has_state=has_state)
    consts = [_const_spec((1, W_GROUP)), _const_spec((8, W_BRANCH)), _const_spec((128, 512)), _const_spec((128, W_BRANCH))]
    zspec = pl.BlockSpec((rows, W_GROUP), lambda i: (i, 2))
    scratch = [pltpu.VMEM((rows, W_BRANCH), F32)] * 7
    if has_state:
        srows = seq_per_tile * W_BRANCH
        in_specs = [zspec, pl.BlockSpec((rows, W_GROUP), lambda i: (i, 0)), pl.BlockSpec((srows, HEAD_DIM), lambda i: (i, 0))] + consts
        s_spec = pl.BlockSpec((srows, HEAD_DIM), lambda i: (i, 0))
        args = (z, prev_rows, state_in, mu, vec, wa, g2)
    else:
        in_specs = [zspec] + consts
        s_spec = pl.BlockSpec((W_BRANCH, HEAD_DIM), lambda i: (i // tiles_per_seq, 0))
        scratch = scratch + [pltpu.VMEM((W_BRANCH, W_BRANCH), F32), pltpu.VMEM((8, W_GROUP), F32)]
        args = (z, mu, vec, wa, g2)
    return pl.pallas_call(
        kernel, grid=(steps,), in_specs=in_specs,
        out_specs=[pl.BlockSpec((rows, W_BRANCH), lambda i: (i, 0)), s_spec],
        out_shape=[jax.ShapeDtypeStruct((n, W_BRANCH), F32), jax.ShapeDtypeStruct((b * W_BRANCH, HEAD_DIM), F32)],
        scratch_shapes=scratch, compiler_params=_params(), name="rwkv")(*args)


def _hgrn_lb_kernel(x_ref, o_ref):
    x = x_ref[...]
    depth = x.shape[0]
    e = jnp.exp(x - jnp.max(x, axis=0, keepdims=True))
    soft = e / jnp.sum(e, axis=0, keepdims=True)
    cum = jnp.zeros((1, x.shape[1]), F32)
    for l in range(depth):
        cum = cum + soft[l:l + 1]
        lb = jnp.maximum(cum - soft[0:1], 0.0)
        o_ref[l, 0:1, :] = lb
        o_ref[l, 1:2, :] = jnp.log(jnp.maximum(lb, LB_TINY))
        o_ref[l, 2:3, :] = jnp.log1p(-lb)
        o_ref[l, 3:8, :] = jnp.zeros((5, x.shape[1]), F32)


def _hgrn_lb(hgrn_lb):
    depth, w = hgrn_lb.shape
    return pl.pallas_call(_hgrn_lb_kernel, out_shape=jax.ShapeDtypeStruct((depth, 8, w), F32), name="hgrn_lb")(hgrn_lb)


def _hgrn_blocks(blocks, ones_bd, bd_mask):
    c = blocks[0][0].shape[0]
    tri = _tril(c).astype(BF)
    t_idx = _iota((c, 1), 0)
    each = lambda fn, *lists: [fn(*xs) for xs in zip(*lists)]
    q, k, v, g = (list(x) for x in zip(*blocks))
    b = each(lambda x: _dot_sel_l(tri, x), g)
    b_end = [x[c - 1:c] for x in b]

    def pair_terms(qq, kk, bb):
        xs = []
        for s in range(c):
            e = jnp.exp(jnp.where(t_idx >= s, bb - bb[s:s + 1], NEG))
            xs.append((qq * e * kk[s:s + 1]).astype(BF))
        return jnp.concatenate(xs, axis=0)

    col = each(lambda qq, kk, bb: _dot(pair_terms(qq, kk, bb), ones_bd), q, k, b)

    def inside(cl, vv):
        o = cl[0:c] * vv[0:1]
        for s in range(1, c):
            o = o + cl[s * c:(s + 1) * c] * vv[s:s + 1]
        return o

    o_in = each(inside, col, v)
    qd = each(lambda qq, bb: (qq * jnp.exp(bb)).astype(BF), q, b)
    upd = each(lambda vv, kk, bb, be: jnp.where(bd_mask, _dot_tn(vv.astype(BF), (kk * jnp.exp(be - bb)).astype(BF)), 0.0),
               v, k, b, b_end)
    decay = each(jnp.exp, b_end)
    return list(zip(qd, o_in, decay, upd))


def _hgrn_apply(par, st):
    qd, o_in, decay, upd = par
    return _dot_nt(qd, st.astype(BF)) + o_in, st * decay + upd


def _hgrn_kernel(*refs, seq_len, t_valid, block, has_state):
    if has_state:
        z_ref, s_in_ref, lb_ref, nw_ref, o_ref, s_out_ref, q_s, k_s, v_s, g_s, y_s = refs
    else:
        z_ref, lb_ref, nw_ref, o_ref, s_out_ref, q_s, k_s, v_s, g_s, y_s, state_s = refs
    rows = z_ref.shape[0]
    i = pl.program_id(0)
    t_row = (i * rows + _iota((rows, 1), 0)) % seq_len
    live = t_row < t_valid
    hq, hf, hi, hg = z_ref[:, 0:256], z_ref[:, 256:512], z_ref[:, 512:768], z_ref[:, 768:1024]
    lb, lb_log, l1m = lb_ref[0:1], lb_ref[1:2], lb_ref[2:3]
    ls = _log_sigmoid(hf)
    x2 = l1m + ls
    lae = jnp.maximum(lb_log, x2) + jnp.log1p(jnp.exp(-jnp.abs(lb_log - x2)))
    logf = jnp.where(lb > 0.0, lae, ls)
    q_s[...] = hq * _sigmoid(hq)
    k_s[...] = jnp.where(live, (1.0 - lb) * _sigmoid(-hf), 0.0)
    v_s[...] = hi
    g_s[...] = jnp.where(live, logf, 0.0)
    ones_bd = _block_diag_mask(W_BRANCH, HEAD_DIM).astype(BF)
    bd_mask = _block_diag_mask(W_BRANCH, HEAD_DIM)

    if not has_state:
        @pl.when((i * rows) % seq_len == 0)
        def _():
            state_s[...] = jnp.zeros_like(state_s)

    lockstep = min(HGRN_LOCKSTEP, rows // block)

    def group(gi, carry):
        def rows_of(j, size):
            return pl.ds(pl.multiple_of((gi * lockstep + j) * size, size), size)

        sls = [rows_of(j, block) for j in range(lockstep)]
        pars = _hgrn_blocks([(q_s[sl, :], k_s[sl, :], v_s[sl, :], g_s[sl, :]) for sl in sls], ones_bd, bd_mask)
        if has_state:
            for j, (sl, par) in enumerate(zip(sls, pars)):
                sr = rows_of(j, W_BRANCH)
                o, st_new = _hgrn_apply(par, _to_block_diag(s_in_ref[sr, :]).T)
                y_s[sl, :] = o
                s_out_ref[sr, :] = _from_block_diag(st_new.T)
        else:
            st = state_s[...]
            for sl, par in zip(sls, pars):
                o, st = _hgrn_apply(par, st)
                y_s[sl, :] = o
            state_s[...] = st
        return carry

    lax.fori_loop(0, rows // (block * lockstep), group, 0)
    if not has_state:
        s_out_ref[...] = _from_block_diag(state_s[...].T)
    o = y_s[...]
    ms = _head_sum(o * o, ones_bd) * (1.0 / HEAD_DIM)
    o_ref[...] = o * lax.rsqrt(ms + RMS_EPS) * nw_ref[...] * (hg * _sigmoid(hg))


def _hgrn(z, state_in, lb_rows, norm_w, b, seq_len, t_valid):
    n = z.shape[0]
    has_state = state_in is not None
    rows = min(ROW_TILE, n)
    block = min(HGRN_BLOCK, seq_len)
    seq_per_tile = max(1, rows // seq_len)
    tiles_per_seq = max(1, seq_len // rows)
    kernel = functools.partial(_hgrn_kernel, seq_len=seq_len, t_valid=t_valid, block=block, has_state=has_state)
    zspec = pl.BlockSpec((rows, W_GROUP), lambda i: (i, 3))
    consts = [_const_spec((8, W_BRANCH)), _const_spec((1, W_BRANCH))]
    scratch = [pltpu.VMEM((rows, W_BRANCH), F32)] * 5
    if has_state:
        srows = seq_per_tile * W_BRANCH
        in_specs = [zspec, pl.BlockSpec((srows, HEAD_DIM), lambda i: (i, 0))] + consts
        s_spec = pl.BlockSpec((srows, HEAD_DIM), lambda i: (i, 0))
        args = (z, state_in, lb_rows, norm_w)
    else:
        in_specs = [zspec] + consts
        s_spec = pl.BlockSpec((W_BRANCH, HEAD_DIM), lambda i: (i // tiles_per_seq, 0))
        scratch = scratch + [pltpu.VMEM((W_BRANCH, W_BRANCH), F32)]
        args = (z, lb_rows, norm_w)
    return pl.pallas_call(
        kernel, grid=(n // rows,), in_specs=in_specs,
        out_specs=[pl.BlockSpec((rows, W_BRANCH), lambda i: (i, 0)), s_spec],
        out_shape=[jax.ShapeDtypeStruct((n, W_BRANCH), F32), jax.ShapeDtypeStruct((b * W_BRANCH, HEAD_DIM), F32)],
        scratch_shapes=scratch, compiler_params=_params(), name="hgrn")(*args)


def _merge_kernel(x_ref, oa_ref, ob_ref, oc_ref, od_ref, wg_ref, wb_ref, wo_ref, ln_ref, h_ref, *, alpha):
    x = x_ref[...]
    xb = x.astype(BF)
    m = jnp.zeros(x.shape, F32)
    for n, o_ref in enumerate((oa_ref, ob_ref, oc_ref, od_ref)):
        gate = _sigmoid(_dot(xb, wg_ref[:, n * D_MODEL:(n + 1) * D_MODEL]))
        m = m + gate * _dot(o_ref[...].astype(BF), wb_ref[n])
    mix = _dot(m.astype(BF), wo_ref[...])
    h_ref[...] = _layer_norm(alpha * x + mix, ln_ref[0:1], ln_ref[1:2])


def _merge(x, oa, ob, oc, od, wg, wb, wo, ln, alpha):
    n = x.shape[0]
    tm = min(ROW_TILE, n)
    row = lambda w: pl.BlockSpec((tm, w), lambda i: (i, 0))
    return pl.pallas_call(
        functools.partial(_merge_kernel, alpha=alpha), grid=(n // tm,),
        in_specs=[row(D_MODEL), row(256), row(256), row(256), row(256), _const_spec(wg.shape), _const_spec(wb.shape),
                  _const_spec(wo.shape), _const_spec((8, D_MODEL))],
        out_specs=row(D_MODEL), out_shape=jax.ShapeDtypeStruct((n, D_MODEL), F32),
        compiler_params=_params(), name="merge")(x, oa, ob, oc, od, wg, wb, wo, ln)


def _ffn_kernel(*refs, seq_len, has_state, n_split, alpha):
    if has_state:
        h_ref, p1_ref, p2_ref, wup_ref, wdn_ref, cv_ref, ln_ref, y_ref, a_ref = refs
    else:
        h_ref, wup_ref, wdn_ref, cv_ref, ln_ref, y_ref, a_ref, last_s = refs
    rows = h_ref.shape[0]
    i = pl.program_id(0)
    h = h_ref[...]
    hb = h.astype(BF)
    ridx = _iota((rows, 1), 0)
    t_row = (i * rows + ridx) % seq_len
    wf = D_FF // n_split
    f = jnp.zeros((rows, D_MODEL), F32)
    if not has_state:
        @pl.when((i * rows) % seq_len == 0)
        def _():
            last_s[...] = jnp.zeros_like(last_s)
    for j in range(n_split):
        lo, hi = j * wf, (j + 1) * wf
        a = _dot(hb, wup_ref[:, lo:hi])
        gt = _dot(hb, wup_ref[:, D_FF + lo:D_FF + hi])
        r1 = pltpu.roll(a, 1, 0)
        r2 = pltpu.roll(a, 2, 0)
        if has_state:
            prev1 = jnp.where(t_row == 0, p1_ref[:, lo:hi], r1)
            prev2 = jnp.where(t_row < 2, p2_ref[:, lo:hi], r2)
            a_ref[:, lo:hi] = a
        else:
            c6 = last_s[6:7, lo:hi]
            c7 = last_s[7:8, lo:hi]
            prev1 = jnp.where(ridx == 0, c7, r1)
            prev2 = jnp.where(ridx == 0, c6, jnp.where(ridx == 1, c7, r2))
            last_s[:, lo:hi] = a[rows - 8:rows]
            a_ref[:, lo:hi] = a[rows - 8:rows]
        conv = cv_ref[3:4, lo:hi] + prev2 * cv_ref[0:1, lo:hi] + prev1 * cv_ref[1:2, lo:hi] + a * cv_ref[2:3, lo:hi]
        hid = _gelu(conv) * gt
        f = f + _dot(hid.astype(BF), wdn_ref[lo:hi, :])
    y_ref[...] = _layer_norm(alpha * h + f, ln_ref[0:1], ln_ref[1:2])


def _ffn(h, p1, p2, wup, wdn, cv, ln, seq_len, alpha):
    n = h.shape[0]
    has_state = p1 is not None
    tm = min(ROW_TILE, n)
    row = lambda w: pl.BlockSpec((tm, w), lambda i: (i, 0))
    kernel = functools.partial(_ffn_kernel, seq_len=seq_len, has_state=has_state, n_split=2, alpha=alpha)
    consts = [_const_spec(wup.shape), _const_spec(wdn.shape), _const_spec((8, D_FF)), _const_spec((8, D_MODEL))]
    if has_state:
        in_specs = [row(D_MODEL), row(D_FF), row(D_FF)] + consts
        a_spec, a_rows, scratch = row(D_FF), n, []
        args = (h, p1, p2, wup, wdn, cv, ln)
    else:
        in_specs = [row(D_MODEL)] + consts
        a_spec, a_rows = pl.BlockSpec((8, D_FF), lambda i: (i, 0)), (n // tm) * 8
        scratch = [pltpu.VMEM((8, D_FF), F32)]
        args = (h, wup, wdn, cv, ln)
    return pl.pallas_call(
        kernel, grid=(n // tm,), in_specs=in_specs, out_specs=[row(D_MODEL), a_spec],
        out_shape=[jax.ShapeDtypeStruct((n, D_MODEL), F32), jax.ShapeDtypeStruct((a_rows, D_FF), F32)],
        scratch_shapes=scratch, compiler_params=_params(), name="ffn")(*args)


def _pad_lanes(x, width):
    return jnp.pad(x, [(0, 0)] * (x.ndim - 1) + [(0, width - x.shape[-1])])


def _regroup_w_in(w_in):
    o = 0
    cols = {}
    for name, w in (('fox_q', 256), ('fox_k', 256), ('fox_v', 256), ('fox_f', 4), ('dsa_q', 256), ('dsa_k', 256),
                    ('dsa_v', 256), ('idx_q', 128), ('idx_k', 32), ('idx_w', 4), ('rwkv', 1024), ('hgrn', 1024),
                    ('gate', 4096)):
        cols[name] = w_in[..., o:o + w]
        o += w
    g0 = _pad_lanes(jnp.concatenate([cols['fox_k'], cols['fox_v'], cols['fox_q'], cols['fox_f']], -1), W_GROUP)
    g1 = _pad_lanes(jnp.concatenate([cols['dsa_k'], cols['dsa_v'], cols['dsa_q'], cols['idx_q'], cols['idx_k'],
                                     cols['idx_w']], -1), W_GROUP)
    wz = jnp.concatenate([g0, g1, cols['rwkv'], cols['hgrn']], -1).astype(BF)
    return wz, cols['gate'].astype(BF)


def _rows8(*vecs, width):
    rows = [v.reshape(1, width) for v in vecs]
    rows.append(jnp.zeros((8 - len(rows), width), F32))
    return jnp.concatenate(rows, axis=0)


def _expand_first_rows(state, tt, offsets):
    b, _, w = state.shape
    rows = dict((ts, s) for s, ts in offsets)
    zero = jnp.zeros((b, 1, w), state.dtype)
    out = jnp.concatenate([state[:, rows[t]:rows[t] + 1] if t in rows else zero for t in range(tt)], axis=1)
    return out.reshape(b * tt, w)


def _layer(x, cfg, lw):
    b, tt, tv, past = cfg['b'], cfg['tt'], cfg['tv'], cfg['past']
    decode = past > 0
    z = _inproj(x, lw['wz'])
    topk = max(1, min(DSA_TOPK, (past + tv) // 4))
    new = {}
    if decode:
        o_a, logf = _fox_decode(z, lw['fox_bf'], cfg['fox_kv'], cfg['fox_lf'], cfg['page_table'], cfg['layer'], b, tt)
        new['fox_kv'] = z[:, 0:512]
    else:
        qt_a, k_a, vt_a, new['fox_kv'], logf, ccol, crow = _fox_prep(z, lw['fox_bf'], b, tt)
        o_a = _fox_attn(qt_a, k_a, vt_a, ccol, crow, b, tt)
    new['fox_logf'] = logf[:, 0:N_HEAD]
    if decode:
        kv_b, misc_b, q_b, qi_b = _dsa_prep(z, cfg['tab_k'], cfg['tab_i'], False)
        keep = _dsa_select(qi_b, misc_b, cfg['dsa_ki'], cfg['page_table'], cfg['layer'], b, tt, topk, tv)
        o_b = _dsa_decode(q_b, keep, kv_b, cfg['dsa_kv'], cfg['page_table'], cfg['layer'], b, tt)
    else:
        kv_b, misc_b, qt_b, k_b, vt_b, qit_b, wt_b, ki4_b = _dsa_prep(z, cfg['tab_k'], cfg['tab_i'], True)
        o_b = _dsa_attn(qt_b, k_b, vt_b, qit_b, wt_b, ki4_b, b, tt, topk)
    new['dsa_kv'] = kv_b
    new['dsa_kidx'] = misc_b[:, 0:D_IDX]
    o_c, new['rwkv'] = _rwkv(z, cfg.get('shift_rows'), cfg.get('rwkv_state'), lw['rwkv_mu'], lw['rwkv_vec'], lw['rwkv_wa'],
                             lw['rwkv_g2'], b, tt, tv)
    new['shift'] = z.reshape(b, tt, N_GROUP * W_GROUP)[:, tv - 1, 2 * W_GROUP:3 * W_GROUP]
    o_d, new['hgrn'] = _hgrn(z, cfg.get('hgrn_state'), lw['hgrn_lb'], lw['hgrn_nw'], b, tt, tv)
    h = _merge(x, o_a, o_b, o_c, o_d, lw['wg'], lw['wb'], lw['wo'], lw['ln1'], cfg['alpha'])
    y, new['conv'] = _ffn(h, cfg.get('conv_p1'), cfg.get('conv_p2'), lw['wup'], lw['wdn'], lw['conv'], lw['ln2'], tt, cfg['alpha'])
    return y, new


def kernel(x_prompt, x_sample, cache_fox_kv, cache_fox_logf, cache_dsa_kv, cache_dsa_kidx, state_rwkv, state_rwkv_shift, state_hgrn, state_ffn_conv, page_table, w_in, fox_bf, rwkv_mu, rwkv_w0, rwkv_w2, rwkv_a0, rwkv_a2, rwkv_g2, rwkv_kk, rwkv_ka, rwkv_rk, rwkv_ln_w, rwkv_ln_b, hgrn_lb, hgrn_norm_w, w_branch, w_o, ln1_g, ln1_b, ln2_g, ln2_b, ffn_w_in, ffn_conv_w, ffn_conv_b, ffn_w_out):
    depth = w_in.shape[0]
    bp, tp, d = x_prompt.shape
    bs, ts, _ = x_sample.shape
    n_pool, page = cache_fox_kv.shape[1], cache_fox_kv.shape[2]
    n_pages = page_table.shape[1]
    past = n_pages * page
    tsp = -(-ts // T_ALIGN) * T_ALIGN
    assert d == D_MODEL and tp % min(ROW_TILE, tp) == 0 and (bs * tsp) % min(ROW_TILE, bs * tsp) == 0
    assert tsp <= 128 and ts >= 2

    wz_all, wg_all = _regroup_w_in(w_in)
    wb_all, wo_all = w_branch.astype(BF), w_o.astype(BF)
    wup_all, wdn_all = ffn_w_in.astype(BF), ffn_w_out.astype(BF)
    zero_l = jnp.zeros((depth, LORA_W, W_BRANCH), F32)
    wa_all = jnp.concatenate([jnp.concatenate([rwkv_w2, zero_l], 2), jnp.concatenate([zero_l, rwkv_a2], 2)], 1).astype(BF)
    g2_all = rwkv_g2.astype(BF)
    lb_all = _hgrn_lb(hgrn_lb)
    page_table = page_table.astype(I32)

    fox_kv_pages = jnp.transpose(cache_fox_kv, (0, 1, 3, 4, 5, 2)).reshape(depth, n_pool, 2 * W_BRANCH, page)
    dsa_kv_pages = jnp.transpose(cache_dsa_kv, (0, 1, 3, 4, 5, 2)).reshape(depth, n_pool, 2 * W_BRANCH, page)
    dsa_ki_pages = jnp.swapaxes(cache_dsa_kidx, 2, 3)
    fox_lf_pages = _pad_rows_nd(jnp.swapaxes(cache_fox_logf, 2, 3), 8)

    pos_p = jnp.arange(tp)
    pos_s = past + (jnp.arange(bs * tsp) % tsp)[:min(ROW_TILE, bs * tsp)]
    alpha = (2 * depth) ** 0.25
    cfg_p = dict(b=bp, tt=tp, tv=tp, past=0, alpha=alpha,
                 tab_k=_rope_tables(pos_p, 256, HEAD_DIM, ROPE_HALF_QK), tab_i=_rope_tables(pos_p, 128, D_IDX, ROPE_HALF_IDX))
    cfg_s = dict(b=bs, tt=tsp, tv=ts, past=past, alpha=alpha, page_table=page_table, fox_kv=fox_kv_pages, fox_lf=fox_lf_pages,
                 dsa_kv=dsa_kv_pages, dsa_ki=dsa_ki_pages,
                 tab_k=_rope_tables(pos_s, 256, HEAD_DIM, ROPE_HALF_QK), tab_i=_rope_tables(pos_s, 128, D_IDX, ROPE_HALF_IDX))

    xp = x_prompt.reshape(bp * tp, d)
    xs = jnp.pad(x_sample, ((0, 0), (0, tsp - ts), (0, 0))).reshape(bs * tsp, d)
    new_p, new_s = [], []
    for l in range(depth):
        lw = dict(wz=wz_all[l], wg=wg_all[l], wb=wb_all[l], wo=wo_all[l], wup=wup_all[l], wdn=wdn_all[l],
                  fox_bf=_pad_lanes(fox_bf[l][None], 128), rwkv_mu=rwkv_mu[l][None],
                  rwkv_vec=_rows8(rwkv_w0[l], rwkv_a0[l], rwkv_kk[l], rwkv_ka[l], rwkv_rk[l], rwkv_ln_w[l], rwkv_ln_b[l],
                                  width=W_BRANCH),
                  rwkv_wa=wa_all[l], rwkv_g2=g2_all[l], hgrn_lb=lb_all[l], hgrn_nw=hgrn_norm_w[l][None],
                  ln1=_rows8(ln1_g[l], ln1_b[l], width=D_MODEL), ln2=_rows8(ln2_g[l], ln2_b[l], width=D_MODEL),
                  conv=_rows8(ffn_conv_w[l, 0], ffn_conv_w[l, 1], ffn_conv_w[l, 2], ffn_conv_b[l], width=D_FF))
        xp, st_p = _layer(xp, cfg_p, lw)
        cfg_l = dict(cfg_s, layer=l,
                     shift_rows=_expand_first_rows(state_rwkv_shift[l][:, None], tsp, ((0, 0),)),
                     rwkv_state=state_rwkv[l].reshape(bs * W_BRANCH, HEAD_DIM),
                     hgrn_state=state_hgrn[l].reshape(bs * W_BRANCH, HEAD_DIM),
                     conv_p1=_expand_first_rows(state_ffn_conv[l], tsp, ((1, 0),)),
                     conv_p2=_expand_first_rows(state_ffn_conv[l], tsp, ((0, 0), (1, 1))))
        xs, st_s = _layer(xs, cfg_l, lw)
        new_p.append(st_p)
        new_s.append(st_s)

    def assemble(new, b, tt, tv, decode):
        def rows(name, shape):
            a = jnp.stack([n[name] for n in new]).reshape(depth, b, tt, -1)[:, :, :tv]
            return a.reshape((depth, b, tv) + shape)

        fox_kv = rows('fox_kv', (2, N_HEAD, HEAD_DIM))
        fox_logf = rows('fox_logf', (N_HEAD,))
        dsa_kv = rows('dsa_kv', (2, N_HEAD, HEAD_DIM))
        dsa_kidx = rows('dsa_kidx', (D_IDX,))
        rwkv = jnp.stack([n['rwkv'] for n in new]).reshape(depth, b, N_HEAD, HEAD_DIM, HEAD_DIM)
        hgrn = jnp.stack([n['hgrn'] for n in new]).reshape(depth, b, N_HEAD, HEAD_DIM, HEAD_DIM)
        shift = jnp.stack([n['shift'] for n in new])
        conv = jnp.stack([n['conv'] for n in new])
        if decode:
            conv = conv.reshape(depth, b, tt, D_FF)[:, :, tv - 2:tv]
        else:
            conv = conv.reshape(depth, b, -1, 8, D_FF)[:, :, -1, 6:8]
        return fox_kv, fox_logf, dsa_kv, dsa_kidx, rwkv, shift, hgrn, conv

    y_p = xp.reshape(bp, tp, d)
    y_s = xs.reshape(bs, tsp, d)[:, :ts]
    return (y_p, y_s) + assemble(new_p, bp, tp, tp, False) + assemble(new_s, bs, tsp, ts, True)


def _pad_rows_nd(x, rows):
    pad = [(0, 0)] * x.ndim
    pad[-2] = (0, rows - x.shape[-2])
    return jnp.pad(x, pad)
```

```python
import functools
import math

import jax
import jax.numpy as jnp
import numpy as np
from jax import lax
from jax.experimental import pallas as pl
from jax.experimental.pallas import tpu as pltpu

F32 = jnp.float32
BF = jnp.bfloat16
I32 = jnp.int32
I16 = jnp.int16

D_MODEL = 1024
N_HEAD = 4
HEAD_DIM = 64
W_BRANCH = N_HEAD * HEAD_DIM
D_IDX = 32
W_IDX = N_HEAD * D_IDX
LORA_W = 64
LORA_A = 64
LORA_G = 128
D_FF = 2816
N_GROUP = 4
W_GROUP = 1024
DSA_TOPK = 256
ROPE_THETA = 500000.0
ROPE_HALF_QK = 8
ROPE_HALF_IDX = 4
RWKV_GN_EPS = 64e-5
RMS_EPS = 1e-6
LB_TINY = 1e-30
LN_EPS = 1e-5
NEG = -1e30
ROW_TILE = 256
KEY_CHUNK = 512
SELECT_SEQS = 16
DECODE_SEQS = 2
RWKV_CHUNK = 64
RWKV_LOCKSTEP = 8
HGRN_BLOCK = 16
HGRN_LOCKSTEP = 8
T_ALIGN = 8
INT_MIN = -2 ** 31


def _dot(a, b):
    return jnp.dot(a, b, preferred_element_type=F32)


def _dot_nt(a, b):
    return lax.dot_general(a, b, (((1,), (1,)), ((), ())), preferred_element_type=F32)


def _dot_tn(a, b):
    return lax.dot_general(a, b, (((0,), (0,)), ((), ())), preferred_element_type=F32)


def _split3(x):
    hi = x.astype(BF)
    r = x - hi.astype(F32)
    mid = r.astype(BF)
    lo = (r - mid.astype(F32)).astype(BF)
    return hi, mid, lo


def _dot_sel_l(sel_bf, x):
    hi, mid, lo = _split3(x)
    return _dot(sel_bf, hi) + _dot(sel_bf, mid) + _dot(sel_bf, lo)


def _dot_sel_r(x, sel_bf):
    hi, mid, lo = _split3(x)
    return _dot(hi, sel_bf) + _dot(mid, sel_bf) + _dot(lo, sel_bf)


def _dot_sel_tn(x, sel_bf):
    hi, mid, lo = _split3(x)
    return _dot_tn(hi, sel_bf) + _dot_tn(mid, sel_bf) + _dot_tn(lo, sel_bf)


def _iota(shape, dim):
    return lax.broadcasted_iota(I32, shape, dim)


def _tril(n, strict=False):
    r, c = _iota((n, n), 0), _iota((n, n), 1)
    return (r > c) if strict else (r >= c)


def _head_lane(width, head_width):
    return _iota((1, width), 1) // head_width


def _block_diag_mask(n, block):
    return (_iota((n, n), 0) // block) == (_iota((n, n), 1) // block)


def _stack_heads(x, head_width=HEAD_DIM):
    hl = _head_lane(x.shape[1], head_width)
    return jnp.concatenate([jnp.where(hl == h, x, jnp.zeros_like(x)) for h in range(N_HEAD)], axis=0)


def _unstack_heads(xs):
    c = xs.shape[0] // N_HEAD
    out = xs[0:c]
    for h in range(1, N_HEAD):
        out = out + xs[h * c:(h + 1) * c]
    return out


def _head_sum(x, ones_bd):
    return _dot_sel_r(x, ones_bd)


def _sigmoid(x):
    return 1.0 / (1.0 + jnp.exp(-x))


def _log_sigmoid(x):
    return jnp.minimum(x, 0.0) - jnp.log1p(jnp.exp(-jnp.abs(x)))


def _softplus(x):
    return jnp.maximum(x, 0.0) + jnp.log1p(jnp.exp(-jnp.abs(x)))


def _gelu(x):
    return 0.5 * x * (1.0 + lax.erf(x * (2.0 ** -0.5)))


def _layer_norm(x, g, b):
    mu = jnp.mean(x, axis=-1, keepdims=True)
    xc = x - mu
    var = jnp.mean(xc * xc, axis=-1, keepdims=True)
    return xc * lax.rsqrt(var + LN_EPS) * g + b


def _params(n_axes=1):
    return pltpu.CompilerParams(dimension_semantics=("arbitrary",) * n_axes)


def _const_spec(shape):
    nd = len(shape)
    return pl.BlockSpec(shape, lambda *_: (0,) * nd)


def _inproj_kernel(x_ref, w_ref, z_ref):
    z_ref[...] = _dot(x_ref[...].astype(BF), w_ref[...])


def _inproj(x, wz):
    n = x.shape[0]
    tm = min(ROW_TILE, n)
    nz = wz.shape[1]
    return pl.pallas_call(
        _inproj_kernel, grid=(n // tm,),
        in_specs=[pl.BlockSpec((tm, D_MODEL), lambda i: (i, 0)), _const_spec((D_MODEL, nz))],
        out_specs=pl.BlockSpec((tm, nz), lambda i: (i, 0)),
        out_shape=jax.ShapeDtypeStruct((n, nz), F32), compiler_params=_params(), name="inproj")(x, wz)


def _fox_prep_kernel(z_ref, bf_ref, qt_ref, k_ref, vt_ref, kv_ref, logf_ref, ccol_ref, crow_ref):
    t = z_ref.shape[0]
    ck = vt_ref.shape[2]
    tq = crow_ref.shape[2]
    kv_ref[...] = z_ref[:, 0:512]
    k_ref[...] = z_ref[:, 0:256].astype(BF)
    qt_ref[...] = (z_ref[:, 512:768] * (HEAD_DIM ** -0.5)).T.astype(BF)
    logf = _log_sigmoid(z_ref[:, 768:896] + bf_ref[...])
    logf_ref[...] = logf
    tri = _tril(ck).astype(BF)
    carry = jnp.zeros((1, 128), F32)
    for c in range(t // ck):
        rows = slice(c * ck, (c + 1) * ck)
        vt_ref[c] = z_ref[rows, 256:512].T.astype(BF)
        cum = _dot_sel_l(tri, logf[rows]) + carry
        carry = cum[ck - 1:ck]
        ccol_ref[rows, :] = cum
        cum_t = cum.T[0:8]
        for j in range(ck // tq):
            crow_ref[c * (ck // tq) + j] = cum_t[:, j * tq:(j + 1) * tq]


def _fox_prep(z, bf_row, b, t):
    ck = min(KEY_CHUNK, t)
    tq = min(ROW_TILE, t)
    nc, nq = t // ck, t // tq
    rows = lambda w: pl.BlockSpec((t, w), lambda i: (i, 0))
    return pl.pallas_call(
        _fox_prep_kernel, grid=(b,),
        in_specs=[rows(W_GROUP), _const_spec((1, 128))],
        out_specs=[pl.BlockSpec((W_BRANCH, t), lambda i: (0, i)), rows(W_BRANCH),
                   pl.BlockSpec((nc, W_BRANCH, ck), lambda i: (i, 0, 0)), rows(512), rows(128), rows(128),
                   pl.BlockSpec((None, nq, 8, tq), lambda i: (i, 0, 0, 0))],
        out_shape=[jax.ShapeDtypeStruct((W_BRANCH, b * t), BF), jax.ShapeDtypeStruct((b * t, W_BRANCH), BF),
                   jax.ShapeDtypeStruct((b * nc, W_BRANCH, ck), BF), jax.ShapeDtypeStruct((b * t, 512), F32),
                   jax.ShapeDtypeStruct((b * t, 128), F32), jax.ShapeDtypeStruct((b * t, 128), F32),
                   jax.ShapeDtypeStruct((b, nq, 8, tq), F32)],
        compiler_params=_params(), name="fox_prep")(z, bf_row)


def _flash_init(m_s, l_s, acc_s):
    m_s[...] = jnp.full(m_s.shape, NEG, F32)
    l_s[...] = jnp.zeros(l_s.shape, F32)
    acc_s[...] = jnp.zeros(acc_s.shape, F32)


def _flash_update(sts, vts, m_s, l_s, acc_s):
    scaled = []
    for h, st in enumerate(sts):
        m_old = m_s[h]
        m_new = jnp.maximum(m_old, jnp.max(st, axis=0, keepdims=True))
        a = jnp.exp(m_old - m_new)
        p = jnp.exp(st - m_new)
        l_s[h] = a * l_s[h] + jnp.sum(p, axis=0, keepdims=True)
        m_s[h] = m_new
        scaled.append((a, p.astype(BF)))
    for h, (a, p) in enumerate(scaled):
        acc_s[h] = a * acc_s[h] + _dot(vts[h], p)


def _flash_result(l_s, acc_s):
    return jnp.concatenate([acc_s[h] / l_s[h] for h in range(N_HEAD)], axis=0).T


def _flash_scratch(tq):
    return [pltpu.VMEM((N_HEAD, 1, tq), F32), pltpu.VMEM((N_HEAD, 1, tq), F32), pltpu.VMEM((N_HEAD, HEAD_DIM, tq), F32)]


def _head_rows(x, head_rows):
    rh = _iota((x.shape[0], 1), 0) // head_rows
    return [jnp.where(rh == h, x, jnp.zeros_like(x)) for h in range(N_HEAD)]


def _causal_chunks(i, tq, ck):
    return (i * tq) // ck + 1


def _fox_attn_kernel(qt_ref, k_ref, vt_ref, ccol_ref, crow_ref, o_ref, m_s, l_s, acc_s):
    tq = qt_ref.shape[1]
    ck = vt_ref.shape[2]
    i = pl.program_id(1)
    qpos = i * tq + _iota((1, tq), 1)
    qth = _head_rows(qt_ref[...], HEAD_DIM)
    cq = [crow_ref[h:h + 1, :] for h in range(N_HEAD)]
    _flash_init(m_s, l_s, acc_s)

    def chunk(c, carry):
        rows = pl.ds(pl.multiple_of(c * ck, ck), ck)
        k = k_ref[rows, :]
        ccol = ccol_ref[rows, :]
        visible = (c * ck + _iota((ck, 1), 0)) <= qpos
        sts = [jnp.where(visible, _dot(k, qth[h]) + cq[h] - ccol[:, h:h + 1], NEG) for h in range(N_HEAD)]
        vts = [vt_ref[c, h * HEAD_DIM:(h + 1) * HEAD_DIM, :] for h in range(N_HEAD)]
        _flash_update(sts, vts, m_s, l_s, acc_s)
        return carry

    lax.fori_loop(0, _causal_chunks(i, tq, ck), chunk, 0)
    o_ref[...] = _flash_result(l_s, acc_s)


def _fox_attn(qt, k, vt, ccol, crow, b, t):
    tq = crow.shape[3]
    nq = t // tq
    ck = vt.shape[2]
    nc = t // ck
    return pl.pallas_call(
        _fox_attn_kernel, grid=(b, nq),
        in_specs=[pl.BlockSpec((W_BRANCH, tq), lambda bi, i: (0, bi * nq + i)),
                  pl.BlockSpec((t, W_BRANCH), lambda bi, i: (bi, 0)),
                  pl.BlockSpec((nc, W_BRANCH, ck), lambda bi, i: (bi, 0, 0)),
                  pl.BlockSpec((t, 128), lambda bi, i: (bi, 0)),
                  pl.BlockSpec((None, None, 8, tq), lambda bi, i: (bi, i, 0, 0))],
        out_specs=pl.BlockSpec((tq, W_BRANCH), lambda bi, i: (bi * nq + i, 0)),
        out_shape=jax.ShapeDtypeStruct((b * t, W_BRANCH), F32), scratch_shapes=_flash_scratch(tq),
        compiler_params=_params(2), name="fox_attn")(qt, k, vt, ccol, crow)


def _page_specs(n_pages, block, layer, n_seq=1):
    nd = len(block)

    def mk(j, p):
        return pl.BlockSpec((None, None) + block, lambda g, pt: (layer, pt[g * n_seq + j, p]) + (0,) * nd)

    return [mk(j, p) for j in range(n_seq) for p in range(n_pages)]


def _softmax_rows(s):
    m = jnp.max(s, axis=1, keepdims=True)
    p = jnp.exp(s - m)
    return p.astype(BF), jnp.sum(p, axis=1, keepdims=True)


def _weighted_values(pb, l, vt_past, v_new, past):
    o = _dot_nt(pb[:, :past], vt_past) + _dot(pb[:, past:], v_new)
    return _unstack_heads(_stack_mask(o / l))


def _cat_pages(refs, lo, hi):
    return jnp.concatenate([r[lo:hi, :] for r in refs], axis=1)


def _stack_mask(o):
    c = o.shape[0] // N_HEAD
    hl = _head_lane(o.shape[1], HEAD_DIM)
    row_h = _iota((o.shape[0], 1), 0) // c
    return jnp.where(row_h == hl, o, 0.0)


def _pad_rows(x, rows):
    return jnp.concatenate([x, jnp.zeros((rows - x.shape[0], x.shape[1]), x.dtype)], axis=0)


def _fox_decode_kernel(pt_ref, z_ref, bf_ref, *refs, n_pages, n_seq):
    o_ref, logf_ref = refs[2 * n_seq * n_pages:]
    tt = z_ref.shape[0] // n_seq
    past = n_pages * refs[0].shape[1]
    parts = [_fox_decode_scores(z_ref, bf_ref, logf_ref, slice(j * tt, (j + 1) * tt),
                                refs[j * n_pages:(j + 1) * n_pages],
                                refs[(n_seq + j) * n_pages:(n_seq + j + 1) * n_pages]) for j in range(n_seq)]
    probs = [_softmax_rows(s) for s, _, _ in parts]
    for j, ((pb, l), (_, vt_past, v_new)) in enumerate(zip(probs, parts)):
        o_ref[j * tt:(j + 1) * tt, :] = _weighted_values(pb, l, vt_past, v_new, past)


def _fox_decode_scores(z_ref, bf_ref, logf_ref, rows, kv_refs, lf_refs):
    tt = rows.stop - rows.start
    n_pages = len(kv_refs)
    page = kv_refs[0].shape[1]
    k_new = z_ref[rows, 0:256]
    v_new = z_ref[rows, 256:512]
    q = z_ref[rows, 512:768] * (HEAD_DIM ** -0.5)
    logf = _log_sigmoid(z_ref[rows, 768:896] + bf_ref[...])
    logf_ref[rows, :] = logf

    lf = jnp.concatenate([r[...] for r in lf_refs], axis=0)
    n = lf.shape[0]
    tri_u = (_iota((page, page), 0) <= _iota((page, page), 1)).astype(BF)
    in_page = _dot_sel_r(lf, tri_u)
    tot = jnp.broadcast_to(in_page[:, page - 1:page], (n, page))
    r, c = _iota((n, n), 0), _iota((n, n), 1)
    later = jnp.where(((r % 8) == (c % 8)) & ((c // 8) >= (r // 8)), -1.0, 0.0).astype(BF)
    ck_rel = in_page + _dot_sel_l(later, tot)

    cn_col = _dot_sel_l(_tril(tt).astype(BF), logf)
    tri_pad = (_iota((tt, 128), 0) <= _iota((tt, 128), 1)).astype(BF)
    cn_row = _dot_sel_tn(logf, tri_pad)

    qbd = _stack_heads(q).astype(BF)
    s_past = _dot(qbd, _cat_pages(kv_refs, 0, W_BRANCH).astype(BF))
    vt_past = _cat_pages(kv_refs, W_BRANCH, 2 * W_BRANCH).astype(BF)
    s_new = _dot_nt(qbd, _pad_rows(k_new, 128).astype(BF))

    cq = jnp.concatenate([cn_col[:, h:h + 1] for h in range(N_HEAD)], axis=0)
    bias_past = jnp.concatenate(
        [jnp.concatenate([jnp.broadcast_to(ck_rel[p * 8 + h:p * 8 + h + 1], (tt, page)) for h in range(N_HEAD)], axis=0)
         for p in range(n_pages)], axis=1)
    bias_new = jnp.concatenate([jnp.broadcast_to(cn_row[h:h + 1], (tt, 128)) for h in range(N_HEAD)], axis=0)
    tq = _iota((N_HEAD * tt, 128), 0) % tt
    ok_new = _iota((N_HEAD * tt, 128), 1) <= tq
    s = jnp.concatenate([s_past + cq - bias_past, jnp.where(ok_new, s_new + cq - bias_new, NEG)], axis=1)
    return s, vt_past, _pad_rows(v_new, 128).astype(BF)


def _fox_decode(z, bf_row, cache_kv, cache_lf, page_table, layer, b, tt):
    n_pages = page_table.shape[1]
    page = cache_kv.shape[3]
    n_seq = min(DECODE_SEQS, b)
    kernel = functools.partial(_fox_decode_kernel, n_pages=n_pages, n_seq=n_seq)
    rows = lambda w: pl.BlockSpec((n_seq * tt, w), lambda i, pt: (i, 0))
    grid_spec = pltpu.PrefetchScalarGridSpec(
        num_scalar_prefetch=1, grid=(b // n_seq,),
        in_specs=[rows(W_GROUP), pl.BlockSpec((1, 128), lambda i, pt: (0, 0))]
        + _page_specs(n_pages, (2 * W_BRANCH, page), layer, n_seq) + _page_specs(n_pages, (8, page), layer, n_seq),
        out_specs=[rows(W_BRANCH), rows(128)])
    return pl.pallas_call(
        kernel, grid_spec=grid_spec,
        out_shape=[jax.ShapeDtypeStruct((b * tt, W_BRANCH), F32), jax.ShapeDtypeStruct((b * tt, 128), F32)],
        compiler_params=_params(), name="fox_decode")(page_table, z, bf_row, *([cache_kv] * (n_seq * n_pages)),
                                                      *([cache_lf] * (n_seq * n_pages)))


def _rope_tables(pos, n_lanes, head_width, half):
    inv = jnp.power(ROPE_THETA, -jnp.arange(half, dtype=F32) / half)
    ang = pos.astype(F32)[:, None] * inv[None]
    d = np.arange(n_lanes) % head_width
    first = jnp.asarray(d < half)[None]
    second = jnp.asarray((d >= half) & (d < 2 * half))[None]
    cos = jnp.cos(ang)[:, d % half]
    sin = jnp.sin(ang)[:, d % half]
    c = jnp.where(first | second, cos, 1.0)
    s_up = jnp.where(first, -sin, 0.0)
    s_dn = jnp.where(second, sin, 0.0)
    return jnp.concatenate([c, s_up, s_dn], axis=1).astype(F32)


def _rope(x, tab, half):
    w = x.shape[1]
    return x * tab[:, 0:w] + pltpu.roll(x, w - half, 1) * tab[:, w:2 * w] + pltpu.roll(x, half, 1) * tab[:, 2 * w:3 * w]


def _dsa_prep_kernel(z_ref, tk_ref, ti_ref, kv_ref, misc_ref, *out_refs, prefill):
    tk = tk_ref[...]
    ti = ti_ref[...]
    k = _rope(z_ref[:, 0:256], tk, ROPE_HALF_QK)
    v = z_ref[:, 256:512]
    q = _rope(z_ref[:, 512:768], tk, ROPE_HALF_QK) * (HEAD_DIM ** -0.5)
    qi = _rope(z_ref[:, 768:896], ti, ROPE_HALF_IDX)
    tail = z_ref[:, 896:1024]
    ki = _rope(tail, ti, ROPE_HALF_IDX)
    kv_ref[...] = jnp.concatenate([k, v], axis=1)
    lane = _iota((1, 128), 1)
    misc = jnp.where(lane < D_IDX, ki, tail * (N_HEAD ** -0.5))
    misc_ref[...] = misc
    if prefill:
        qt_ref, k_ref, vt_ref, qit_ref, wt_ref, ki4_ref = out_refs
        qt_ref[...] = q.T.astype(BF)
        k_ref[...] = k.astype(BF)
        vt_ref[...] = v.T.astype(BF)
        qit_ref[...] = qi.T.astype(BF)
        wt_ref[...] = misc.T[D_IDX:D_IDX + 8]
        kim = jnp.where(lane < D_IDX, ki, 0.0)
        ki4 = kim + pltpu.roll(kim, 32, 1) + pltpu.roll(kim, 64, 1) + pltpu.roll(kim, 96, 1)
        ki4_ref[...] = ki4.astype(BF)
    else:
        q_ref, qi_ref = out_refs
        q_ref[...] = q
        qi_ref[...] = qi


def _dsa_prep(z, tab_k, tab_i, prefill):
    n = z.shape[0]
    tm = min(ROW_TILE, n)
    nt = tab_k.shape[0] // tm
    row = lambda w: pl.BlockSpec((tm, w), lambda i: (i, 0))
    col = lambda w: pl.BlockSpec((w, tm), lambda i: (0, i))
    out_specs = [row(512), row(128)]
    out_shape = [jax.ShapeDtypeStruct((n, 512), F32), jax.ShapeDtypeStruct((n, 128), F32)]
    if prefill:
        ck = 2 * tm
        out_specs += [col(256), row(256), pl.BlockSpec((None, W_BRANCH, tm), lambda i: (i // 2, 0, i % 2)),
                      col(128), col(8), row(128)]
        out_shape += [jax.ShapeDtypeStruct((256, n), BF), jax.ShapeDtypeStruct((n, 256), BF),
                      jax.ShapeDtypeStruct((n // ck, W_BRANCH, ck), BF), jax.ShapeDtypeStruct((128, n), BF),
                      jax.ShapeDtypeStruct((8, n), F32), jax.ShapeDtypeStruct((n, 128), BF)]
    else:
        out_specs += [row(256), row(128)]
        out_shape += [jax.ShapeDtypeStruct((n, 256), F32), jax.ShapeDtypeStruct((n, 128), F32)]
    return pl.pallas_call(
        functools.partial(_dsa_prep_kernel, prefill=prefill), grid=(n // tm,),
        in_specs=[pl.BlockSpec((tm, W_GROUP), lambda i: (i, 1)),
                  pl.BlockSpec((tm, 768), lambda i: (i % nt, 0)), pl.BlockSpec((tm, 384), lambda i: (i % nt, 0))],
        out_specs=out_specs, out_shape=out_shape, compiler_params=_params(), name="dsa_prep")(z, tab_k, tab_i)


def _sortable(x):
    b = pltpu.bitcast(x, I32)
    return b ^ ((b >> 31) & I32(0x7FFFFFFF))


def _topk_mask(score, idx, topk, n_idx_bits):
    key = _sortable(score)
    rows = score.shape[0]

    def count(pred):
        return jnp.sum(pred.astype(I32), axis=1, keepdims=True)

    def value_step(it, ans):
        cand = ans + jnp.left_shift(I32(1), I32(31) - it)
        return jnp.where(count(key >= cand) >= topk, cand, ans)

    thr = lax.fori_loop(0, 32, value_step, jnp.full((rows, 1), INT_MIN, I32))
    above = key > thr
    tie = key == thr
    need = topk - count(above)

    def index_step(it, lo):
        cand = lo + jnp.left_shift(I32(1), I32(n_idx_bits - 1) - it)
        return jnp.where(count(tie & (idx < cand)) < need, cand, lo)

    last = lax.fori_loop(0, n_idx_bits, index_step, jnp.zeros((rows, 1), I32))
    return above | (tie & (idx <= last))


def _dsa_attn_kernel(qt_ref, qit_ref, wt_ref, k_ref, vt_ref, ki4_ref, o_ref, hi_s, lo_s, tie_s, keep_s, m_s, l_s, acc_s, *,
                     topk, n_idx_bits):
    tq = qt_ref.shape[1]
    ck = hi_s.shape[1]
    i = pl.program_id(1)
    nc = _causal_chunks(i, tq, ck)
    qpos = i * tq + _iota((1, tq), 1)
    qpos16 = qpos.astype(I16)
    qith = _head_rows(qit_ref[...], D_IDX)
    w_row = [wt_ref[h:h + 1, :] * (D_IDX ** -0.5) for h in range(N_HEAD)]
    i16_min, i16_max = I16(-32768), I16(32767)

    def key_pos(c):
        return c * ck + _iota((ck, tq), 0)

    def score_chunk(c, carry):
        ki4 = ki4_ref[pl.ds(pl.multiple_of(c * ck, ck), ck), :]
        sc = jnp.zeros((ck, tq), F32)
        for h in range(N_HEAD):
            sc = sc + jnp.maximum(_dot(ki4, qith[h]), 0.0) * w_row[h]
        sc = jnp.where(key_pos(c) <= qpos, sc, NEG)
        key = _sortable(jnp.where(sc == 0.0, 0.0, sc))
        hi_s[c] = (key >> 16).astype(I16)
        lo_s[c] = ((key & I32(0xFFFF)) - I32(32768)).astype(I16)
        return carry

    lax.fori_loop(0, nc, score_chunk, 0)

    def count(pred):
        def body(c, acc):
            ones = jnp.where(pred(c), I16(1), I16(0))
            for j in range(ck // 16):
                acc = acc + ones[j * 16:(j + 1) * 16]
            return acc
        acc = lax.fori_loop(0, nc, body, jnp.zeros((16, tq), I16))
        return jnp.sum(acc.astype(I32), axis=0, keepdims=True)

    def search16(ref, want):
        def step(it, ans):
            cand = ans + jnp.left_shift(I32(1), I32(15) - it)
            c16 = cand.astype(I16)
            return jnp.where(count(lambda c: ref[c] >= c16) >= want, cand, ans)
        return lax.fori_loop(0, 16, step, jnp.full((1, tq), -32768, I32)).astype(I16)

    h16 = search16(hi_s, topk)
    n_above = count(lambda c: hi_s[c] > h16)

    def mask_lo(c, carry):
        lo_s[c] = jnp.where(hi_s[c] == h16, lo_s[c], i16_min)
        return carry

    lax.fori_loop(0, nc, mask_lo, 0)
    l16 = search16(lo_s, topk - n_above)
    need = topk - n_above - count(lambda c: lo_s[c] > l16)

    def tie_chunk(c, carry):
        kpos16 = key_pos(c).astype(I16)
        tie = (hi_s[c] == h16) & (lo_s[c] == l16) & (kpos16 <= qpos16)
        tie_s[c] = jnp.where(tie, kpos16, i16_max)
        return carry

    lax.fori_loop(0, nc, tie_chunk, 0)

    def index_step(it, lo):
        cand = lo + jnp.left_shift(I32(1), I32(n_idx_bits - 1) - it)
        c16 = cand.astype(I16)
        return jnp.where(count(lambda c: tie_s[c] < c16) < need, cand, lo)

    last16 = lax.fori_loop(0, n_idx_bits, index_step, jnp.zeros((1, tq), I32)).astype(I16)

    def keep_chunk(c, carry):
        kpos16 = key_pos(c).astype(I16)
        sel = jnp.where(hi_s[c] > h16, I16(1), I16(0)) + jnp.where(lo_s[c] > l16, I16(1), I16(0)) \
            + jnp.where(tie_s[c] <= last16, I16(1), I16(0))
        sel = jnp.where(kpos16 <= qpos16, sel, I16(0))
        keep_s[c] = jnp.where(sel.astype(I32) > 0, 0.0, NEG)
        return carry

    lax.fori_loop(0, nc, keep_chunk, 0)

    qth = _head_rows(qt_ref[...], HEAD_DIM)
    _flash_init(m_s, l_s, acc_s)

    def attend_chunk(c, carry):
        k = k_ref[pl.ds(pl.multiple_of(c * ck, ck), ck), :]
        keep = keep_s[c]
        sts = [_dot(k, qth[h]) + keep for h in range(N_HEAD)]
        vts = [vt_ref[c, h * HEAD_DIM:(h + 1) * HEAD_DIM, :] for h in range(N_HEAD)]
        _flash_update(sts, vts, m_s, l_s, acc_s)
        return carry

    lax.fori_loop(0, nc, attend_chunk, 0)
    o_ref[...] = _flash_result(l_s, acc_s)


def _dsa_attn(qt, k, vt, qit, wt, ki4, b, t, topk):
    tq = min(ROW_TILE, t)
    ck = vt.shape[2]
    nq, nc = t // tq, t // ck
    assert topk <= ck and t < 2 ** 15
    qcol = lambda w: pl.BlockSpec((w, tq), lambda bi, i: (0, bi * nq + i))
    seq = lambda w: pl.BlockSpec((t, w), lambda bi, i: (bi, 0))
    kernel = functools.partial(_dsa_attn_kernel, topk=topk, n_idx_bits=max(1, (t - 1).bit_length()))
    scratch = [pltpu.VMEM((nc, ck, tq), I16)] * 3 + [pltpu.VMEM((nc, ck, tq), F32)] + _flash_scratch(tq)
    return pl.pallas_call(
        kernel, grid=(b, nq),
        in_specs=[qcol(256), qcol(128), qcol(8), seq(256), pl.BlockSpec((nc, W_BRANCH, ck), lambda bi, i: (bi, 0, 0)),
                  seq(128)],
        out_specs=pl.BlockSpec((tq, W_BRANCH), lambda bi, i: (bi * nq + i, 0)),
        out_shape=jax.ShapeDtypeStruct((b * t, W_BRANCH), F32), scratch_shapes=scratch,
        compiler_params=_params(2), name="dsa_attn")(qt, qit, wt, k, vt, ki4)


def _dsa_select_kernel(pt_ref, qi_ref, misc_ref, *refs, n_pages, n_seq, tt, topk, t_valid):
    ki_refs = refs[:n_seq * n_pages]
    keep_ref = refs[n_seq * n_pages]
    page = ki_refs[0].shape[1]
    past = n_pages * page
    scores = []
    for j in range(n_seq):
        rows = slice(j * tt, (j + 1) * tt)
        qi = qi_ref[rows, :]
        qi_h = jnp.concatenate([qi[:, h * D_IDX:(h + 1) * D_IDX] for h in range(N_HEAD)], axis=0).astype(BF)
        w_col = jnp.concatenate([misc_ref[rows, D_IDX + h:D_IDX + h + 1] for h in range(N_HEAD)], axis=0)
        kit_past = _cat_pages(ki_refs[j * n_pages:(j + 1) * n_pages], 0, D_IDX).astype(BF)
        ki_new = _pad_rows(misc_ref[rows, 0:D_IDX], 128).astype(BF)
        s = jnp.concatenate([_dot(qi_h, kit_past), _dot_nt(qi_h, ki_new)], axis=1)
        scores.append(_unstack_heads(jnp.maximum(s * (D_IDX ** -0.5), 0.0) * w_col))
    score = jnp.concatenate(scores, axis=0)
    n_keys = past + 128
    idx = _iota((1, n_keys), 1)
    new_t = idx - past
    t_q = _iota((n_seq * tt, 1), 0) % tt
    visible = (new_t <= t_q) & (new_t < t_valid)
    score = jnp.where(visible, jnp.where(score == 0.0, 0.0, score), NEG)
    sel = _topk_mask(score, idx, topk, n_keys.bit_length()) & visible
    keep_ref[...] = jnp.where(sel, 0.0, NEG)


def _dsa_select(qi, misc, cache_ki, page_table, layer, b, tt, topk, t_valid):
    n_pages = page_table.shape[1]
    page = cache_ki.shape[3]
    n_seq = min(SELECT_SEQS, b)
    n_keys = n_pages * page + 128
    kernel = functools.partial(_dsa_select_kernel, n_pages=n_pages, n_seq=n_seq, tt=tt, topk=topk, t_valid=t_valid)

    def page_spec(j, p):
        return pl.BlockSpec((None, None, D_IDX, page), lambda g, pt: (layer, pt[g * n_seq + j, p], 0, 0))

    rows = lambda w: pl.BlockSpec((n_seq * tt, w), lambda g, pt: (g, 0))
    grid_spec = pltpu.PrefetchScalarGridSpec(
        num_scalar_prefetch=1, grid=(b // n_seq,),
        in_specs=[rows(128), rows(128)] + [page_spec(j, p) for j in range(n_seq) for p in range(n_pages)],
        out_specs=rows(n_keys))
    return pl.pallas_call(
        kernel, grid_spec=grid_spec, out_shape=jax.ShapeDtypeStruct((b * tt, n_keys), F32),
        compiler_params=_params(), name="dsa_select")(page_table, qi, misc, *([cache_ki] * (n_seq * n_pages)))


def _dsa_decode_kernel(pt_ref, q_ref, keep_ref, kvn_ref, *refs, n_pages, n_seq):
    o_ref = refs[n_seq * n_pages]
    tt = q_ref.shape[0] // n_seq
    past = n_pages * refs[0].shape[1]

    def scores(j):
        rows = slice(j * tt, (j + 1) * tt)
        kv_refs = refs[j * n_pages:(j + 1) * n_pages]
        qbd = _stack_heads(q_ref[rows, :]).astype(BF)
        k_new = _pad_rows(kvn_ref[rows, 0:256], 128).astype(BF)
        v_new = _pad_rows(kvn_ref[rows, 256:512], 128).astype(BF)
        s = jnp.concatenate([_dot(qbd, _cat_pages(kv_refs, 0, W_BRANCH).astype(BF)), _dot_nt(qbd, k_new)], axis=1)
        keep4 = jnp.concatenate([keep_ref[rows, :]] * N_HEAD, axis=0)
        return jnp.where(keep4 == 0.0, s, NEG), _cat_pages(kv_refs, W_BRANCH, 2 * W_BRANCH).astype(BF), v_new

    parts = [scores(j) for j in range(n_seq)]
    probs = [_softmax_rows(s) for s, _, _ in parts]
    for j, ((pb, l), (_, vt_past, v_new)) in enumerate(zip(probs, parts)):
        o_ref[j * tt:(j + 1) * tt, :] = _weighted_values(pb, l, vt_past, v_new, past)


def _dsa_decode(q, keep, kv_new, cache_kv, page_table, layer, b, tt):
    n_pages = page_table.shape[1]
    page = cache_kv.shape[3]
    n_seq = min(DECODE_SEQS, b)
    kernel = functools.partial(_dsa_decode_kernel, n_pages=n_pages, n_seq=n_seq)
    row = lambda w: pl.BlockSpec((n_seq * tt, w), lambda i, pt: (i, 0))
    grid_spec = pltpu.PrefetchScalarGridSpec(
        num_scalar_prefetch=1, grid=(b // n_seq,),
        in_specs=[row(256), row(keep.shape[1]), row(512)] + _page_specs(n_pages, (2 * W_BRANCH, page), layer, n_seq),
        out_specs=row(256))
    return pl.pallas_call(
        kernel, grid_spec=grid_spec, out_shape=jax.ShapeDtypeStruct((b * tt, W_BRANCH), F32),
        compiler_params=_params(), name="dsa_decode")(page_table, q, keep, kv_new, *([cache_kv] * (n_seq * n_pages)))


def _rwkv_chunks(chunks, n_double):
    c = chunks[0][0].shape[0]
    cc = N_HEAD * c
    tri = _tril(c).astype(BF)
    strict = _tril(cc, strict=True)
    incl = _tril(cc)
    each = lambda fn, *lists: [fn(*xs) for xs in zip(*lists)]
    r, k, v, lw, kap, beta = (list(x) for x in zip(*chunks))
    g = each(lambda x: _dot_sel_l(tri, x), lw)
    g_end = [x[c - 1:c] for x in g]
    e_neg = each(lambda x: jnp.exp(-x), g)
    e_end = each(lambda x, y: jnp.exp(y - x), g, g_end)
    a_f = each(lambda kp, x, l: _stack_heads(kp * jnp.exp(x - l)), kap, g, lw)
    a_s = each(lambda x: x.astype(BF), a_f)
    r_s = each(lambda x, y: _stack_heads(x * jnp.exp(y)), r, g)
    r_sb = each(lambda x: x.astype(BF), r_s)
    bb_s = each(lambda x, e: _stack_heads(x * e).astype(BF), beta, e_neg)
    bk_s = each(lambda x, e: _stack_heads(x * e).astype(BF), k, e_neg)
    v_s = each(lambda x: _stack_heads(x).astype(BF), v)
    kh_s = each(lambda x, e: _stack_heads(x * e).astype(BF), k, e_end)
    bh_s = each(lambda x, e: _stack_heads(x * e).astype(BF), beta, e_end)
    l_b = each(lambda x, y: jnp.where(strict, _dot_nt(x, y), 0.0), a_s, bb_s)
    l_k = each(lambda x, y: jnp.where(strict, _dot_nt(x, y), 0.0).astype(BF), a_s, bk_s)
    w_b = each(lambda x, y: jnp.where(incl, _dot_nt(x, y), 0.0).astype(BF), r_sb, bb_s)
    w_k = each(lambda x, y: jnp.where(incl, _dot_nt(x, y), 0.0).astype(BF), r_sb, bk_s)
    y = each(lambda x: -x, l_b)
    n = y
    for _ in range(n_double):
        y = each(lambda x: _dot(x.astype(BF), x.astype(BF)), y)
        n = each(lambda p, q: p + q + _dot(p.astype(BF), q.astype(BF)), n, y)
    nb = each(lambda x: x.astype(BF), n)
    a_t = each(lambda f, p, q: (f + _dot(p, q)).astype(BF), a_f, nb, a_s)
    lkv = each(_dot, l_k, v_s)
    u0 = each(lambda x, p: (x + _dot(p, x.astype(BF))).astype(BF), lkv, nb)
    r_hat = each(lambda x, w, a: (x - _dot(w, a)).astype(BF), r_s, w_b, a_t)
    y0 = each(lambda wk, vs, wb, u: _dot(wk, vs) - _dot(wb, u), w_k, v_s, w_b, u0)
    h_mat = each(lambda a, b: _dot_tn(a, b).astype(BF), a_t, bh_s)
    s_add = each(lambda vs, kh, u, bh: _dot_tn(vs, kh) - _dot_tn(u, bh), v_s, kh_s, u0, bh_s)
    decay = each(jnp.exp, g_end)
    return list(zip(r_hat, y0, decay, h_mat, s_add))


def _rwkv_apply(par, s_big):
    r_hat, y0, decay, h_mat, s_add = par
    sb = s_big.astype(BF)
    ys = _dot_nt(r_hat, sb) + y0
    s_new = s_big * decay - _dot(sb, h_mat) + s_add
    return _unstack_heads(ys), s_new


def _to_block_diag(x):
    return jnp.where(_block_diag_mask(W_BRANCH, HEAD_DIM), jnp.concatenate([x] * N_HEAD, axis=1), 0.0)


def _from_block_diag(x):
    y = x + pltpu.roll(x, 64, 1) + pltpu.roll(x, 128, 1) + pltpu.roll(x, 192, 1)
    return y[:, 0:HEAD_DIM]


def _rwkv_kernel(*refs, seq_len, t_valid, chunk, n_double, lockstep, has_state):
    if has_state:
        (z_ref, prev_ref, s_in_ref, mu_ref, vec_ref, wa_ref, g2_ref, o_ref, s_out_ref,
         r_s, k_s, v_s, lw_s, kap_s, beta_s, y_s) = refs
    else:
        (z_ref, mu_ref, vec_ref, wa_ref, g2_ref, o_ref, s_out_ref,
         r_s, k_s, v_s, lw_s, kap_s, beta_s, y_s, state_s, last_s) = refs
    rows = z_ref.shape[0]
    i = pl.program_id(0)
    t_row = (i * rows + _iota((rows, 1), 0)) % seq_len
    pr = z_ref[...]
    shifted = pltpu.roll(pr, 1, 0)
    if has_state:
        prev = jnp.where(t_row == 0, prev_ref[...], shifted)
    else:
        first = (i * rows) % seq_len == 0

        @pl.when(first)
        def _():
            last_s[...] = jnp.zeros_like(last_s)
            state_s[...] = jnp.zeros_like(state_s)

        prev = jnp.where(_iota((rows, 1), 0) == 0, last_s[7:8, :], shifted)
        last_s[...] = pr[rows - 8:rows]
    xs = pr + (prev - pr) * mu_ref[...]
    r, k, v = xs[:, 0:256], xs[:, 256:512], xs[:, 512:768]
    lora = xs[:, 768:896]
    lora = jnp.where(_iota((1, 128), 1) < LORA_W, jnp.tanh(lora), lora)
    wa = _dot(lora.astype(BF), wa_ref[...])
    w0, a0, kk_p, ka_p = vec_ref[0:1], vec_ref[1:2], vec_ref[2:3], vec_ref[3:4]
    rk_p, ln_w, ln_b = vec_ref[4:5], vec_ref[5:6], vec_ref[6:7]
    w_log = -_softplus(-(w0 + wa[:, 0:256])) - 0.5
    lw = -jnp.exp(w_log)
    a = _sigmoid(a0 + wa[:, 256:512])
    gate = _dot(_sigmoid(xs[:, 896:1024]).astype(BF), g2_ref[...])
    ones_bd = _block_diag_mask(W_BRANCH, HEAD_DIM).astype(BF)
    kk = k * kk_p
    kap = kk / jnp.maximum(jnp.sqrt(_head_sum(kk * kk, ones_bd)), 1e-12)
    k2 = k * (1.0 + (a - 1.0) * ka_p)
    bonus = _head_sum(r * k2 * rk_p, ones_bd) * v
    live = t_row < t_valid
    r_s[...] = r
    k_s[...] = jnp.where(live, k2, 0.0)
    v_s[...] = jnp.where(live, v, 0.0)
    lw_s[...] = jnp.where(live, lw, 0.0)
    kap_s[...] = jnp.where(live, kap, 0.0)
    beta_s[...] = jnp.where(live, a * kap, 0.0)

    def group(gi, carry):
        def rows_of(j, size):
            return pl.ds(pl.multiple_of((gi * lockstep + j) * size, size), size)

        sls = [rows_of(j, chunk) for j in range(lockstep)]
        pars = _rwkv_chunks([(r_s[sl, :], k_s[sl, :], v_s[sl, :], lw_s[sl, :], kap_s[sl, :], beta_s[sl, :]) for sl in sls],
                            n_double)
        if has_state:
            for j, (sl, par) in enumerate(zip(sls, pars)):
                st = rows_of(j, W_BRANCH)
                y, s_new = _rwkv_apply(par, _to_block_diag(s_in_ref[st, :]))
                y_s[sl, :] = y
                s_out_ref[st, :] = _from_block_diag(s_new)
        else:
            s_big = state_s[...]
            for sl, par in zip(sls, pars):
                y, s_big = _rwkv_apply(par, s_big)
                y_s[sl, :] = y
            state_s[...] = s_big
        return carry

    lax.fori_loop(0, rows // (chunk * lockstep), group, 0)
    if not has_state:
        s_out_ref[...] = _from_block_diag(state_s[...])
    y = y_s[...]
    mu = _head_sum(y, ones_bd) * (1.0 / HEAD_DIM)
    yc = y - mu
    var = _head_sum(yc * yc, ones_bd) * (1.0 / HEAD_DIM)
    yn = yc * lax.rsqrt(var + RWKV_GN_EPS) * ln_w + ln_b
    o_ref[...] = (yn + bonus) * gate


def _rwkv(z, prev_rows, state_in, mu, vec, wa, g2, b, seq_len, t_valid):
    n = z.shape[0]
    has_state = state_in is not None
    chunk = min(RWKV_CHUNK, seq_len)
    rows = min(ROW_TILE if has_state else RWKV_LOCKSTEP * chunk, n)
    n_double = max(0, int(math.log2(chunk)) - 1)
    steps = n // rows
    seq_per_tile = max(1, rows // seq_len)
    tiles_per_seq = max(1, seq_len // rows)
    lockstep = min(RWKV_LOCKSTEP, rows // chunk)
    kernel = functools.partial(_rwkv_kernel, seq_len=seq_len, t_valid=t_valid, chunk=chunk, n_double=n_double,
                               lockstep=lockstep, has_state=has_state)
    consts = [_const_spec((1, W_GROUP)), _const_spec((8, W_BRANCH)), _const_spec((128, 512)), _const_spec((128, W_BRANCH))]
    zspec = pl.BlockSpec((rows, W_GROUP), lambda i: (i, 2))
    scratch = [pltpu.VMEM((rows, W_BRANCH), F32)] * 7
    if has_state:
        srows = seq_per_tile * W_BRANCH
        in_specs = [zspec, pl.BlockSpec((rows, W_GROUP), lambda i: (i, 0)), pl.BlockSpec((srows, HEAD_DIM), lambda i: (i, 0))] + consts
        s_spec = pl.BlockSpec((srows, HEAD_DIM), lambda i: (i, 0))
        args = (z, prev_rows, state_in, mu, vec, wa, g2)
    else:
        in_specs = [zspec] + consts
        s_spec = pl.BlockSpec((W_BRANCH, HEAD_DIM), lambda i: (i // tiles_per_seq, 0))
        scratch = scratch + [pltpu.VMEM((W_BRANCH, W_BRANCH), F32), pltpu.VMEM((8, W_GROUP), F32)]
        args = (z, mu, vec, wa, g2)
    return pl.pallas_call(
        kernel, grid=(steps,), in_specs=in_specs,
        out_specs=[pl.BlockSpec((rows, W_BRANCH), lambda i: (i, 0)), s_spec],
        out_shape=[jax.ShapeDtypeStruct((n, W_BRANCH), F32), jax.ShapeDtypeStruct((b * W_BRANCH, HEAD_DIM), F32)],
        scratch_shapes=scratch, compiler_params=_params(), name="rwkv")(*args)


def _hgrn_lb_kernel(x_ref, o_ref):
    x = x_ref[...]
    depth = x.shape[0]
    e = jnp.exp(x - jnp.max(x, axis=0, keepdims=True))
    soft = e / jnp.sum(e, axis=0, keepdims=True)
    cum = jnp.zeros((1, x.shape[1]), F32)
    for l in range(depth):
        cum = cum + soft[l:l + 1]
        lb = jnp.maximum(cum - soft[0:1], 0.0)
        o_ref[l, 0:1, :] = lb
        o_ref[l, 1:2, :] = jnp.log(jnp.maximum(lb, LB_TINY))
        o_ref[l, 2:3, :] = jnp.log1p(-lb)
        o_ref[l, 3:8, :] = jnp.zeros((5, x.shape[1]), F32)


def _hgrn_lb(hgrn_lb):
    depth, w = hgrn_lb.shape
    return pl.pallas_call(_hgrn_lb_kernel, out_shape=jax.ShapeDtypeStruct((depth, 8, w), F32), name="hgrn_lb")(hgrn_lb)


def _hgrn_blocks(blocks, ones_bd, bd_mask):
    c = blocks[0][0].shape[0]
    tri = _tril(c).astype(BF)
    t_idx = _iota((c, 1), 0)
    each = lambda fn, *lists: [fn(*xs) for xs in zip(*lists)]
    q, k, v, g = (list(x) for x in zip(*blocks))
    b = each(lambda x: _dot_sel_l(tri, x), g)
    b_end = [x[c - 1:c] for x in b]

    def pair_terms(qq, kk, bb):
        xs = []
        for s in range(c):
            e = jnp.exp(jnp.where(t_idx >= s, bb - bb[s:s + 1], NEG))
            xs.append((qq * e * kk[s:s + 1]).astype(BF))
        return jnp.concatenate(xs, axis=0)

    col = each(lambda qq, kk, bb: _dot(pair_terms(qq, kk, bb), ones_bd), q, k, b)

    def inside(cl, vv):
        o = cl[0:c] * vv[0:1]
        for s in range(1, c):
            o = o + cl[s * c:(s + 1) * c] * vv[s:s + 1]
        return o

    o_in = each(inside, col, v)
    qd = each(lambda qq, bb: (qq * jnp.exp(bb)).astype(BF), q, b)
    upd = each(lambda vv, kk, bb, be: jnp.where(bd_mask, _dot_tn(vv.astype(BF), (kk * jnp.exp(be - bb)).astype(BF)), 0.0),
               v, k, b, b_end)
    decay = each(jnp.exp, b_end)
    return list(zip(qd, o_in, decay, upd))


def _hgrn_apply(par, st):
    qd, o_in, decay, upd = par
    return _dot_nt(qd, st.astype(BF)) + o_in, st * decay + upd


def _hgrn_kernel(*refs, seq_len, t_valid, block, has_state):
    if has_state:
        z_ref, s_in_ref, lb_ref, nw_ref, o_ref, s_out_ref, q_s, k_s, v_s, g_s, y_s = refs
    else:
        z_ref, lb_ref, nw_ref, o_ref, s_out_ref, q_s, k_s, v_s, g_s, y_s, state_s = refs
    rows = z_ref.shape[0]
    i = pl.program_id(0)
    t_row = (i * rows + _iota((rows, 1), 0)) % seq_len
    live = t_row < t_valid
    hq, hf, hi, hg = z_ref[:, 0:256], z_ref[:, 256:512], z_ref[:, 512:768], z_ref[:, 768:1024]
    lb, lb_log, l1m = lb_ref[0:1], lb_ref[1:2], lb_ref[2:3]
    ls = _log_sigmoid(hf)
    x2 = l1m + ls
    lae = jnp.maximum(lb_log, x2) + jnp.log1p(jnp.exp(-jnp.abs(lb_log - x2)))
    logf = jnp.where(lb > 0.0, lae, ls)
    q_s[...] = hq * _sigmoid(hq)
    k_s[...] = jnp.where(live, (1.0 - lb) * _sigmoid(-hf), 0.0)
    v_s[...] = hi
    g_s[...] = jnp.where(live, logf, 0.0)
    ones_bd = _block_diag_mask(W_BRANCH, HEAD_DIM).astype(BF)
    bd_mask = _block_diag_mask(W_BRANCH, HEAD_DIM)

    if not has_state:
        @pl.when((i * rows) % seq_len == 0)
        def _():
            state_s[...] = jnp.zeros_like(state_s)

    lockstep = min(HGRN_LOCKSTEP, rows // block)

    def group(gi, carry):
        def rows_of(j, size):
            return pl.ds(pl.multiple_of((gi * lockstep + j) * size, size), size)

        sls = [rows_of(j, block) for j in range(lockstep)]
        pars = _hgrn_blocks([(q_s[sl, :], k_s[sl, :], v_s[sl, :], g_s[sl, :]) for sl in sls], ones_bd, bd_mask)
        if has_state:
            for j, (sl, par) in enumerate(zip(sls, pars)):
                sr = rows_of(j, W_BRANCH)
                o, st_new = _hgrn_apply(par, _to_block_diag(s_in_ref[sr, :]).T)
                y_s[sl, :] = o
                s_out_ref[sr, :] = _from_block_diag(st_new.T)
        else:
            st = state_s[...]
            for sl, par in zip(sls, pars):
                o, st = _hgrn_apply(par, st)
                y_s[sl, :] = o
            state_s[...] = st
        return carry

    lax.fori_loop(0, rows // (block * lockstep), group, 0)
    if not has_state:
        s_out_ref[...] = _from_block_diag(state_s[...].T)
    o = y_s[...]
    ms = _head_sum(o * o, ones_bd) * (1.0 / HEAD_DIM)
    o_ref[...] = o * lax.rsqrt(ms + RMS_EPS) * nw_ref[...] * (hg * _sigmoid(hg))


def _hgrn(z, state_in, lb_rows, norm_w, b, seq_len, t_valid):
    n = z.shape[0]
    has_state = state_in is not None
    rows = min(ROW_TILE, n)
    block = min(HGRN_BLOCK, seq_len)
    seq_per_tile = max(1, rows // seq_len)
    tiles_per_seq = max(1, seq_len // rows)
    kernel = functools.partial(_hgrn_kernel, seq_len=seq_len, t_valid=t_valid, block=block, has_state=has_state)
    zspec = pl.BlockSpec((rows, W_GROUP), lambda i: (i, 3))
    consts = [_const_spec((8, W_BRANCH)), _const_spec((1, W_BRANCH))]
    scratch = [pltpu.VMEM((rows, W_BRANCH), F32)] * 5
    if has_state:
        srows = seq_per_tile * W_BRANCH
        in_specs = [zspec, pl.BlockSpec((srows, HEAD_DIM), lambda i: (i, 0))] + consts
        s_spec = pl.BlockSpec((srows, HEAD_DIM), lambda i: (i, 0))
        args = (z, state_in, lb_rows, norm_w)
    else:
        in_specs = [zspec] + consts
        s_spec = pl.BlockSpec((W_BRANCH, HEAD_DIM), lambda i: (i // tiles_per_seq, 0))
        scratch = scratch + [pltpu.VMEM((W_BRANCH, W_BRANCH), F32)]
        args = (z, lb_rows, norm_w)
    return pl.pallas_call(
        kernel, grid=(n // rows,), in_specs=in_specs,
        out_specs=[pl.BlockSpec((rows, W_BRANCH), lambda i: (i, 0)), s_spec],
        out_shape=[jax.ShapeDtypeStruct((n, W_BRANCH), F32), jax.ShapeDtypeStruct((b * W_BRANCH, HEAD_DIM), F32)],
        scratch_shapes=scratch, compiler_params=_params(), name="hgrn")(*args)


def _merge_kernel(x_ref, oa_ref, ob_ref, oc_ref, od_ref, wg_ref, wb_ref, wo_ref, ln_ref, h_ref, *, alpha):
    x = x_ref[...]
    xb = x.astype(BF)
    m = jnp.zeros(x.shape, F32)
    for n, o_ref in enumerate((oa_ref, ob_ref, oc_ref, od_ref)):
        gate = _sigmoid(_dot(xb, wg_ref[:, n * D_MODEL:(n + 1) * D_MODEL]))
        m = m + gate * _dot(o_ref[...].astype(BF), wb_ref[n])
    mix = _dot(m.astype(BF), wo_ref[...])
    h_ref[...] = _layer_norm(alpha * x + mix, ln_ref[0:1], ln_ref[1:2])


def _merge(x, oa, ob, oc, od, wg, wb, wo, ln, alpha):
    n = x.shape[0]
    tm = min(ROW_TILE, n)
    row = lambda w: pl.BlockSpec((tm, w), lambda i: (i, 0))
    return pl.pallas_call(
        functools.partial(_merge_kernel, alpha=alpha), grid=(n // tm,),
        in_specs=[row(D_MODEL), row(256), row(256), row(256), row(256), _const_spec(wg.shape), _const_spec(wb.shape),
                  _const_spec(wo.shape), _const_spec((8, D_MODEL))],
        out_specs=row(D_MODEL), out_shape=jax.ShapeDtypeStruct((n, D_MODEL), F32),
        compiler_params=_params(), name="merge")(x, oa, ob, oc, od, wg, wb, wo, ln)


def _ffn_kernel(*refs, seq_len, has_state, n_split, alpha):
    if has_state:
        h_ref, p1_ref, p2_ref, wup_ref, wdn_ref, cv_ref, ln_ref, y_ref, a_ref = refs
    else:
        h_ref, wup_ref, wdn_ref, cv_ref, ln_ref, y_ref, a_ref, last_s = refs
    rows = h_ref.shape[0]
    i = pl.program_id(0)
    h = h_ref[...]
    hb = h.astype(BF)
    ridx = _iota((rows, 1), 0)
    t_row = (i * rows + ridx) % seq_len
    wf = D_FF // n_split
    f = jnp.zeros((rows, D_MODEL), F32)
    if not has_state:
        @pl.when((i * rows) % seq_len == 0)
        def _():
            last_s[...] = jnp.zeros_like(last_s)
    for j in range(n_split):
        lo, hi = j * wf, (j + 1) * wf
        a = _dot(hb, wup_ref[:, lo:hi])
        gt = _dot(hb, wup_ref[:, D_FF + lo:D_FF + hi])
        r1 = pltpu.roll(a, 1, 0)
        r2 = pltpu.roll(a, 2, 0)
        if has_state:
            prev1 = jnp.where(t_row == 0, p1_ref[:, lo:hi], r1)
            prev2 = jnp.where(t_row < 2, p2_ref[:, lo:hi], r2)
            a_ref[:, lo:hi] = a
        else:
            c6 = last_s[6:7, lo:hi]
            c7 = last_s[7:8, lo:hi]
            prev1 = jnp.where(ridx == 0, c7, r1)
            prev2 = jnp.where(ridx == 0, c6, jnp.where(ridx == 1, c7, r2))
            last_s[:, lo:hi] = a[rows - 8:rows]
            a_ref[:, lo:hi] = a[rows - 8:rows]
        conv = cv_ref[3:4, lo:hi] + prev2 * cv_ref[0:1, lo:hi] + prev1 * cv_ref[1:2, lo:hi] + a * cv_ref[2:3, lo:hi]
        hid = _gelu(conv) * gt
        f = f + _dot(hid.astype(BF), wdn_ref[lo:hi, :])
    y_ref[...] = _layer_norm(alpha * h + f, ln_ref[0:1], ln_ref[1:2])


def _ffn(h, p1, p2, wup, wdn, cv, ln, seq_len, alpha):
    n = h.shape[0]
    has_state = p1 is not None
    tm = min(ROW_TILE, n)
    row = lambda w: pl.BlockSpec((tm, w), lambda i: (i, 0))
    kernel = functools.partial(_ffn_kernel, seq_len=seq_len, has_state=has_state, n_split=2, alpha=alpha)
    consts = [_const_spec(wup.shape), _const_spec(wdn.shape), _const_spec((8, D_FF)), _const_spec((8, D_MODEL))]
    if has_state:
        in_specs = [row(D_MODEL), row(D_FF), row(D_FF)] + consts
        a_spec, a_rows, scratch = row(D_FF), n, []
        args = (h, p1, p2, wup, wdn, cv, ln)
    else:
        in_specs = [row(D_MODEL)] + consts
        a_spec, a_rows = pl.BlockSpec((8, D_FF), lambda i: (i, 0)), (n // tm) * 8
        scratch = [pltpu.VMEM((8, D_FF), F32)]
        args = (h, wup, wdn, cv, ln)
    return pl.pallas_call(
        kernel, grid=(n // tm,), in_specs=in_specs, out_specs=[row(D_MODEL), a_spec],
        out_shape=[jax.ShapeDtypeStruct((n, D_MODEL), F32), jax.ShapeDtypeStruct((a_rows, D_FF), F32)],
        scratch_shapes=scratch, compiler_params=_params(), name="ffn")(*args)


def _pad_lanes(x, width):
    return jnp.pad(x, [(0, 0)] * (x.ndim - 1) + [(0, width - x.shape[-1])])


def _regroup_w_in(w_in):
    o = 0
    cols = {}
    for name, w in (('fox_q', 256), ('fox_k', 256), ('fox_v', 256), ('fox_f', 4), ('dsa_q', 256), ('dsa_k', 256),
                    ('dsa_v', 256), ('idx_q', 128), ('idx_k', 32), ('idx_w', 4), ('rwkv', 1024), ('hgrn', 1024),
                    ('gate', 4096)):
        cols[name] = w_in[..., o:o + w]
        o += w
    g0 = _pad_lanes(jnp.concatenate([cols['fox_k'], cols['fox_v'], cols['fox_q'], cols['fox_f']], -1), W_GROUP)
    g1 = _pad_lanes(jnp.concatenate([cols['dsa_k'], cols['dsa_v'], cols['dsa_q'], cols['idx_q'], cols['idx_k'],
                                     cols['idx_w']], -1), W_GROUP)
    wz = jnp.concatenate([g0, g1, cols['rwkv'], cols['hgrn']], -1).astype(BF)
    return wz, cols['gate'].astype(BF)


def _rows8(*vecs, width):
    rows = [v.reshape(1, width) for v in vecs]
    rows.append(jnp.zeros((8 - len(rows), width), F32))
    return jnp.concatenate(rows, axis=0)


def _expand_first_rows(state, tt, offsets):
    b, _, w = state.shape
    rows = dict((ts, s) for s, ts in offsets)
    zero = jnp.zeros((b, 1, w), state.dtype)
    out = jnp.concatenate([state[:, rows[t]:rows[t] + 1] if t in rows else zero for t in range(tt)], axis=1)
    return out.reshape(b * tt, w)


def _layer(x, cfg, lw):
    b, tt, tv, past = cfg['b'], cfg['tt'], cfg['tv'], cfg['past']
    decode = past > 0
    z = _inproj(x, lw['wz'])
    topk = max(1, min(DSA_TOPK, (past + tv) // 4))
    new = {}
    if decode:
        o_a, logf = _fox_decode(z, lw['fox_bf'], cfg['fox_kv'], cfg['fox_lf'], cfg['page_table'], cfg['layer'], b, tt)
        new['fox_kv'] = z[:, 0:512]
    else:
        qt_a, k_a, vt_a, new['fox_kv'], logf, ccol, crow = _fox_prep(z, lw['fox_bf'], b, tt)
        o_a = _fox_attn(qt_a, k_a, vt_a, ccol, crow, b, tt)
    new['fox_logf'] = logf[:, 0:N_HEAD]
    if decode:
        kv_b, misc_b, q_b, qi_b = _dsa_prep(z, cfg['tab_k'], cfg['tab_i'], False)
        keep = _dsa_select(qi_b, misc_b, cfg['dsa_ki'], cfg['page_table'], cfg['layer'], b, tt, topk, tv)
        o_b = _dsa_decode(q_b, keep, kv_b, cfg['dsa_kv'], cfg['page_table'], cfg['layer'], b, tt)
    else:
        kv_b, misc_b, qt_b, k_b, vt_b, qit_b, wt_b, ki4_b = _dsa_prep(z, cfg['tab_k'], cfg['tab_i'], True)
        o_b = _dsa_attn(qt_b, k_b, vt_b, qit_b, wt_b, ki4_b, b, tt, topk)
    new['dsa_kv'] = kv_b
    new['dsa_kidx'] = misc_b[:, 0:D_IDX]
    o_c, new['rwkv'] = _rwkv(z, cfg.get('shift_rows'), cfg.get('rwkv_state'), lw['rwkv_mu'], lw['rwkv_vec'], lw['rwkv_wa'],
                             lw['rwkv_g2'], b, tt, tv)
    new['shift'] = z.reshape(b, tt, N_GROUP * W_GROUP)[:, tv - 1, 2 * W_GROUP:3 * W_GROUP]
    o_d, new['hgrn'] = _hgrn(z, cfg.get('hgrn_state'), lw['hgrn_lb'], lw['hgrn_nw'], b, tt, tv)
    h = _merge(x, o_a, o_b, o_c, o_d, lw['wg'], lw['wb'], lw['wo'], lw['ln1'], cfg['alpha'])
    y, new['conv'] = _ffn(h, cfg.get('conv_p1'), cfg.get('conv_p2'), lw['wup'], lw['wdn'], lw['conv'], lw['ln2'], tt, cfg['alpha'])
    return y, new


def kernel(x_prompt, x_sample, cache_fox_kv, cache_fox_logf, cache_dsa_kv, cache_dsa_kidx, state_rwkv, state_rwkv_shift, state_hgrn, state_ffn_conv, page_table, w_in, fox_bf, rwkv_mu, rwkv_w0, rwkv_w2, rwkv_a0, rwkv_a2, rwkv_g2, rwkv_kk, rwkv_ka, rwkv_rk, rwkv_ln_w, rwkv_ln_b, hgrn_lb, hgrn_norm_w, w_branch, w_o, ln1_g, ln1_b, ln2_g, ln2_b, ffn_w_in, ffn_conv_w, ffn_conv_b, ffn_w_out):
    depth = w_in.shape[0]
    bp, tp, d = x_prompt.shape
    bs, ts, _ = x_sample.shape
    n_pool, page = cache_fox_kv.shape[1], cache_fox_kv.shape[2]
    n_pages = page_table.shape[1]
    past = n_pages * page
    tsp = -(-ts // T_ALIGN) * T_ALIGN
    assert d == D_MODEL and tp % min(ROW_TILE, tp) == 0 and (bs * tsp) % min(ROW_TILE, bs * tsp) == 0
    assert tsp <= 128 and ts >= 2

    wz_all, wg_all = _regroup_w_in(w_in)
    wb_all, wo_all = w_branch.astype(BF), w_o.astype(BF)
    wup_all, wdn_all = ffn_w_in.astype(BF), ffn_w_out.astype(BF)
    zero_l = jnp.zeros((depth, LORA_W, W_BRANCH), F32)
    wa_all = jnp.concatenate([jnp.concatenate([rwkv_w2, zero_l], 2), jnp.concatenate([zero_l, rwkv_a2], 2)], 1).astype(BF)
    g2_all = rwkv_g2.astype(BF)
    lb_all = _hgrn_lb(hgrn_lb)
    page_table = page_table.astype(I32)

    fox_kv_pages = jnp.transpose(cache_fox_kv, (0, 1, 3, 4, 5, 2)).reshape(depth, n_pool, 2 * W_BRANCH, page)
    dsa_kv_pages = jnp.transpose(cache_dsa_kv, (0, 1, 3, 4, 5, 2)).reshape(depth, n_pool, 2 * W_BRANCH, page)
    dsa_ki_pages = jnp.swapaxes(cache_dsa_kidx, 2, 3)
    fox_lf_pages = _pad_rows_nd(jnp.swapaxes(cache_fox_logf, 2, 3), 8)

    pos_p = jnp.arange(tp)
    pos_s = past + (jnp.arange(bs * tsp) % tsp)[:min(ROW_TILE, bs * tsp)]
    alpha = (2 * depth) ** 0.25
    cfg_p = dict(b=bp, tt=tp, tv=tp, past=0, alpha=alpha,
                 tab_k=_rope_tables(pos_p, 256, HEAD_DIM, ROPE_HALF_QK), tab_i=_rope_tables(pos_p, 128, D_IDX, ROPE_HALF_IDX))
    cfg_s = dict(b=bs, tt=tsp, tv=ts, past=past, alpha=alpha, page_table=page_table, fox_kv=fox_kv_pages, fox_lf=fox_lf_pages,
                 dsa_kv=dsa_kv_pages, dsa_ki=dsa_ki_pages,
                 tab_k=_rope_tables(pos_s, 256, HEAD_DIM, ROPE_HALF_QK), tab_i=_rope_tables(pos_s, 128, D_IDX, ROPE_HALF_IDX))

    xp = x_prompt.reshape(bp * tp, d)
    xs = jnp.pad(x_sample, ((0, 0), (0, tsp - ts), (0, 0))).reshape(bs * tsp, d)
    new_p, new_s = [], []
    for l in range(depth):
        lw = dict(wz=wz_all[l], wg=wg_all[l], wb=wb_all[l], wo=wo_all[l], wup=wup_all[l], wdn=wdn_all[l],
                  fox_bf=_pad_lanes(fox_bf[l][None], 128), rwkv_mu=rwkv_mu[l][None],
                  rwkv_vec=_rows8(rwkv_w0[l], rwkv_a0[l], rwkv_kk[l], rwkv_ka[l], rwkv_rk[l], rwkv_ln_w[l], rwkv_ln_b[l],
                                  width=W_BRANCH),
                  rwkv_wa=wa_all[l], rwkv_g2=g2_all[l], hgrn_lb=lb_all[l], hgrn_nw=hgrn_norm_w[l][None],
                  ln1=_rows8(ln1_g[l], ln1_b[l], width=D_MODEL), ln2=_rows8(ln2_g[l], ln2_b[l], width=D_MODEL),
                  conv=_rows8(ffn_conv_w[l, 0], ffn_conv_w[l, 1], ffn_conv_w[l, 2], ffn_conv_b[l], width=D_FF))
        xp, st_p = _layer(xp, cfg_p, lw)
        cfg_l = dict(cfg_s, layer=l,
                     shift_rows=_expand_first_rows(state_rwkv_shift[l][:, None], tsp, ((0, 0),)),
                     rwkv_state=state_rwkv[l].reshape(bs * W_BRANCH, HEAD_DIM),
                     hgrn_state=state_hgrn[l].reshape(bs * W_BRANCH, HEAD_DIM),
                     conv_p1=_expand_first_rows(state_ffn_conv[l], tsp, ((1, 0),)),
                     conv_p2=_expand_first_rows(state_ffn_conv[l], tsp, ((0, 0), (1, 1))))
        xs, st_s = _layer(xs, cfg_l, lw)
        new_p.append(st_p)
        new_s.append(st_s)

    def assemble(new, b, tt, tv, decode):
        def rows(name, shape):
            a = jnp.stack([n[name] for n in new]).reshape(depth, b, tt, -1)[:, :, :tv]
            return a.reshape((depth, b, tv) + shape)

        fox_kv = rows('fox_kv', (2, N_HEAD, HEAD_DIM))
        fox_logf = rows('fox_logf', (N_HEAD,))
        dsa_kv = rows('dsa_kv', (2, N_HEAD, HEAD_DIM))
        dsa_kidx = rows('dsa_kidx', (D_IDX,))
        rwkv = jnp.stack([n['rwkv'] for n in new]).reshape(depth, b, N_HEAD, HEAD_DIM, HEAD_DIM)
        hgrn = jnp.stack([n['hgrn'] for n in new]).reshape(depth, b, N_HEAD, HEAD_DIM, HEAD_DIM)
        shift = jnp.stack([n['shift'] for n in new])
        conv = jnp.stack([n['conv'] for n in new])
        if decode:
            conv = conv.reshape(depth, b, tt, D_FF)[:, :, tv - 2:tv]
        else:
            conv = conv.reshape(depth, b, -1, 8, D_FF)[:, :, -1, 6:8]
        return fox_kv, fox_logf, dsa_kv, dsa_kidx, rwkv, shift, hgrn, conv

    y_p = xp.reshape(bp, tp, d)
    y_s = xs.reshape(bs, tsp, d)[:, :ts]
    return (y_p, y_s) + assemble(new_p, bp, tp, tp, False) + assemble(new_s, bs, tsp, ts, True)


def _pad_rows_nd(x, rows):
    pad = [(0, 0)] * x.ndim
    pad[-2] = (0, rows - x.shape[-2])
    return jnp.pad(x, pad)
```

```python
import functools
import math

import jax
import jax.numpy as jnp
import numpy as np
from jax import lax
from jax.experimental import pallas as pl
from jax.experimental.pallas import tpu as pltpu

F32 = jnp.float32
BF = jnp.bfloat16
I32 = jnp.int32
I16 = jnp.int16

D_MODEL = 1024
N_HEAD = 4
HEAD_DIM = 64
W_BRANCH = N_HEAD * HEAD_DIM
D_IDX = 32
W_IDX = N_HEAD * D_IDX
LORA_W = 64
LORA_A = 64
LORA_G = 128
D_FF = 2816
N_GROUP = 4
W_GROUP = 1024
DSA_TOPK = 256
ROPE_THETA = 500000.0
ROPE_HALF_QK = 8
ROPE_HALF_IDX = 4
RWKV_GN_EPS = 64e-5
RMS_EPS = 1e-6
LB_TINY = 1e-30
LN_EPS = 1e-5
NEG = -1e30
ROW_TILE = 256
KEY_CHUNK = 512
SELECT_SEQS = 16
DECODE_SEQS = 4
RWKV_CHUNK = 64
RWKV_LOCKSTEP = 8
HGRN_BLOCK = 16
HGRN_LOCKSTEP = 8
T_ALIGN = 8
INT_MIN = -2 ** 31


def _dot(a, b):
    return jnp.dot(a, b, preferred_element_type=F32)


def _dot_nt(a, b):
    return lax.dot_general(a, b, (((1,), (1,)), ((), ())), preferred_element_type=F32)


def _dot_tn(a, b):
    return lax.dot_general(a, b, (((0,), (0,)), ((), ())), preferred_element_type=F32)


def _split3(x):
    hi = x.astype(BF)
    r = x - hi.astype(F32)
    mid = r.astype(BF)
    lo = (r - mid.astype(F32)).astype(BF)
    return hi, mid, lo


def _dot_sel_l(sel_bf, x):
    hi, mid, lo = _split3(x)
    return _dot(sel_bf, hi) + _dot(sel_bf, mid) + _dot(sel_bf, lo)


def _dot_sel_r(x, sel_bf):
    hi, mid, lo = _split3(x)
    return _dot(hi, sel_bf) + _dot(mid, sel_bf) + _dot(lo, sel_bf)


def _dot_sel_tn(x, sel_bf):
    hi, mid, lo = _split3(x)
    return _dot_tn(hi, sel_bf) + _dot_tn(mid, sel_bf) + _dot_tn(lo, sel_bf)


def _iota(shape, dim):
    return lax.broadcasted_iota(I32, shape, dim)


def _tril(n, strict=False):
    r, c = _iota((n, n), 0), _iota((n, n), 1)
    return (r > c) if strict else (r >= c)


def _head_lane(width, head_width):
    return _iota((1, width), 1) // head_width


def _block_diag_mask(n, block):
    return (_iota((n, n), 0) // block) == (_iota((n, n), 1) // block)


def _stack_heads(x, head_width=HEAD_DIM):
    hl = _head_lane(x.shape[1], head_width)
    return jnp.concatenate([jnp.where(hl == h, x, jnp.zeros_like(x)) for h in range(N_HEAD)], axis=0)


def _unstack_heads(xs):
    c = xs.shape[0] // N_HEAD
    out = xs[0:c]
    for h in range(1, N_HEAD):
        out = out + xs[h * c:(h + 1) * c]
    return out


def _head_sum(x, ones_bd):
    return _dot_sel_r(x, ones_bd)


def _sigmoid(x):
    return 1.0 / (1.0 + jnp.exp(-x))


def _log_sigmoid(x):
    return jnp.minimum(x, 0.0) - jnp.log1p(jnp.exp(-jnp.abs(x)))


def _softplus(x):
    return jnp.maximum(x, 0.0) + jnp.log1p(jnp.exp(-jnp.abs(x)))


def _gelu(x):
    return 0.5 * x * (1.0 + lax.erf(x * (2.0 ** -0.5)))


def _layer_norm(x, g, b):
    mu = jnp.mean(x, axis=-1, keepdims=True)
    xc = x - mu
    var = jnp.mean(xc * xc, axis=-1, keepdims=True)
    return xc * lax.rsqrt(var + LN_EPS) * g + b


def _params(n_axes=1):
    return pltpu.CompilerParams(dimension_semantics=("arbitrary",) * n_axes)


def _const_spec(shape):
    nd = len(shape)
    return pl.BlockSpec(shape, lambda *_: (0,) * nd)


def _inproj_kernel(x_ref, w_ref, z_ref):
    z_ref[...] = _dot(x_ref[...].astype(BF), w_ref[...])


def _inproj(x, wz):
    n = x.shape[0]
    tm = min(ROW_TILE, n)
    nz = wz.shape[1]
    return pl.pallas_call(
        _inproj_kernel, grid=(n // tm,),
        in_specs=[pl.BlockSpec((tm, D_MODEL), lambda i: (i, 0)), _const_spec((D_MODEL, nz))],
        out_specs=pl.BlockSpec((tm, nz), lambda i: (i, 0)),
        out_shape=jax.ShapeDtypeStruct((n, nz), F32), compiler_params=_params(), name="inproj")(x, wz)


def _fox_prep_kernel(z_ref, bf_ref, qt_ref, k_ref, vt_ref, kv_ref, logf_ref, ccol_ref, crow_ref):
    t = z_ref.shape[0]
    ck = vt_ref.shape[2]
    tq = crow_ref.shape[2]
    kv_ref[...] = z_ref[:, 0:512]
    k_ref[...] = z_ref[:, 0:256].astype(BF)
    qt_ref[...] = (z_ref[:, 512:768] * (HEAD_DIM ** -0.5)).T.astype(BF)
    logf = _log_sigmoid(z_ref[:, 768:896] + bf_ref[...])
    logf_ref[...] = logf
    tri = _tril(ck).astype(BF)
    carry = jnp.zeros((1, 128), F32)
    for c in range(t // ck):
        rows = slice(c * ck, (c + 1) * ck)
        vt_ref[c] = z_ref[rows, 256:512].T.astype(BF)
        cum = _dot_sel_l(tri, logf[rows]) + carry
        carry = cum[ck - 1:ck]
        ccol_ref[rows, :] = cum
        cum_t = cum.T[0:8]
        for j in range(ck // tq):
            crow_ref[c * (ck // tq) + j] = cum_t[:, j * tq:(j + 1) * tq]


def _fox_prep(z, bf_row, b, t):
    ck = min(KEY_CHUNK, t)
    tq = min(ROW_TILE, t)
    nc, nq = t // ck, t // tq
    rows = lambda w: pl.BlockSpec((t, w), lambda i: (i, 0))
    return pl.pallas_call(
        _fox_prep_kernel, grid=(b,),
        in_specs=[rows(W_GROUP), _const_spec((1, 128))],
        out_specs=[pl.BlockSpec((W_BRANCH, t), lambda i: (0, i)), rows(W_BRANCH),
                   pl.BlockSpec((nc, W_BRANCH, ck), lambda i: (i, 0, 0)), rows(512), rows(128), rows(128),
                   pl.BlockSpec((None, nq, 8, tq), lambda i: (i, 0, 0, 0))],
        out_shape=[jax.ShapeDtypeStruct((W_BRANCH, b * t), BF), jax.ShapeDtypeStruct((b * t, W_BRANCH), BF),
                   jax.ShapeDtypeStruct((b * nc, W_BRANCH, ck), BF), jax.ShapeDtypeStruct((b * t, 512), F32),
                   jax.ShapeDtypeStruct((b * t, 128), F32), jax.ShapeDtypeStruct((b * t, 128), F32),
                   jax.ShapeDtypeStruct((b, nq, 8, tq), F32)],
        compiler_params=_params(), name="fox_prep")(z, bf_row)


def _flash_init(m_s, l_s, acc_s):
    m_s[...] = jnp.full(m_s.shape, NEG, F32)
    l_s[...] = jnp.zeros(l_s.shape, F32)
    acc_s[...] = jnp.zeros(acc_s.shape, F32)


def _flash_update(sts, vts, m_s, l_s, acc_s):
    scaled = []
    for h, st in enumerate(sts):
        m_old = m_s[h]
        m_new = jnp.maximum(m_old, jnp.max(st, axis=0, keepdims=True))
        a = jnp.exp(m_old - m_new)
        p = jnp.exp(st - m_new)
        l_s[h] = a * l_s[h] + jnp.sum(p, axis=0, keepdims=True)
        m_s[h] = m_new
        scaled.append((a, p.astype(BF)))
    for h, (a, p) in enumerate(scaled):
        acc_s[h] = a * acc_s[h] + _dot(vts[h], p)


def _flash_result(l_s, acc_s):
    return jnp.concatenate([acc_s[h] / l_s[h] for h in range(N_HEAD)], axis=0).T


def _flash_scratch(tq):
    return [pltpu.VMEM((N_HEAD, 1, tq), F32), pltpu.VMEM((N_HEAD, 1, tq), F32), pltpu.VMEM((N_HEAD, HEAD_DIM, tq), F32)]


def _head_rows(x, head_rows):
    rh = _iota((x.shape[0], 1), 0) // head_rows
    return [jnp.where(rh == h, x, jnp.zeros_like(x)) for h in range(N_HEAD)]


def _causal_chunks(i, tq, ck):
    return (i * tq) // ck + 1


def _fox_attn_kernel(qt_ref, k_ref, vt_ref, ccol_ref, crow_ref, o_ref, m_s, l_s, acc_s):
    tq = qt_ref.shape[1]
    ck = vt_ref.shape[2]
    i = pl.program_id(1)
    qpos = i * tq + _iota((1, tq), 1)
    qth = _head_rows(qt_ref[...], HEAD_DIM)
    cq = [crow_ref[h:h + 1, :] for h in range(N_HEAD)]
    _flash_init(m_s, l_s, acc_s)

    def chunk(c, carry):
        rows = pl.ds(pl.multiple_of(c * ck, ck), ck)
        k = k_ref[rows, :]
        ccol = ccol_ref[rows, :]
        visible = (c * ck + _iota((ck, 1), 0)) <= qpos
        sts = [jnp.where(visible, _dot(k, qth[h]) + cq[h] - ccol[:, h:h + 1], NEG) for h in range(N_HEAD)]
        vts = [vt_ref[c, h * HEAD_DIM:(h + 1) * HEAD_DIM, :] for h in range(N_HEAD)]
        _flash_update(sts, vts, m_s, l_s, acc_s)
        return carry

    lax.fori_loop(0, _causal_chunks(i, tq, ck), chunk, 0)
    o_ref[...] = _flash_result(l_s, acc_s)


def _fox_attn(qt, k, vt, ccol, crow, b, t):
    tq = crow.shape[3]
    nq = t // tq
    ck = vt.shape[2]
    nc = t // ck
    return pl.pallas_call(
        _fox_attn_kernel, grid=(b, nq),
        in_specs=[pl.BlockSpec((W_BRANCH, tq), lambda bi, i: (0, bi * nq + i)),
                  pl.BlockSpec((t, W_BRANCH), lambda bi, i: (bi, 0)),
                  pl.BlockSpec((nc, W_BRANCH, ck), lambda bi, i: (bi, 0, 0)),
                  pl.BlockSpec((t, 128), lambda bi, i: (bi, 0)),
                  pl.BlockSpec((None, None, 8, tq), lambda bi, i: (bi, i, 0, 0))],
        out_specs=pl.BlockSpec((tq, W_BRANCH), lambda bi, i: (bi * nq + i, 0)),
        out_shape=jax.ShapeDtypeStruct((b * t, W_BRANCH), F32), scratch_shapes=_flash_scratch(tq),
        compiler_params=_params(2), name="fox_attn")(qt, k, vt, ccol, crow)


def _page_specs(n_pages, block, layer, n_seq=1):
    nd = len(block)

    def mk(j, p):
        return pl.BlockSpec((None, None) + block, lambda g, pt: (layer, pt[g * n_seq + j, p]) + (0,) * nd)

    return [mk(j, p) for j in range(n_seq) for p in range(n_pages)]


def _softmax_rows(s):
    m = jnp.max(s, axis=1, keepdims=True)
    p = jnp.exp(s - m)
    return p.astype(BF), jnp.sum(p, axis=1, keepdims=True)


def _weighted_values(pb, l, vt_past, v_new, past):
    o = _dot_nt(pb[:, :past], vt_past) + _dot(pb[:, past:], v_new)
    return _unstack_heads(_stack_mask(o / l))


def _cat_pages(refs, lo, hi):
    return jnp.concatenate([r[lo:hi, :] for r in refs], axis=1)


def _stack_mask(o):
    c = o.shape[0] // N_HEAD
    hl = _head_lane(o.shape[1], HEAD_DIM)
    row_h = _iota((o.shape[0], 1), 0) // c
    return jnp.where(row_h == hl, o, 0.0)


def _pad_rows(x, rows):
    return jnp.concatenate([x, jnp.zeros((rows - x.shape[0], x.shape[1]), x.dtype)], axis=0)


def _fox_decode_kernel(pt_ref, z_ref, bf_ref, *refs, n_pages, n_seq):
    o_ref, logf_ref = refs[2 * n_seq * n_pages:]
    tt = z_ref.shape[0] // n_seq
    past = n_pages * refs[0].shape[1]
    parts = [_fox_decode_scores(z_ref, bf_ref, logf_ref, slice(j * tt, (j + 1) * tt),
                                refs[j * n_pages:(j + 1) * n_pages],
                                refs[(n_seq + j) * n_pages:(n_seq + j + 1) * n_pages]) for j in range(n_seq)]
    probs = [_softmax_rows(s) for s, _, _ in parts]
    for j, ((pb, l), (_, vt_past, v_new)) in enumerate(zip(probs, parts)):
        o_ref[j * tt:(j + 1) * tt, :] = _weighted_values(pb, l, vt_past, v_new, past)


def _fox_decode_scores(z_ref, bf_ref, logf_ref, rows, kv_refs, lf_refs):
    tt = rows.stop - rows.start
    n_pages = len(kv_refs)
    page = kv_refs[0].shape[1]
    k_new = z_ref[rows, 0:256]
    v_new = z_ref[rows, 256:512]
    q = z_ref[rows, 512:768] * (HEAD_DIM ** -0.5)
    logf = _log_sigmoid(z_ref[rows, 768:896] + bf_ref[...])
    logf_ref[rows, :] = logf

    lf = jnp.concatenate([r[...] for r in lf_refs], axis=0)
    n = lf.shape[0]
    tri_u = (_iota((page, page), 0) <= _iota((page, page), 1)).astype(BF)
    in_page = _dot_sel_r(lf, tri_u)
    tot = jnp.broadcast_to(in_page[:, page - 1:page], (n, page))
    r, c = _iota((n, n), 0), _iota((n, n), 1)
    later = jnp.where(((r % 8) == (c % 8)) & ((c // 8) >= (r // 8)), -1.0, 0.0).astype(BF)
    ck_rel = in_page + _dot_sel_l(later, tot)

    cn_col = _dot_sel_l(_tril(tt).astype(BF), logf)
    tri_pad = (_iota((tt, 128), 0) <= _iota((tt, 128), 1)).astype(BF)
    cn_row = _dot_sel_tn(logf, tri_pad)

    qbd = _stack_heads(q).astype(BF)
    s_past = _dot(qbd, _cat_pages(kv_refs, 0, W_BRANCH).astype(BF))
    vt_past = _cat_pages(kv_refs, W_BRANCH, 2 * W_BRANCH).astype(BF)
    s_new = _dot_nt(qbd, _pad_rows(k_new, 128).astype(BF))

    cq = jnp.concatenate([cn_col[:, h:h + 1] for h in range(N_HEAD)], axis=0)
    bias_past = jnp.concatenate(
        [jnp.concatenate([jnp.broadcast_to(ck_rel[p * 8 + h:p * 8 + h + 1], (tt, page)) for h in range(N_HEAD)], axis=0)
         for p in range(n_pages)], axis=1)
    bias_new = jnp.concatenate([jnp.broadcast_to(cn_row[h:h + 1], (tt, 128)) for h in range(N_HEAD)], axis=0)
    tq = _iota((N_HEAD * tt, 128), 0) % tt
    ok_new = _iota((N_HEAD * tt, 128), 1) <= tq
    s = jnp.concatenate([s_past + cq - bias_past, jnp.where(ok_new, s_new + cq - bias_new, NEG)], axis=1)
    return s, vt_past, _pad_rows(v_new, 128).astype(BF)


def _fox_decode(z, bf_row, cache_kv, cache_lf, page_table, layer, b, tt):
    n_pages = page_table.shape[1]
    page = cache_kv.shape[3]
    n_seq = min(DECODE_SEQS, b)
    kernel = functools.partial(_fox_decode_kernel, n_pages=n_pages, n_seq=n_seq)
    rows = lambda w: pl.BlockSpec((n_seq * tt, w), lambda i, pt: (i, 0))
    grid_spec = pltpu.PrefetchScalarGridSpec(
        num_scalar_prefetch=1, grid=(b // n_seq,),
        in_specs=[rows(W_GROUP), pl.BlockSpec((1, 128), lambda i, pt: (0, 0))]
        + _page_specs(n_pages, (2 * W_BRANCH, page), layer, n_seq) + _page_specs(n_pages, (8, page), layer, n_seq),
        out_specs=[rows(W_BRANCH), rows(128)])
    return pl.pallas_call(
        kernel, grid_spec=grid_spec,
        out_shape=[jax.ShapeDtypeStruct((b * tt, W_BRANCH), F32), jax.ShapeDtypeStruct((b * tt, 128), F32)],
        compiler_params=_params(), name="fox_decode")(page_table, z, bf_row, *([cache_kv] * (n_seq * n_pages)),
                                                      *([cache_lf] * (n_seq * n_pages)))


def _rope_tables(pos, n_lanes, head_width, half):
    inv = jnp.power(ROPE_THETA, -jnp.arange(half, dtype=F32) / half)
    ang = pos.astype(F32)[:, None] * inv[None]
    d = np.arange(n_lanes) % head_width
    first = jnp.asarray(d < half)[None]
    second = jnp.asarray((d >= half) & (d < 2 * half))[None]
    cos = jnp.cos(ang)[:, d % half]
    sin = jnp.sin(ang)[:, d % half]
    c = jnp.where(first | second, cos, 1.0)
    s_up = jnp.where(first, -sin, 0.0)
    s_dn = jnp.where(second, sin, 0.0)
    return jnp.concatenate([c, s_up, s_dn], axis=1).astype(F32)


def _rope(x, tab, half):
    w = x.shape[1]
    return x * tab[:, 0:w] + pltpu.roll(x, w - half, 1) * tab[:, w:2 * w] + pltpu.roll(x, half, 1) * tab[:, 2 * w:3 * w]


def _dsa_prep_kernel(z_ref, tk_ref, ti_ref, kv_ref, misc_ref, *out_refs, prefill):
    tk = tk_ref[...]
    ti = ti_ref[...]
    k = _rope(z_ref[:, 0:256], tk, ROPE_HALF_QK)
    v = z_ref[:, 256:512]
    q = _rope(z_ref[:, 512:768], tk, ROPE_HALF_QK) * (HEAD_DIM ** -0.5)
    qi = _rope(z_ref[:, 768:896], ti, ROPE_HALF_IDX)
    tail = z_ref[:, 896:1024]
    ki = _rope(tail, ti, ROPE_HALF_IDX)
    kv_ref[...] = jnp.concatenate([k, v], axis=1)
    lane = _iota((1, 128), 1)
    misc = jnp.where(lane < D_IDX, ki, tail * (N_HEAD ** -0.5))
    misc_ref[...] = misc
    if prefill:
        qt_ref, k_ref, vt_ref, qit_ref, wt_ref, ki4_ref = out_refs
        qt_ref[...] = q.T.astype(BF)
        k_ref[...] = k.astype(BF)
        vt_ref[...] = v.T.astype(BF)
        qit_ref[...] = qi.T.astype(BF)
        wt_ref[...] = misc.T[D_IDX:D_IDX + 8]
        kim = jnp.where(lane < D_IDX, ki, 0.0)
        ki4 = kim + pltpu.roll(kim, 32, 1) + pltpu.roll(kim, 64, 1) + pltpu.roll(kim, 96, 1)
        ki4_ref[...] = ki4.astype(BF)
    else:
        q_ref, qi_ref = out_refs
        q_ref[...] = q
        qi_ref[...] = qi


def _dsa_prep(z, tab_k, tab_i, prefill):
    n = z.shape[0]
    tm = min(ROW_TILE, n)
    nt = tab_k.shape[0] // tm
    row = lambda w: pl.BlockSpec((tm, w), lambda i: (i, 0))
    col = lambda w: pl.BlockSpec((w, tm), lambda i: (0, i))
    out_specs = [row(512), row(128)]
    out_shape = [jax.ShapeDtypeStruct((n, 512), F32), jax.ShapeDtypeStruct((n, 128), F32)]
    if prefill:
        ck = 2 * tm
        out_specs += [col(256), row(256), pl.BlockSpec((None, W_BRANCH, tm), lambda i: (i // 2, 0, i % 2)),
                      col(128), col(8), row(128)]
        out_shape += [jax.ShapeDtypeStruct((256, n), BF), jax.ShapeDtypeStruct((n, 256), BF),
                      jax.ShapeDtypeStruct((n // ck, W_BRANCH, ck), BF), jax.ShapeDtypeStruct((128, n), BF),
                      jax.ShapeDtypeStruct((8, n), F32), jax.ShapeDtypeStruct((n, 128), BF)]
    else:
        out_specs += [row(256), row(128)]
        out_shape += [jax.ShapeDtypeStruct((n, 256), F32), jax.ShapeDtypeStruct((n, 128), F32)]
    return pl.pallas_call(
        functools.partial(_dsa_prep_kernel, prefill=prefill), grid=(n // tm,),
        in_specs=[pl.BlockSpec((tm, W_GROUP), lambda i: (i, 1)),
                  pl.BlockSpec((tm, 768), lambda i: (i % nt, 0)), pl.BlockSpec((tm, 384), lambda i: (i % nt, 0))],
        out_specs=out_specs, out_shape=out_shape, compiler_params=_params(), name="dsa_prep")(z, tab_k, tab_i)


def _sortable(x):
    b = pltpu.bitcast(x, I32)
    return b ^ ((b >> 31) & I32(0x7FFFFFFF))


def _topk_mask(score, idx, topk, n_idx_bits):
    key = _sortable(score)
    rows = score.shape[0]

    def count(pred):
        return jnp.sum(pred.astype(I32), axis=1, keepdims=True)

    def value_step(it, ans):
        cand = ans + jnp.left_shift(I32(1), I32(31) - it)
        return jnp.where(count(key >= cand) >= topk, cand, ans)

    thr = lax.fori_loop(0, 32, value_step, jnp.full((rows, 1), INT_MIN, I32))
    above = key > thr
    tie = key == thr
    need = topk - count(above)

    def index_step(it, lo):
        cand = lo + jnp.left_shift(I32(1), I32(n_idx_bits - 1) - it)
        return jnp.where(count(tie & (idx < cand)) < need, cand, lo)

    last = lax.fori_loop(0, n_idx_bits, index_step, jnp.zeros((rows, 1), I32))
    return above | (tie & (idx <= last))


def _dsa_attn_kernel(qt_ref, qit_ref, wt_ref, k_ref, vt_ref, ki4_ref, o_ref, hi_s, lo_s, tie_s, keep_s, m_s, l_s, acc_s, *,
                     topk, n_idx_bits):
    tq = qt_ref.shape[1]
    ck = hi_s.shape[1]
    i = pl.program_id(1)
    nc = _causal_chunks(i, tq, ck)
    qpos = i * tq + _iota((1, tq), 1)
    qpos16 = qpos.astype(I16)
    qith = _head_rows(qit_ref[...], D_IDX)
    w_row = [wt_ref[h:h + 1, :] * (D_IDX ** -0.5) for h in range(N_HEAD)]
    i16_min, i16_max = I16(-32768), I16(32767)

    def key_pos(c):
        return c * ck + _iota((ck, tq), 0)

    def score_chunk(c, carry):
        ki4 = ki4_ref[pl.ds(pl.multiple_of(c * ck, ck), ck), :]
        sc = jnp.zeros((ck, tq), F32)
        for h in range(N_HEAD):
            sc = sc + jnp.maximum(_dot(ki4, qith[h]), 0.0) * w_row[h]
        sc = jnp.where(key_pos(c) <= qpos, sc, NEG)
        key = _sortable(jnp.where(sc == 0.0, 0.0, sc))
        hi_s[c] = (key >> 16).astype(I16)
        lo_s[c] = ((key & I32(0xFFFF)) - I32(32768)).astype(I16)
        return carry

    lax.fori_loop(0, nc, score_chunk, 0)

    def count(pred):
        def body(c, acc):
            ones = jnp.where(pred(c), I16(1), I16(0))
            for j in range(ck // 16):
                acc = acc + ones[j * 16:(j + 1) * 16]
            return acc
        acc = lax.fori_loop(0, nc, body, jnp.zeros((16, tq), I16))
        return jnp.sum(acc.astype(I32), axis=0, keepdims=True)

    def search16(ref, want):
        def step(it, ans):
            cand = ans + jnp.left_shift(I32(1), I32(15) - it)
            c16 = cand.astype(I16)
            return jnp.where(count(lambda c: ref[c] >= c16) >= want, cand, ans)
        return lax.fori_loop(0, 16, step, jnp.full((1, tq), -32768, I32)).astype(I16)

    h16 = search16(hi_s, topk)
    n_above = count(lambda c: hi_s[c] > h16)

    def mask_lo(c, carry):
        lo_s[c] = jnp.where(hi_s[c] == h16, lo_s[c], i16_min)
        return carry

    lax.fori_loop(0, nc, mask_lo, 0)
    l16 = search16(lo_s, topk - n_above)
    need = topk - n_above - count(lambda c: lo_s[c] > l16)

    def tie_chunk(c, carry):
        kpos16 = key_pos(c).astype(I16)
        tie = (hi_s[c] == h16) & (lo_s[c] == l16) & (kpos16 <= qpos16)
        tie_s[c] = jnp.where(tie, kpos16, i16_max)
        return carry

    lax.fori_loop(0, nc, tie_chunk, 0)

    def index_step(it, lo):
        cand = lo + jnp.left_shift(I32(1), I32(n_idx_bits - 1) - it)
        c16 = cand.astype(I16)
        return jnp.where(count(lambda c: tie_s[c] < c16) < need, cand, lo)

    last16 = lax.fori_loop(0, n_idx_bits, index_step, jnp.zeros((1, tq), I32)).astype(I16)

    def keep_chunk(c, carry):
        kpos16 = key_pos(c).astype(I16)
        sel = jnp.where(hi_s[c] > h16, I16(1), I16(0)) + jnp.where(lo_s[c] > l16, I16(1), I16(0)) \
            + jnp.where(tie_s[c] <= last16, I16(1), I16(0))
        sel = jnp.where(kpos16 <= qpos16, sel, I16(0))
        keep_s[c] = jnp.where(sel.astype(I32) > 0, 0.0, NEG)
        return carry

    lax.fori_loop(0, nc, keep_chunk, 0)

    qth = _head_rows(qt_ref[...], HEAD_DIM)
    _flash_init(m_s, l_s, acc_s)

    def attend_chunk(c, carry):
        k = k_ref[pl.ds(pl.multiple_of(c * ck, ck), ck), :]
        keep = keep_s[c]
        sts = [_dot(k, qth[h]) + keep for h in range(N_HEAD)]
        vts = [vt_ref[c, h * HEAD_DIM:(h + 1) * HEAD_DIM, :] for h in range(N_HEAD)]
        _flash_update(sts, vts, m_s, l_s, acc_s)
        return carry

    lax.fori_loop(0, nc, attend_chunk, 0)
    o_ref[...] = _flash_result(l_s, acc_s)


def _dsa_attn(qt, k, vt, qit, wt, ki4, b, t, topk):
    tq = min(ROW_TILE, t)
    ck = vt.shape[2]
    nq, nc = t // tq, t // ck
    assert topk <= ck and t < 2 ** 15
    qcol = lambda w: pl.BlockSpec((w, tq), lambda bi, i: (0, bi * nq + i))
    seq = lambda w: pl.BlockSpec((t, w), lambda bi, i: (bi, 0))
    kernel = functools.partial(_dsa_attn_kernel, topk=topk, n_idx_bits=max(1, (t - 1).bit_length()))
    scratch = [pltpu.VMEM((nc, ck, tq), I16)] * 3 + [pltpu.VMEM((nc, ck, tq), F32)] + _flash_scratch(tq)
    return pl.pallas_call(
        kernel, grid=(b, nq),
        in_specs=[qcol(256), qcol(128), qcol(8), seq(256), pl.BlockSpec((nc, W_BRANCH, ck), lambda bi, i: (bi, 0, 0)),
                  seq(128)],
        out_specs=pl.BlockSpec((tq, W_BRANCH), lambda bi, i: (bi * nq + i, 0)),
        out_shape=jax.ShapeDtypeStruct((b * t, W_BRANCH), F32), scratch_shapes=scratch,
        compiler_params=_params(2), name="dsa_attn")(qt, qit, wt, k, vt, ki4)


def _dsa_select_kernel(pt_ref, qi_ref, misc_ref, *refs, n_pages, n_seq, tt, topk, t_valid):
    ki_refs = refs[:n_seq * n_pages]
    keep_ref = refs[n_seq * n_pages]
    page = ki_refs[0].shape[1]
    past = n_pages * page
    scores = []
    for j in range(n_seq):
        rows = slice(j * tt, (j + 1) * tt)
        qi = qi_ref[rows, :]
        qi_h = jnp.concatenate([qi[:, h * D_IDX:(h + 1) * D_IDX] for h in range(N_HEAD)], axis=0).astype(BF)
        w_col = jnp.concatenate([misc_ref[rows, D_IDX + h:D_IDX + h + 1] for h in range(N_HEAD)], axis=0)
        kit_past = _cat_pages(ki_refs[j * n_pages:(j + 1) * n_pages], 0, D_IDX).astype(BF)
        ki_new = _pad_rows(misc_ref[rows, 0:D_IDX], 128).astype(BF)
        s = jnp.concatenate([_dot(qi_h, kit_past), _dot_nt(qi_h, ki_new)], axis=1)
        scores.append(_unstack_heads(jnp.maximum(s * (D_IDX ** -0.5), 0.0) * w_col))
    score = jnp.concatenate(scores, axis=0)
    n_keys = past + 128
    idx = _iota((1, n_keys), 1)
    new_t = idx - past
    t_q = _iota((n_seq * tt, 1), 0) % tt
    visible = (new_t <= t_q) & (new_t < t_valid)
    score = jnp.where(visible, jnp.where(score == 0.0, 0.0, score), NEG)
    sel = _topk_mask(score, idx, topk, n_keys.bit_length()) & visible
    keep_ref[...] = jnp.where(sel, 0.0, NEG)


def _dsa_select(qi, misc, cache_ki, page_table, layer, b, tt, topk, t_valid):
    n_pages = page_table.shape[1]
    page = cache_ki.shape[3]
    n_seq = min(SELECT_SEQS, b)
    n_keys = n_pages * page + 128
    kernel = functools.partial(_dsa_select_kernel, n_pages=n_pages, n_seq=n_seq, tt=tt, topk=topk, t_valid=t_valid)

    def page_spec(j, p):
        return pl.BlockSpec((None, None, D_IDX, page), lambda g, pt: (layer, pt[g * n_seq + j, p], 0, 0))

    rows = lambda w: pl.BlockSpec((n_seq * tt, w), lambda g, pt: (g, 0))
    grid_spec = pltpu.PrefetchScalarGridSpec(
        num_scalar_prefetch=1, grid=(b // n_seq,),
        in_specs=[rows(128), rows(128)] + [page_spec(j, p) for j in range(n_seq) for p in range(n_pages)],
        out_specs=rows(n_keys))
    return pl.pallas_call(
        kernel, grid_spec=grid_spec, out_shape=jax.ShapeDtypeStruct((b * tt, n_keys), F32),
        compiler_params=_params(), name="dsa_select")(page_table, qi, misc, *([cache_ki] * (n_seq * n_pages)))


def _dsa_decode_kernel(pt_ref, q_ref, keep_ref, kvn_ref, *refs, n_pages, n_seq):
    o_ref = refs[n_seq * n_pages]
    tt = q_ref.shape[0] // n_seq
    past = n_pages * refs[0].shape[1]

    def scores(j):
        rows = slice(j * tt, (j + 1) * tt)
        kv_refs = refs[j * n_pages:(j + 1) * n_pages]
        qbd = _stack_heads(q_ref[rows, :]).astype(BF)
        k_new = _pad_rows(kvn_ref[rows, 0:256], 128).astype(BF)
        v_new = _pad_rows(kvn_ref[rows, 256:512], 128).astype(BF)
        s = jnp.concatenate([_dot(qbd, _cat_pages(kv_refs, 0, W_BRANCH).astype(BF)), _dot_nt(qbd, k_new)], axis=1)
        keep4 = jnp.concatenate([keep_ref[rows, :]] * N_HEAD, axis=0)
        return jnp.where(keep4 == 0.0, s, NEG), _cat_pages(kv_refs, W_BRANCH, 2 * W_BRANCH).astype(BF), v_new

    parts = [scores(j) for j in range(n_seq)]
    probs = [_softmax_rows(s) for s, _, _ in parts]
    for j, ((pb, l), (_, vt_past, v_new)) in enumerate(zip(probs, parts)):
        o_ref[j * tt:(j + 1) * tt, :] = _weighted_values(pb, l, vt_past, v_new, past)


def _dsa_decode(q, keep, kv_new, cache_kv, page_table, layer, b, tt):
    n_pages = page_table.shape[1]
    page = cache_kv.shape[3]
    n_seq = min(DECODE_SEQS, b)
    kernel = functools.partial(_dsa_decode_kernel, n_pages=n_pages, n_seq=n_seq)
    row = lambda w: pl.BlockSpec((n_seq * tt, w), lambda i, pt: (i, 0))
    grid_spec = pltpu.PrefetchScalarGridSpec(
        num_scalar_prefetch=1, grid=(b // n_seq,),
        in_specs=[row(256), row(keep.shape[1]), row(512)] + _page_specs(n_pages, (2 * W_BRANCH, page), layer, n_seq),
        out_specs=row(256))
    return pl.pallas_call(
        kernel, grid_spec=grid_spec, out_shape=jax.ShapeDtypeStruct((b * tt, W_BRANCH), F32),
        compiler_params=_params(), name="dsa_decode")(page_table, q, keep, kv_new, *([cache_kv] * (n_seq * n_pages)))


def _rwkv_chunks(chunks, n_double):
    c = chunks[0][0].shape[0]
    cc = N_HEAD * c
    tri = _tril(c).astype(BF)
    strict = _tril(cc, strict=True)
    incl = _tril(cc)
    each = lambda fn, *lists: [fn(*xs) for xs in zip(*lists)]
    r, k, v, lw, kap, beta = (list(x) for x in zip(*chunks))
    g = each(lambda x: _dot_sel_l(tri, x), lw)
    g_end = [x[c - 1:c] for x in g]
    e_neg = each(lambda x: jnp.exp(-x), g)
    e_end = each(lambda x, y: jnp.exp(y - x), g, g_end)
    a_f = each(lambda kp, x, l: _stack_heads(kp * jnp.exp(x - l)), kap, g, lw)
    a_s = each(lambda x: x.astype(BF), a_f)
    r_s = each(lambda x, y: _stack_heads(x * jnp.exp(y)), r, g)
    r_sb = each(lambda x: x.astype(BF), r_s)
    bb_s = each(lambda x, e: _stack_heads(x * e).astype(BF), beta, e_neg)
    bk_s = each(lambda x, e: _stack_heads(x * e).astype(BF), k, e_neg)
    v_s = each(lambda x: _stack_heads(x).astype(BF), v)
    kh_s = each(lambda x, e: _stack_heads(x * e).astype(BF), k, e_end)
    bh_s = each(lambda x, e: _stack_heads(x * e).astype(BF), beta, e_end)
    l_b = each(lambda x, y: jnp.where(strict, _dot_nt(x, y), 0.0), a_s, bb_s)
    l_k = each(lambda x, y: jnp.where(strict, _dot_nt(x, y), 0.0).astype(BF), a_s, bk_s)
    w_b = each(lambda x, y: jnp.where(incl, _dot_nt(x, y), 0.0).astype(BF), r_sb, bb_s)
    w_k = each(lambda x, y: jnp.where(incl, _dot_nt(x, y), 0.0).astype(BF), r_sb, bk_s)
    y = each(lambda x: -x, l_b)
    n = y
    for _ in range(n_double):
        y = each(lambda x: _dot(x.astype(BF), x.astype(BF)), y)
        n = each(lambda p, q: p + q + _dot(p.astype(BF), q.astype(BF)), n, y)
    nb = each(lambda x: x.astype(BF), n)
    a_t = each(lambda f, p, q: (f + _dot(p, q)).astype(BF), a_f, nb, a_s)
    lkv = each(_dot, l_k, v_s)
    u0 = each(lambda x, p: (x + _dot(p, x.astype(BF))).astype(BF), lkv, nb)
    r_hat = each(lambda x, w, a: (x - _dot(w, a)).astype(BF), r_s, w_b, a_t)
    y0 = each(lambda wk, vs, wb, u: _dot(wk, vs) - _dot(wb, u), w_k, v_s, w_b, u0)
    h_mat = each(lambda a, b: _dot_tn(a, b).astype(BF), a_t, bh_s)
    s_add = each(lambda vs, kh, u, bh: _dot_tn(vs, kh) - _dot_tn(u, bh), v_s, kh_s, u0, bh_s)
    decay = each(jnp.exp, g_end)
    return list(zip(r_hat, y0, decay, h_mat, s_add))


def _rwkv_apply(par, s_big):
    r_hat, y0, decay, h_mat, s_add = par
    sb = s_big.astype(BF)
    ys = _dot_nt(r_hat, sb) + y0
    s_new = s_big * decay - _dot(sb, h_mat) + s_add
    return _unstack_heads(ys), s_new


def _to_block_diag(x):
    return jnp.where(_block_diag_mask(W_BRANCH, HEAD_DIM), jnp.concatenate([x] * N_HEAD, axis=1), 0.0)


def _from_block_diag(x):
    y = x + pltpu.roll(x, 64, 1) + pltpu.roll(x, 128, 1) + pltpu.roll(x, 192, 1)
    return y[:, 0:HEAD_DIM]


def _rwkv_kernel(*refs, seq_len, t_valid, chunk, n_double, lockstep, has_state):
    if has_state:
        (z_ref, prev_ref, s_in_ref, mu_ref, vec_ref, wa_ref, g2_ref, o_ref, s_out_ref,
         r_s, k_s, v_s, lw_s, kap_s, beta_s, y_s) = refs
    else:
        (z_ref, mu_ref, vec_ref, wa_ref, g2_ref, o_ref, s_out_ref,
         r_s, k_s, v_s, lw_s, kap_s, beta_s, y_s, state_s, last_s) = refs
    rows = z_ref.shape[0]
    i = pl.program_id(0)
    t_row = (i * rows + _iota((rows, 1), 0)) % seq_len
    pr = z_ref[...]
    shifted = pltpu.roll(pr, 1, 0)
    if has_state:
        prev = jnp.where(t_row == 0, prev_ref[...], shifted)
    else:
        first = (i * rows) % seq_len == 0

        @pl.when(first)
        def _():
            last_s[...] = jnp.zeros_like(last_s)
            state_s[...] = jnp.zeros_like(state_s)

        prev = jnp.where(_iota((rows, 1), 0) == 0, last_s[7:8, :], shifted)
        last_s[...] = pr[rows - 8:rows]
    xs = pr + (prev - pr) * mu_ref[...]
    r, k, v = xs[:, 0:256], xs[:, 256:512], xs[:, 512:768]
    lora = xs[:, 768:896]
    lora = jnp.where(_iota((1, 128), 1) < LORA_W, jnp.tanh(lora), lora)
    wa = _dot(lora.astype(BF), wa_ref[...])
    w0, a0, kk_p, ka_p = vec_ref[0:1], vec_ref[1:2], vec_ref[2:3], vec_ref[3:4]
    rk_p, ln_w, ln_b = vec_ref[4:5], vec_ref[5:6], vec_ref[6:7]
    w_log = -_softplus(-(w0 + wa[:, 0:256])) - 0.5
    lw = -jnp.exp(w_log)
    a = _sigmoid(a0 + wa[:, 256:512])
    gate = _dot(_sigmoid(xs[:, 896:1024]).astype(BF), g2_ref[...])
    ones_bd = _block_diag_mask(W_BRANCH, HEAD_DIM).astype(BF)
    kk = k * kk_p
    kap = kk / jnp.maximum(jnp.sqrt(_head_sum(kk * kk, ones_bd)), 1e-12)
    k2 = k * (1.0 + (a - 1.0) * ka_p)
    bonus = _head_sum(r * k2 * rk_p, ones_bd) * v
    live = t_row < t_valid
    r_s[...] = r
    k_s[...] = jnp.where(live, k2, 0.0)
    v_s[...] = jnp.where(live, v, 0.0)
    lw_s[...] = jnp.where(live, lw, 0.0)
    kap_s[...] = jnp.where(live, kap, 0.0)
    beta_s[...] = jnp.where(live, a * kap, 0.0)

    def group(gi, carry):
        def rows_of(j, size):
            return pl.ds(pl.multiple_of((gi * lockstep + j) * size, size), size)

        sls = [rows_of(j, chunk) for j in range(lockstep)]
        pars = _rwkv_chunks([(r_s[sl, :], k_s[sl, :], v_s[sl, :], lw_s[sl, :], kap_s[sl, :], beta_s[sl, :]) for sl in sls],
                            n_double)
        if has_state:
            for j, (sl, par) in enumerate(zip(sls, pars)):
                st = rows_of(j, W_BRANCH)
                y, s_new = _rwkv_apply(par, _to_block_diag(s_in_ref[st, :]))
                y_s[sl, :] = y
                s_out_ref[st, :] = _from_block_diag(s_new)
        else:
            s_big = state_s[...]
            for sl, par in zip(sls, pars):
                y, s_big = _rwkv_apply(par, s_big)
                y_s[sl, :] = y
            state_s[...] = s_big
        return carry

    lax.fori_loop(0, rows // (chunk * lockstep), group, 0)
    if not has_state:
        s_out_ref[...] = _from_block_diag(state_s[...])
    y = y_s[...]
    mu = _head_sum(y, ones_bd) * (1.0 / HEAD_DIM)
    yc = y - mu
    var = _head_sum(yc * yc, ones_bd) * (1.0 / HEAD_DIM)
    yn = yc * lax.rsqrt(var + RWKV_GN_EPS) * ln_w + ln_b
    o_ref[...] = (yn + bonus) * gate


def _rwkv(z, prev_rows, state_in, mu, vec, wa, g2, b, seq_len, t_valid):
    n = z.shape[0]
    has_state = state_in is not None
    chunk = min(RWKV_CHUNK, seq_len)
    rows = min(ROW_TILE if has_state else RWKV_LOCKSTEP * chunk, n)
    n_double = max(0, int(math.log2(chunk)) - 1)
    steps = n // rows
    seq_per_tile = max(1, rows // seq_len)
    tiles_per_seq = max(1, seq_len // rows)
    lockstep = min(RWKV_LOCKSTEP, rows // chunk)
    kernel = functools.partial(_rwkv_kernel, seq_len=seq_len, t_valid=t_valid, chunk=chunk, n_double=n_double,
                               lockstep=lockstep, has_state=has_state)
    consts = [_const_spec((1, W_GROUP)), _const_spec((8, W_BRANCH)), _const_spec((128, 512)), _const_spec((128, W_BRANCH))]
    zspec = pl.BlockSpec((rows, W_GROUP), lambda i: (i, 2))
    scratch = [pltpu.VMEM((rows, W_BRANCH), F32)] * 7
    if has_state:
        srows = seq_per_tile * W_BRANCH
        in_specs = [zspec, pl.BlockSpec((rows, W_GROUP), lambda i: (i, 0)), pl.BlockSpec((srows, HEAD_DIM), lambda i: (i, 0))] + consts
        s_spec = pl.BlockSpec((srows, HEAD_DIM), lambda i: (i, 0))
        args = (z, prev_rows, state_in, mu, vec, wa, g2)
    else:
        in_specs = [zspec] + consts
        s_spec = pl.BlockSpec((W_BRANCH, HEAD_DIM), lambda i: (i // tiles_per_seq, 0))
        scratch = scratch + [pltpu.VMEM((W_BRANCH, W_BRANCH), F32), pltpu.VMEM((8, W_GROUP), F32)]
        args = (z, mu, vec, wa, g2)
    return pl.pallas_call(
        kernel, grid=(steps,), in_specs=in_specs,
        out_specs=[pl.BlockSpec((rows, W_BRANCH), lambda i: (i, 0)), s_spec],
        out_shape=[jax.ShapeDtypeStruct((n, W_BRANCH), F32), jax.ShapeDtypeStruct((b * W_BRANCH, HEAD_DIM), F32)],
        scratch_shapes=scratch, compiler_params=_params(), name="rwkv")(*args)


def _hgrn_lb_kernel(x_ref, o_ref):
    x = x_ref[...]
    depth = x.shape[0]
    e = jnp.exp(x - jnp.max(x, axis=0, keepdims=True))
    soft = e / jnp.sum(e, axis=0, keepdims=True)
    cum = jnp.zeros((1, x.shape[1]), F32)
    for l in range(depth):
        cum = cum + soft[l:l + 1]
        lb = jnp.maximum(cum - soft[0:1], 0.0)
        o_ref[l, 0:1, :] = lb
        o_ref[l, 1:2, :] = jnp.log(jnp.maximum(lb, LB_TINY))
        o_ref[l, 2:3, :] = jnp.log1p(-lb)
        o_ref[l, 3:8, :] = jnp.zeros((5, x.shape[1]), F32)


def _hgrn_lb(hgrn_lb):
    depth, w = hgrn_lb.shape
    return pl.pallas_call(_hgrn_lb_kernel, out_shape=jax.ShapeDtypeStruct((depth, 8, w), F32), name="hgrn_lb")(hgrn_lb)


def _hgrn_blocks(blocks, ones_bd, bd_mask):
    c = blocks[0][0].shape[0]
    tri = _tril(c).astype(BF)
    t_idx = _iota((c, 1), 0)
    each = lambda fn, *lists: [fn(*xs) for xs in zip(*lists)]
    q, k, v, g = (list(x) for x in zip(*blocks))
    b = each(lambda x: _dot_sel_l(tri, x), g)
    b_end = [x[c - 1:c] for x in b]

    def pair_terms(qq, kk, bb):
        xs = []
        for s in range(c):
            e = jnp.exp(jnp.where(t_idx >= s, bb - bb[s:s + 1], NEG))
            xs.append((qq * e * kk[s:s + 1]).astype(BF))
        return jnp.concatenate(xs, axis=0)

    col = each(lambda qq, kk, bb: _dot(pair_terms(qq, kk, bb), ones_bd), q, k, b)

    def inside(cl, vv):
        o = cl[0:c] * vv[0:1]
        for s in range(1, c):
            o = o + cl[s * c:(s + 1) * c] * vv[s:s + 1]
        return o

    o_in = each(inside, col, v)
    qd = each(lambda qq, bb: (qq * jnp.exp(bb)).astype(BF), q, b)
    upd = each(lambda vv, kk, bb, be: jnp.where(bd_mask, _dot_tn(vv.astype(BF), (kk * jnp.exp(be - bb)).astype(BF)), 0.0),
               v, k, b, b_end)
    decay = each(jnp.exp, b_end)
    return list(zip(qd, o_in, decay, upd))


def _hgrn_apply(par, st):
    qd, o_in, decay, upd = par
    return _dot_nt(qd, st.astype(BF)) + o_in, st * decay + upd


def _hgrn_kernel(*refs, seq_len, t_valid, block, has_state):
    if has_state:
        z_ref, s_in_ref, lb_ref, nw_ref, o_ref, s_out_ref, q_s, k_s, v_s, g_s, y_s = refs
    else:
        z_ref, lb_ref, nw_ref, o_ref, s_out_ref, q_s, k_s, v_s, g_s, y_s, state_s = refs
    rows = z_ref.shape[0]
    i = pl.program_id(0)
    t_row = (i * rows + _iota((rows, 1), 0)) % seq_len
    live = t_row < t_valid
    hq, hf, hi, hg = z_ref[:, 0:256], z_ref[:, 256:512], z_ref[:, 512:768], z_ref[:, 768:1024]
    lb, lb_log, l1m = lb_ref[0:1], lb_ref[1:2], lb_ref[2:3]
    ls = _log_sigmoid(hf)
    x2 = l1m + ls
    lae = jnp.maximum(lb_log, x2) + jnp.log1p(jnp.exp(-jnp.abs(lb_log - x2)))
    logf = jnp.where(lb > 0.0, lae, ls)
    q_s[...] = hq * _sigmoid(hq)
    k_s[...] = jnp.where(live, (1.0 - lb) * _sigmoid(-hf), 0.0)
    v_s[...] = hi
    g_s[...] = jnp.where(live, logf, 0.0)
    ones_bd = _block_diag_mask(W_BRANCH, HEAD_DIM).astype(BF)
    bd_mask = _block_diag_mask(W_BRANCH, HEAD_DIM)

    if not has_state:
        @pl.when((i * rows) % seq_len == 0)
        def _():
            state_s[...] = jnp.zeros_like(state_s)

    lockstep = min(HGRN_LOCKSTEP, rows // block)

    def group(gi, carry):
        def rows_of(j, size):
            return pl.ds(pl.multiple_of((gi * lockstep + j) * size, size), size)

        sls = [rows_of(j, block) for j in range(lockstep)]
        pars = _hgrn_blocks([(q_s[sl, :], k_s[sl, :], v_s[sl, :], g_s[sl, :]) for sl in sls], ones_bd, bd_mask)
        if has_state:
            for j, (sl, par) in enumerate(zip(sls, pars)):
                sr = rows_of(j, W_BRANCH)
                o, st_new = _hgrn_apply(par, _to_block_diag(s_in_ref[sr, :]).T)
                y_s[sl, :] = o
                s_out_ref[sr, :] = _from_block_diag(st_new.T)
        else:
            st = state_s[...]
            for sl, par in zip(sls, pars):
                o, st = _hgrn_apply(par, st)
                y_s[sl, :] = o
            state_s[...] = st
        return carry

    lax.fori_loop(0, rows // (block * lockstep), group, 0)
    if not has_state:
        s_out_ref[...] = _from_block_diag(state_s[...].T)
    o = y_s[...]
    ms = _head_sum(o * o, ones_bd) * (1.0 / HEAD_DIM)
    o_ref[...] = o * lax.rsqrt(ms + RMS_EPS) * nw_ref[...] * (hg * _sigmoid(hg))


def _hgrn(z, state_in, lb_rows, norm_w, b, seq_len, t_valid):
    n = z.shape[0]
    has_state = state_in is not None
    rows = min(ROW_TILE, n)
    block = min(HGRN_BLOCK, seq_len)
    seq_per_tile = max(1, rows // seq_len)
    tiles_per_seq = max(1, seq_len // rows)
    kernel = functools.partial(_hgrn_kernel, seq_len=seq_len, t_valid=t_valid, block=block, has_state=has_state)
    zspec = pl.BlockSpec((rows, W_GROUP), lambda i: (i, 3))
    consts = [_const_spec((8, W_BRANCH)), _const_spec((1, W_BRANCH))]
    scratch = [pltpu.VMEM((rows, W_BRANCH), F32)] * 5
    if has_state:
        srows = seq_per_tile * W_BRANCH
        in_specs = [zspec, pl.BlockSpec((srows, HEAD_DIM), lambda i: (i, 0))] + consts
        s_spec = pl.BlockSpec((srows, HEAD_DIM), lambda i: (i, 0))
        args = (z, state_in, lb_rows, norm_w)
    else:
        in_specs = [zspec] + consts
        s_spec = pl.BlockSpec((W_BRANCH, HEAD_DIM), lambda i: (i // tiles_per_seq, 0))
        scratch = scratch + [pltpu.VMEM((W_BRANCH, W_BRANCH), F32)]
        args = (z, lb_rows, norm_w)
    return pl.pallas_call(
        kernel, grid=(n // rows,), in_specs=in_specs,
        out_specs=[pl.BlockSpec((rows, W_BRANCH), lambda i: (i, 0)), s_spec],
        out_shape=[jax.ShapeDtypeStruct((n, W_BRANCH), F32), jax.ShapeDtypeStruct((b * W_BRANCH, HEAD_DIM), F32)],
        scratch_shapes=scratch, compiler_params=_params(), name="hgrn")(*args)


def _merge_kernel(x_ref, oa_ref, ob_ref, oc_ref, od_ref, wg_ref, wb_ref, wo_ref, ln_ref, h_ref, *, alpha):
    x = x_ref[...]
    xb = x.astype(BF)
    m = jnp.zeros(x.shape, F32)
    for n, o_ref in enumerate((oa_ref, ob_ref, oc_ref, od_ref)):
        gate = _sigmoid(_dot(xb, wg_ref[:, n * D_MODEL:(n + 1) * D_MODEL]))
        m = m + gate * _dot(o_ref[...].astype(BF), wb_ref[n])
    mix = _dot(m.astype(BF), wo_ref[...])
    h_ref[...] = _layer_norm(alpha * x + mix, ln_ref[0:1], ln_ref[1:2])


def _merge(x, oa, ob, oc, od, wg, wb, wo, ln, alpha):
    n = x.shape[0]
    tm = min(ROW_TILE, n)
    row = lambda w: pl.BlockSpec((tm, w), lambda i: (i, 0))
    return pl.pallas_call(
        functools.partial(_merge_kernel, alpha=alpha), grid=(n // tm,),
        in_specs=[row(D_MODEL), row(256), row(256), row(256), row(256), _const_spec(wg.shape), _const_spec(wb.shape),
                  _const_spec(wo.shape), _const_spec((8, D_MODEL))],
        out_specs=row(D_MODEL), out_shape=jax.ShapeDtypeStruct((n, D_MODEL), F32),
        compiler_params=_params(), name="merge")(x, oa, ob, oc, od, wg, wb, wo, ln)


def _ffn_kernel(*refs, seq_len, has_state, n_split, alpha):
    if has_state:
        h_ref, p1_ref, p2_ref, wup_ref, wdn_ref, cv_ref, ln_ref, y_ref, a_ref = refs
    else:
        h_ref, wup_ref, wdn_ref, cv_ref, ln_ref, y_ref, a_ref, last_s = refs
    rows = h_ref.shape[0]
    i = pl.program_id(0)
    h = h_ref[...]
    hb = h.astype(BF)
    ridx = _iota((rows, 1), 0)
    t_row = (i * rows + ridx) % seq_len
    wf = D_FF // n_split
    f = jnp.zeros((rows, D_MODEL), F32)
    if not has_state:
        @pl.when((i * rows) % seq_len == 0)
        def _():
            last_s[...] = jnp.zeros_like(last_s)
    for j in range(n_split):
        lo, hi = j * wf, (j + 1) * wf
        a = _dot(hb, wup_ref[:, lo:hi])
        gt = _dot(hb, wup_ref[:, D_FF + lo:D_FF + hi])
        r1 = pltpu.roll(a, 1, 0)
        r2 = pltpu.roll(a, 2, 0)
        if has_state:
            prev1 = jnp.where(t_row == 0, p1_ref[:, lo:hi], r1)
            prev2 = jnp.where(t_row < 2, p2_ref[:, lo:hi], r2)
            a_ref[:, lo:hi] = a
        else:
            c6 = last_s[6:7, lo:hi]
            c7 = last_s[7:8, lo:hi]
            prev1 = jnp.where(ridx == 0, c7, r1)
            prev2 = jnp.where(ridx == 0, c6, jnp.where(ridx == 1, c7, r2))
            last_s[:, lo:hi] = a[rows - 8:rows]
            a_ref[:, lo:hi] = a[rows - 8:rows]
        conv = cv_ref[3:4, lo:hi] + prev2 * cv_ref[0:1, lo:hi] + prev1 * cv_ref[1:2, lo:hi] + a * cv_ref[2:3, lo:hi]
        hid = _gelu(conv) * gt
        f = f + _dot(hid.astype(BF), wdn_ref[lo:hi, :])
    y_ref[...] = _layer_norm(alpha * h + f, ln_ref[0:1], ln_ref[1:2])


def _ffn(h, p1, p2, wup, wdn, cv, ln, seq_len, alpha):
    n = h.shape[0]
    has_state = p1 is not None
    tm = min(ROW_TILE, n)
    row = lambda w: pl.BlockSpec((tm, w), lambda i: (i, 0))
    kernel = functools.partial(_ffn_kernel, seq_len=seq_len, has_state=has_state, n_split=2, alpha=alpha)
    consts = [_const_spec(wup.shape), _const_spec(wdn.shape), _const_spec((8, D_FF)), _const_spec((8, D_MODEL))]
    if has_state:
        in_specs = [row(D_MODEL), row(D_FF), row(D_FF)] + consts
        a_spec, a_rows, scratch = row(D_FF), n, []
        args = (h, p1, p2, wup, wdn, cv, ln)
    else:
        in_specs = [row(D_MODEL)] + consts
        a_spec, a_rows = pl.BlockSpec((8, D_FF), lambda i: (i, 0)), (n // tm) * 8
        scratch = [pltpu.VMEM((8, D_FF), F32)]
        args = (h, wup, wdn, cv, ln)
    return pl.pallas_call(
        kernel, grid=(n // tm,), in_specs=in_specs, out_specs=[row(D_MODEL), a_spec],
        out_shape=[jax.ShapeDtypeStruct((n, D_MODEL), F32), jax.ShapeDtypeStruct((a_rows, D_FF), F32)],
        scratch_shapes=scratch, compiler_params=_params(), name="ffn")(*args)


def _pad_lanes(x, width):
    return jnp.pad(x, [(0, 0)] * (x.ndim - 1) + [(0, width - x.shape[-1])])


def _regroup_w_in(w_in):
    o = 0
    cols = {}
    for name, w in (('fox_q', 256), ('fox_k', 256), ('fox_v', 256), ('fox_f', 4), ('dsa_q', 256), ('dsa_k', 256),
                    ('dsa_v', 256), ('idx_q', 128), ('idx_k', 32), ('idx_w', 4), ('rwkv', 1024), ('hgrn', 1024),
                    ('gate', 4096)):
        cols[name] = w_in[..., o:o + w]
        o += w
    g0 = _pad_lanes(jnp.concatenate([cols['fox_k'], cols['fox_v'], cols['fox_q'], cols['fox_f']], -1), W_GROUP)
    g1 = _pad_lanes(jnp.concatenate([cols['dsa_k'], cols['dsa_v'], cols['dsa_q'], cols['idx_q'], cols['idx_k'],
                                     cols['idx_w']], -1), W_GROUP)
    wz = jnp.concatenate([g0, g1, cols['rwkv'], cols['hgrn']], -1).astype(BF)
    return wz, cols['gate'].astype(BF)


def _rows8(*vecs, width):
    rows = [v.reshape(1, width) for v in vecs]
    rows.append(jnp.zeros((8 - len(rows), width), F32))
    return jnp.concatenate(rows, axis=0)


def _expand_first_rows(state, tt, offsets):
    b, _, w = state.shape
    rows = dict((ts, s) for s, ts in offsets)
    zero = jnp.zeros((b, 1, w), state.dtype)
    out = jnp.concatenate([state[:, rows[t]:rows[t] + 1] if t in rows else zero for t in range(tt)], axis=1)
    return out.reshape(b * tt, w)


def _layer(x, cfg, lw):
    b, tt, tv, past = cfg['b'], cfg['tt'], cfg['tv'], cfg['past']
    decode = past > 0
    z = _inproj(x, lw['wz'])
    topk = max(1, min(DSA_TOPK, (past + tv) // 4))
    new = {}
    if decode:
        o_a, logf = _fox_decode(z, lw['fox_bf'], cfg['fox_kv'], cfg['fox_lf'], cfg['page_table'], cfg['layer'], b, tt)
        new['fox_kv'] = z[:, 0:512]
    else:
        qt_a, k_a, vt_a, new['fox_kv'], logf, ccol, crow = _fox_prep(z, lw['fox_bf'], b, tt)
        o_a = _fox_attn(qt_a, k_a, vt_a, ccol, crow, b, tt)
    new['fox_logf'] = logf[:, 0:N_HEAD]
    if decode:
        kv_b, misc_b, q_b, qi_b = _dsa_prep(z, cfg['tab_k'], cfg['tab_i'], False)
        keep = _dsa_select(qi_b, misc_b, cfg['dsa_ki'], cfg['page_table'], cfg['layer'], b, tt, topk, tv)
        o_b = _dsa_decode(q_b, keep, kv_b, cfg['dsa_kv'], cfg['page_table'], cfg['layer'], b, tt)
    else:
        kv_b, misc_b, qt_b, k_b, vt_b, qit_b, wt_b, ki4_b = _dsa_prep(z, cfg['tab_k'], cfg['tab_i'], True)
        o_b = _dsa_attn(qt_b, k_b, vt_b, qit_b, wt_b, ki4_b, b, tt, topk)
    new['dsa_kv'] = kv_b
    new['dsa_kidx'] = misc_b[:, 0:D_IDX]
    o_c, new['rwkv'] = _rwkv(z, cfg.get('shift_rows'), cfg.get('rwkv_state'), lw['rwkv_mu'], lw['rwkv_vec'], lw['rwkv_wa'],
                             lw['rwkv_g2'], b, tt, tv)
    new['shift'] = z.reshape(b, tt, N_GROUP * W_GROUP)[:, tv - 1, 2 * W_GROUP:3 * W_GROUP]
    o_d, new['hgrn'] = _hgrn(z, cfg.get('hgrn_state'), lw['hgrn_lb'], lw['hgrn_nw'], b, tt, tv)
    h = _merge(x, o_a, o_b, o_c, o_d, lw['wg'], lw['wb'], lw['wo'], lw['ln1'], cfg['alpha'])
    y, new['conv'] = _ffn(h, cfg.get('conv_p1'), cfg.get('conv_p2'), lw['wup'], lw['wdn'], lw['conv'], lw['ln2'], tt, cfg['alpha'])
    return y, new


def kernel(x_prompt, x_sample, cache_fox_kv, cache_fox_logf, cache_dsa_kv, cache_dsa_kidx, state_rwkv, state_rwkv_shift, state_hgrn, state_ffn_conv, page_table, w_in, fox_bf, rwkv_mu, rwkv_w0, rwkv_w2, rwkv_a0, rwkv_a2, rwkv_g2, rwkv_kk, rwkv_ka, rwkv_rk, rwkv_ln_w, rwkv_ln_b, hgrn_lb, hgrn_norm_w, w_branch, w_o, ln1_g, ln1_b, ln2_g, ln2_b, ffn_w_in, ffn_conv_w, ffn_conv_b, ffn_w_out):
    depth = w_in.shape[0]
    bp, tp, d = x_prompt.shape
    bs, ts, _ = x_sample.shape
    n_pool, page = cache_fox_kv.shape[1], cache_fox_kv.shape[2]
    n_pages = page_table.shape[1]
    past = n_pages * page
    tsp = -(-ts // T_ALIGN) * T_ALIGN
    assert d == D_MODEL and tp % min(ROW_TILE, tp) == 0 and (bs * tsp) % min(ROW_TILE, bs * tsp) == 0
    assert tsp <= 128 and ts >= 2

    wz_all, wg_all = _regroup_w_in(w_in)
    wb_all, wo_all = w_branch.astype(BF), w_o.astype(BF)
    wup_all, wdn_all = ffn_w_in.astype(BF), ffn_w_out.astype(BF)
    zero_l = jnp.zeros((depth, LORA_W, W_BRANCH), F32)
    wa_all = jnp.concatenate([jnp.concatenate([rwkv_w2, zero_l], 2), jnp.concatenate([zero_l, rwkv_a2], 2)], 1).astype(BF)
    g2_all = rwkv_g2.astype(BF)
    lb_all = _hgrn_lb(hgrn_lb)
    page_table = page_table.astype(I32)

    fox_kv_pages = jnp.transpose(cache_fox_kv, (0, 1, 3, 4, 5, 2)).reshape(depth, n_pool, 2 * W_BRANCH, page)
    dsa_kv_pages = jnp.transpose(cache_dsa_kv, (0, 1, 3, 4, 5, 2)).reshape(depth, n_pool, 2 * W_BRANCH, page)
    dsa_ki_pages = jnp.swapaxes(cache_dsa_kidx, 2, 3)
    fox_lf_pages = _pad_rows_nd(jnp.swapaxes(cache_fox_logf, 2, 3), 8)

    pos_p = jnp.arange(tp)
    pos_s = past + (jnp.arange(bs * tsp) % tsp)[:min(ROW_TILE, bs * tsp)]
    alpha = (2 * depth) ** 0.25
    cfg_p = dict(b=bp, tt=tp, tv=tp, past=0, alpha=alpha,
                 tab_k=_rope_tables(pos_p, 256, HEAD_DIM, ROPE_HALF_QK), tab_i=_rope_tables(pos_p, 128, D_IDX, ROPE_HALF_IDX))
    cfg_s = dict(b=bs, tt=tsp, tv=ts, past=past, alpha=alpha, page_table=page_table, fox_kv=fox_kv_pages, fox_lf=fox_lf_pages,
                 dsa_kv=dsa_kv_pages, dsa_ki=dsa_ki_pages,
                 tab_k=_rope_tables(pos_s, 256, HEAD_DIM, ROPE_HALF_QK), tab_i=_rope_tables(pos_s, 128, D_IDX, ROPE_HALF_IDX))

    xp = x_prompt.reshape(bp * tp, d)
    xs = jnp.pad(x_sample, ((0, 0), (0, tsp - ts), (0, 0))).reshape(bs * tsp, d)
    new_p, new_s = [], []
    for l in range(depth):
        lw = dict(wz=wz_all[l], wg=wg_all[l], wb=wb_all[l], wo=wo_all[l], wup=wup_all[l], wdn=wdn_all[l],
                  fox_bf=_pad_lanes(fox_bf[l][None], 128), rwkv_mu=rwkv_mu[l][None],
                  rwkv_vec=_rows8(rwkv_w0[l], rwkv_a0[l], rwkv_kk[l], rwkv_ka[l], rwkv_rk[l], rwkv_ln_w[l], rwkv_ln_b[l],
                                  width=W_BRANCH),
                  rwkv_wa=wa_all[l], rwkv_g2=g2_all[l], hgrn_lb=lb_all[l], hgrn_nw=hgrn_norm_w[l][None],
                  ln1=_rows8(ln1_g[l], ln1_b[l], width=D_MODEL), ln2=_rows8(ln2_g[l], ln2_b[l], width=D_MODEL),
                  conv=_rows8(ffn_conv_w[l, 0], ffn_conv_w[l, 1], ffn_conv_w[l, 2], ffn_conv_b[l], width=D_FF))
        xp, st_p = _layer(xp, cfg_p, lw)
        cfg_l = dict(cfg_s, layer=l,
                     shift_rows=_expand_first_rows(state_rwkv_shift[l][:, None], tsp, ((0, 0),)),
                     rwkv_state=state_rwkv[l].reshape(bs * W_BRANCH, HEAD_DIM),
                     hgrn_state=state_hgrn[l].reshape(bs * W_BRANCH, HEAD_DIM),
                     conv_p1=_expand_first_rows(state_ffn_conv[l], tsp, ((1, 0),)),
                     conv_p2=_expand_first_rows(state_ffn_conv[l], tsp, ((0, 0), (1, 1))))
        xs, st_s = _layer(xs, cfg_l, lw)
        new_p.append(st_p)
        new_s.append(st_s)

    def assemble(new, b, tt, tv, decode):
        def rows(name, shape):
            a = jnp.stack([n[name] for n in new]).reshape(depth, b, tt, -1)[:, :, :tv]
            return a.reshape((depth, b, tv) + shape)

        fox_kv = rows('fox_kv', (2, N_HEAD, HEAD_DIM))
        fox_logf = rows('fox_logf', (N_HEAD,))
        dsa_kv = rows('dsa_kv', (2, N_HEAD, HEAD_DIM))
        dsa_kidx = rows('dsa_kidx', (D_IDX,))
        rwkv = jnp.stack([n['rwkv'] for n in new]).reshape(depth, b, N_HEAD, HEAD_DIM, HEAD_DIM)
        hgrn = jnp.stack([n['hgrn'] for n in new]).reshape(depth, b, N_HEAD, HEAD_DIM, HEAD_DIM)
        shift = jnp.stack([n['shift'] for n in new])
        conv = jnp.stack([n['conv'] for n in new])
        if decode:
            conv = conv.reshape(depth, b, tt, D_FF)[:, :, tv - 2:tv]
        else:
            conv = conv.reshape(depth, b, -1, 8, D_FF)[:, :, -1, 6:8]
        return fox_kv, fox_logf, dsa_kv, dsa_kidx, rwkv, shift, hgrn, conv

    y_p = xp.reshape(bp, tp, d)
    y_s = xs.reshape(bs, tsp, d)[:, :ts]
    return (y_p, y_s) + assemble(new_p, bp, tp, tp, False) + assemble(new_s, bs, tsp, ts, True)


def _pad_rows_nd(x, rows):
    pad = [(0, 0)] * x.ndim
    pad[-2] = (0, rows - x.shape[-2])
    return jnp.pad(x, pad)
```
